```python
import jax
import jax.numpy as jnp
from jax import lax
import numpy as np

D_MODEL = 1024
BATCH = 16
SEQ = 4096
DEPTH = 4

GRID_W = 64
CTX_LEN = 256
HEAD_DIM = 64
BLOCK = 128
ROPE_THETA = 10000.0
NEG_INF = -1e30
EPS = 1e-6

NA_HEADS = 4
NA_ROWS = 8
NA_COLS = 16
GQA_HEADS = 4
GQA_KV_HEADS = 2
WIN_HEADS = 4
WIN_KV_HEADS = 2
WINDOW = 128
MLA_HEADS = 4
MLA_Q_RANK = 256
MLA_KV_RANK = 128
MLA_NOPE = 64
MLA_ROPE = 32
MLA_V = 64

D_MIX = NA_HEADS * HEAD_DIM + GQA_HEADS * HEAD_DIM + WIN_HEADS * HEAD_DIM + MLA_HEADS * MLA_V
PROJ_SPLITS = ((NA_HEADS * HEAD_DIM,) * 3
               + (GQA_HEADS * HEAD_DIM, GQA_KV_HEADS * HEAD_DIM, GQA_KV_HEADS * HEAD_DIM)
               + (WIN_HEADS * HEAD_DIM, WIN_KV_HEADS * HEAD_DIM, WIN_KV_HEADS * HEAD_DIM)
               + (MLA_Q_RANK, MLA_KV_RANK, MLA_ROPE))
D_PROJ = sum(PROJ_SPLITS)
PROJ_OFFSETS = tuple(int(o) for o in np.cumsum(PROJ_SPLITS)[:-1])

N_GROUPS = 4
EXPERTS_PER_GROUP = 4
N_EXPERTS = N_GROUPS * EXPERTS_PER_GROUP
TOP_K = 2
D_EXPERT = 512

ALPHA = (2.0 * DEPTH) ** 0.25
BETA = (8.0 * DEPTH) ** -0.25

kernel_name = 'hybrid_dit_parallel_heads_hmoe'


def _layer_norm(x, g=None, b=None):
    xf = x.astype(jnp.float32)
    mu = jnp.mean(xf, -1, keepdims=True)
    var = jnp.mean(jnp.square(xf - mu), -1, keepdims=True)
    y = (xf - mu) * lax.rsqrt(var + EPS)
    if g is not None:
        y = y * g.astype(jnp.float32) + b.astype(jnp.float32)
    return y.astype(x.dtype)


def _rms_norm(x, g):
    xf = x.astype(jnp.float32)
    y = xf * lax.rsqrt(jnp.mean(xf * xf, -1, keepdims=True) + EPS) * g.astype(jnp.float32)
    return y.astype(x.dtype)


def _heads(z, n):
    return z.reshape(z.shape[0], z.shape[1], n, -1)


def _rope_1d(x, pos):
    d = x.shape[-1]
    freq = ROPE_THETA ** (-jnp.arange(0, d, 2, dtype=jnp.float32) / d)
    ang = pos.astype(jnp.float32)[:, None] * freq[None, :]
    cos = jnp.cos(ang)[:, None, :]
    sin = jnp.sin(ang)[:, None, :]
    x1 = x[..., : d // 2].astype(jnp.float32)
    x2 = x[..., d // 2:].astype(jnp.float32)
    return jnp.concatenate([x1 * cos - x2 * sin, x1 * sin + x2 * cos], -1).astype(x.dtype)


def _rope_2d(x, row, col):
    h = x.shape[-1] // 2
    return jnp.concatenate([_rope_1d(x[..., :h], row), _rope_1d(x[..., h:], col)], -1)


def _gqa_dense(q, k, v, sink=None):
    b, tq, hq, dk = q.shape
    hkv, dv = k.shape[2], v.shape[-1]
    g = hq // hkv
    scale = dk ** -0.5
    qb = q.reshape(b, tq // BLOCK, BLOCK, hkv, g, dk).transpose(1, 0, 2, 3, 4, 5)

    def one(qi):
        s = jnp.einsum('bqkgd,bskd->bkgqs', qi, k).astype(jnp.float32) * scale
        if sink is not None:
            s_sink = jnp.broadcast_to(sink.astype(jnp.float32).reshape(1, hkv, g, 1, 1), s.shape[:-1] + (1,))
            p = jax.nn.softmax(jnp.concatenate([s, s_sink], -1), -1)[..., :-1]
        else:
            p = jax.nn.softmax(s, -1)
        return jnp.einsum('bkgqs,bskd->bqkgd', p.astype(v.dtype), v)

    o = lax.map(one, qb)
    return o.transpose(1, 0, 2, 3, 4, 5).reshape(b, tq, hq * dv)


def _window_attn(q, k, v, kc, vc, sink):
    b, t, hq, dk = q.shape
    hkv, dv = k.shape[2], v.shape[-1]
    g = hq // hkv
    n_blk = t // BLOCK
    span = BLOCK + 2 * WINDOW
    scale = dk ** -0.5
    pad = ((0, 0), (WINDOW, WINDOW), (0, 0), (0, 0))
    kp = jnp.pad(k, pad)
    vp = jnp.pad(v, pad)
    k_rel = jnp.arange(span) - WINDOW
    band = jnp.abs(jnp.arange(BLOCK)[:, None] - k_rel[None, :]) <= WINDOW
    sink_l = sink.astype(jnp.float32).reshape(1, hkv, g, 1, 1)
    qb = q.reshape(b, n_blk, BLOCK, hkv, g, dk).transpose(1, 0, 2, 3, 4, 5)

    def one(args):
        n, qi = args
        start = n * BLOCK
        kw = lax.dynamic_slice_in_dim(kp, start, span, axis=1)
        vw = lax.dynamic_slice_in_dim(vp, start, span, axis=1)
        k_abs = start + k_rel
        mask = band & ((k_abs >= 0) & (k_abs < t))[None, :]
        s_loc = jnp.einsum('bqkgd,bskd->bkgqs', qi, kw).astype(jnp.float32) * scale
        s_loc = jnp.where(mask, s_loc, NEG_INF)
        s_ctx = jnp.einsum('bqkgd,blkd->bkgql', qi, kc).astype(jnp.float32) * scale
        s_sink = jnp.broadcast_to(sink_l, s_ctx.shape[:-1] + (1,))
        p = jax.nn.softmax(jnp.concatenate([s_loc, s_ctx, s_sink], -1), -1).astype(v.dtype)
        return (jnp.einsum('bkgqs,bskd->bqkgd', p[..., :span], vw)
                + jnp.einsum('bkgql,blkd->bqkgd', p[..., span:-1], vc))

    o = lax.map(one, (jnp.arange(n_blk), qb))
    return o.transpose(1, 0, 2, 3, 4, 5).reshape(b, t, hq * dv)


def _neighbourhood_attn(q, k, v, kc, vc, bias):
    b, t, h, d = q.shape
    rows = t // GRID_W
    kr = min(NA_ROWS, rows)
    nk = kr * NA_COLS
    scale = d ** -0.5
    cols = jnp.arange(GRID_W)
    c0 = jnp.clip(cols - NA_COLS // 2, 0, GRID_W - NA_COLS)
    key_col = c0[:, None, None] + jnp.arange(NA_COLS)[None, None, :]
    col_off = key_col - cols[:, None, None] + (NA_COLS - 1)
    dr = jnp.arange(kr)
    qb = q.reshape(b, rows, GRID_W, h, d).transpose(1, 0, 2, 3, 4)

    def one(args):
        r, qi = args
        r0 = jnp.clip(r - kr // 2, 0, rows - kr)
        key_row = r0 + dr
        idx = (key_row[None, :, None] * GRID_W + key_col).reshape(GRID_W, nk)
        rb = bias[:, (key_row - r + NA_ROWS - 1)[None, :, None], col_off].reshape(h, GRID_W, nk)
        kg = jnp.take(k, idx, axis=1)
        vg = jnp.take(v, idx, axis=1)
        s_loc = jnp.einsum('bqhd,bqkhd->bhqk', qi, kg).astype(jnp.float32) * scale + rb.astype(jnp.float32)[None]
        s_ctx = jnp.einsum('bqhd,blhd->bhql', qi, kc).astype(jnp.float32) * scale
        p = jax.nn.softmax(jnp.concatenate([s_loc, s_ctx], -1), -1).astype(v.dtype)
        return (jnp.einsum('bhqk,bqkhd->bqhd', p[..., :nk], vg)
                + jnp.einsum('bhql,blhd->bqhd', p[..., nk:], vc))

    o = lax.map(one, (jnp.arange(rows), qb))
    return o.transpose(1, 0, 2, 3, 4).reshape(b, t, h * d)


def _mla_q(cq, gain, w_qb):
    return _heads(_rms_norm(cq, gain) @ w_qb, MLA_HEADS)


def _mla_kv(ckv, gain, w_kvb):
    kv = _heads(_rms_norm(ckv, gain) @ w_kvb, MLA_HEADS)
    return kv[..., :MLA_NOPE], kv[..., MLA_NOPE:]


def _mla_key(k_nope, k_rope):
    return jnp.concatenate([k_nope, jnp.broadcast_to(k_rope, k_nope.shape[:3] + (MLA_ROPE,))], -1)


def _token_mix(h, hc, row, col, w_in, na_bias, gqa_q_gain, gqa_k_gain, win_sink,
               mla_q_gain, mla_w_qb, mla_kv_gain, mla_w_kvb, w_out, with_ctx):
    aq, ak, av, bq, bk, bv, cq, ck, cv, dq, dkv, dkr = jnp.split(h @ w_in, PROJ_OFFSETS, axis=-1)
    aqc, akc, avc, bqc, bkc, bvc, cqc, ckc, cvc, dqc, dkvc, dkrc = jnp.split(hc @ w_in, PROJ_OFFSETS, axis=-1)

    ka_c, va_c = _heads(akc, NA_HEADS), _heads(avc, NA_HEADS)
    out_a = _neighbourhood_attn(_heads(aq, NA_HEADS), _heads(ak, NA_HEADS), _heads(av, NA_HEADS),
                                ka_c, va_c, na_bias)

    qb = _rope_2d(_rms_norm(_heads(bq, GQA_HEADS), gqa_q_gain), row, col)
    kb = _rope_2d(_rms_norm(_heads(bk, GQA_KV_HEADS), gqa_k_gain), row, col)
    kb_c = _rms_norm(_heads(bkc, GQA_KV_HEADS), gqa_k_gain)
    vb_c = _heads(bvc, GQA_KV_HEADS)
    out_b = _gqa_dense(qb, jnp.concatenate([kb, kb_c], 1),
                       jnp.concatenate([_heads(bv, GQA_KV_HEADS), vb_c], 1))

    kc_c, vc_c = _heads(ckc, WIN_KV_HEADS), _heads(cvc, WIN_KV_HEADS)
    out_c = _window_attn(_rope_2d(_heads(cq, WIN_HEADS), row, col),
                         _rope_2d(_heads(ck, WIN_KV_HEADS), row, col),
                         _heads(cv, WIN_KV_HEADS), kc_c, vc_c, win_sink)

    qd = _mla_q(dq, mla_q_gain, mla_w_qb)
    qd = jnp.concatenate([qd[..., :MLA_NOPE], _rope_2d(qd[..., MLA_NOPE:], row, col)], -1)
    kn, vd = _mla_kv(dkv, mla_kv_gain, mla_w_kvb)
    kd = _mla_key(kn, _rope_2d(dkr[:, :, None, :], row, col))
    kn_c, vd_c = _mla_kv(dkvc, mla_kv_gain, mla_w_kvb)
    kd_c = _mla_key(kn_c, dkrc[:, :, None, :])
    out_d = _gqa_dense(qd, jnp.concatenate([kd, kd_c], 1), jnp.concatenate([vd, vd_c], 1))

    mix = jnp.concatenate([out_a, out_b, out_c, out_d], -1) @ w_out
    if not with_ctx:
        return mix, None
    mix_c = jnp.concatenate([
        _gqa_dense(_heads(aqc, NA_HEADS), ka_c, va_c),
        _gqa_dense(_rms_norm(_heads(bqc, GQA_HEADS), gqa_q_gain), kb_c, vb_c),
        _gqa_dense(_heads(cqc, WIN_HEADS), kc_c, vc_c, win_sink),
        _gqa_dense(_mla_q(dqc, mla_q_gain, mla_w_qb), kd_c, vd_c),
    ], -1) @ w_out
    return mix, mix_c


def _hier_moe(h, wg, bg, we, be, w_gate, w_up, w_down):
    shp = h.shape
    hf = h.reshape(-1, shp[-1])
    lg = (hf @ wg).astype(jnp.float32) + bg.astype(jnp.float32)
    grp = jnp.argmax(lg, -1)
    p_grp = jnp.max(jax.nn.softmax(lg, -1), -1, keepdims=True)
    le = ((hf @ we).astype(jnp.float32) + be.astype(jnp.float32)).reshape(-1, N_GROUPS, EXPERTS_PER_GROUP)
    le_sel = jnp.take_along_axis(le, grp[:, None, None], axis=1)[:, 0]
    top_v, top_i = lax.top_k(le_sel, TOP_K)
    w_top = jax.nn.softmax(top_v, -1) * p_grp
    eid = grp[:, None] * EXPERTS_PER_GROUP + top_i
    gate = jnp.einsum('nke,nk->ne', jax.nn.one_hot(eid, N_EXPERTS, dtype=jnp.float32), w_top)
    y = jnp.zeros(hf.shape, jnp.float32)
    for e in range(N_EXPERTS):
        a = jax.nn.silu(hf @ w_gate[e]) * (hf @ w_up[e])
        y = y + gate[:, e:e + 1] * (a @ w_down[e]).astype(jnp.float32)
    return y.astype(h.dtype).reshape(shp)


def setup_inputs(seed: int = 0) -> dict:
    key = jax.random.key(seed)
    ks = iter(jax.random.split(key, 32))

    def nrm(shape, s):
        return jax.random.normal(next(ks), shape, jnp.float32) * s

    L, D = DEPTH, D_MODEL
    return {
        'x': nrm((BATCH, SEQ, D), 1.0),
        'c': nrm((BATCH, D), 1.0),
        'ctx': nrm((BATCH, CTX_LEN, D), 1.0),
        'c_ctx': nrm((D,), 1.0),
        'w_ada': nrm((L, D, 6 * D), 0.5 * D ** -0.5),
        'b_ada': nrm((L, 6 * D), 0.02),
        'w_in': nrm((L, D, D_PROJ), D ** -0.5),
        'na_bias': nrm((L, NA_HEADS, 2 * NA_ROWS - 1, 2 * NA_COLS - 1), 0.1),
        'gqa_q_gain': 1.0 + nrm((L, HEAD_DIM), 0.02),
        'gqa_k_gain': 1.0 + nrm((L, HEAD_DIM), 0.02),
        'win_sink': nrm((L, WIN_HEADS), 0.5),
        'mla_q_gain': 1.0 + nrm((L, MLA_Q_RANK), 0.02),
        'mla_w_qb': nrm((L, MLA_Q_RANK, MLA_HEADS * (MLA_NOPE + MLA_ROPE)), MLA_Q_RANK ** -0.5),
        'mla_kv_gain': 1.0 + nrm((L, MLA_KV_RANK), 0.02),
        'mla_w_kvb': nrm((L, MLA_KV_RANK, MLA_HEADS * (MLA_NOPE + MLA_V)), MLA_KV_RANK ** -0.5),
        'w_out': nrm((L, D_MIX, D), BETA * D_MIX ** -0.5),
        'ln1_g': 1.0 + nrm((L, D), 0.02),
        'ln1_b': nrm((L, D), 0.02),
        'router_group_w': nrm((L, D, N_GROUPS), D ** -0.5),
        'router_group_b': nrm((L, N_GROUPS), 0.01),
        'router_expert_w': nrm((L, D, N_EXPERTS), D ** -0.5),
        'router_expert_b': nrm((L, N_EXPERTS), 0.01),
        'moe_w_gate': nrm((L, N_EXPERTS, D, D_EXPERT), D ** -0.5),
        'moe_w_up': nrm((L, N_EXPERTS, D, D_EXPERT), D ** -0.5),
        'moe_w_down': nrm((L, N_EXPERTS, D_EXPERT, D), BETA * D_EXPERT ** -0.5),
        'ln2_g': 1.0 + nrm((L, D), 0.02),
        'ln2_b': nrm((L, D), 0.02),
    }


def reference(x, c, ctx, c_ctx, w_ada, b_ada, w_in, na_bias, gqa_q_gain, gqa_k_gain, win_sink,
              mla_q_gain, mla_w_qb, mla_kv_gain, mla_w_kvb, w_out, ln1_g, ln1_b,
              router_group_w, router_group_b, router_expert_w, router_expert_b,
              moe_w_gate, moe_w_up, moe_w_down, ln2_g, ln2_b):
    t = x.shape[1]
    pos = jnp.arange(t)
    row = pos // GRID_W
    col = pos % GRID_W
    xc = ctx
    for l in range(DEPTH):
        with_ctx = l < DEPTH - 1
        mod = jax.nn.silu(c) @ w_ada[l] + b_ada[l]
        sh1, sc1, g1, sh2, sc2, g2 = jnp.split(mod[:, None, :], 6, axis=-1)
        sh1c, sc1c, g1c, sh2c, sc2c, g2c = jnp.split(jax.nn.silu(c_ctx) @ w_ada[l] + b_ada[l], 6, axis=-1)

        h = _layer_norm(x) * (1 + sc1) + sh1
        hc = _layer_norm(xc) * (1 + sc1c) + sh1c
        mix, mix_c = _token_mix(h, hc, row, col, w_in[l], na_bias[l], gqa_q_gain[l], gqa_k_gain[l],
                                win_sink[l], mla_q_gain[l], mla_w_qb[l], mla_kv_gain[l], mla_w_kvb[l],
                                w_out[l], with_ctx)
        x = _layer_norm(ALPHA * x + g1 * mix, ln1_g[l], ln1_b[l])
        moe_p = (router_group_w[l], router_group_b[l], router_expert_w[l], router_expert_b[l],
                 moe_w_gate[l], moe_w_up[l], moe_w_down[l])
        x = _layer_norm(ALPHA * x + g2 * _hier_moe(_layer_norm(x) * (1 + sc2) + sh2, *moe_p),
                        ln2_g[l], ln2_b[l])
        if with_ctx:
            xc = _layer_norm(ALPHA * xc + g1c * mix_c, ln1_g[l], ln1_b[l])
            xc = _layer_norm(ALPHA * xc + g2c * _hier_moe(_layer_norm(xc) * (1 + sc2c) + sh2c, *moe_p),
                             ln2_g[l], ln2_b[l])
    return x
```

```python
import functools

import jax
import jax.numpy as jnp
from jax import lax
from jax.experimental import pallas as pl
from jax.experimental.pallas import tpu as pltpu

F32 = jnp.float32
BF16 = jnp.bfloat16

GRID_W = 64
HEAD_DIM = 64
BLOCK = 128
WINDOW = 128
ROPE_THETA = 10000.0
NEG = -1e30
EPS = 1e-6
LOG2E = 1.4426950408889634

NA_ROWS = 8
NA_COLS = 16
NBR_SPAN_ROWS = 10
MLA_NOPE = 64
MLA_ROPE = 32
MLA_QK = MLA_NOPE + MLA_ROPE
N_GROUPS = 4
EXPERTS_PER_GROUP = 4
N_EXPERTS = 16
N_PAIRS = 6
N_BUCKETS = N_GROUPS * N_PAIRS

LANES = 128
VMEM_LIMIT = 48 * 1024 * 1024

ROW_TILE = 256
MOE_TILE = 256
KV_CHUNK = 512


def _cparams(*sem):
    return pltpu.CompilerParams(dimension_semantics=sem, vmem_limit_bytes=VMEM_LIMIT)


def _dot(a, b):
    return jnp.dot(a, b, preferred_element_type=F32)


def _dot_nt(a, b):
    return lax.dot_general(a, b, (((1,), (1,)), ((), ())), preferred_element_type=F32)


def _split(a):
    hi = a.astype(BF16)
    lo = (a - hi.astype(F32)).astype(BF16)
    return hi, lo


def _dot3(a, b):
    ah, al = _split(a)
    bh, bl = _split(b)
    return _dot(al, bh) + _dot(ah, bl) + _dot(ah, bh)


def _ln(x):
    mu = jnp.mean(x, axis=-1, keepdims=True)
    xc = x - mu
    var = jnp.mean(xc * xc, axis=-1, keepdims=True)
    return xc * lax.rsqrt(var + EPS)


def _silu(g):
    return g / (1.0 + jnp.exp(-g))


def _ada_kernel(c_ref, w_ref, b_ref, o_ref):
    o_ref[0] = _dot3(_silu(c_ref[...]), w_ref[0]) + b_ref[0]


def _ada(cs, w_ada, b_ada):
    depth, d, n = w_ada.shape
    rows = cs.shape[0]
    tn = 1536
    return pl.pallas_call(
        _ada_kernel,
        grid=(depth, n // tn),
        in_specs=[pl.BlockSpec((rows, d), lambda l, j: (0, 0)),
                  pl.BlockSpec((1, d, tn), lambda l, j: (l, 0, j)),
                  pl.BlockSpec((1, 1, tn), lambda l, j: (l, 0, j))],
        out_specs=pl.BlockSpec((1, rows, tn), lambda l, j: (l, 0, j)),
        out_shape=jax.ShapeDtypeStruct((depth, rows, n), F32),
        compiler_params=_cparams("parallel", "parallel"),
    )(cs, w_ada, b_ada.reshape(depth, 1, n))


def _head_rms(x, bd, gain):
    hi, lo = _split(x * x)
    ss = _dot(lo, bd) + _dot(hi, bd)
    return x * lax.rsqrt(ss * (1.0 / HEAD_DIM) + EPS) * gain


def _rms(x, gain):
    return x * lax.rsqrt(jnp.mean(x * x, axis=-1, keepdims=True) + EPS) * gain


def _rope(x, tab_ref, shift):
    outs = []
    for j in range(x.shape[1] // LANES):
        xb = x[:, j * LANES:(j + 1) * LANES]
        outs.append(xb * tab_ref[0]
                    + pltpu.roll(xb, shift, 1) * tab_ref[1]
                    + pltpu.roll(xb, LANES - shift, 1) * tab_ref[2])
    return outs[0] if len(outs) == 1 else jnp.concatenate(outs, axis=1)


def _proj_kernel(rope, *refs):
    (x_ref, sc_ref, sh_ref, w_ref, gq_ref, gk_ref, gmq_ref, gmkv_ref,
     wqb_ref, wkvb_ref, bd_ref) = refs[:11]
    refs = refs[11:]
    if rope:
        t64_ref, tq_ref, tk_ref = refs[:3]
        refs = refs[3:]
    qa, ka, va, qb, kb, vb, qc, kc, vc, qd, kd, vd = refs

    h = _ln(x_ref[...]) * (1.0 + sc_ref[0]) + sh_ref[0]
    hb = h.astype(BF16)
    sq = HEAD_DIM ** -0.5 * LOG2E

    def proj(a, b):
        return _dot(hb, w_ref[:, a:b])

    pa = proj(0, 768)
    qa[...] = (pa[:, 0:256] * sq).astype(BF16)
    ka[...] = pa[:, 256:512].astype(BF16)
    va[...] = pa[:, 512:768].astype(BF16)

    pb = proj(768, 1280)
    q = _head_rms(pb[:, 0:256], bd_ref[...], gq_ref[...])
    k = _head_rms(pb[:, 256:384], bd_ref[0:128, 0:128], gk_ref[...])
    if rope:
        q = _rope(q, t64_ref, 16)
        k = _rope(k, t64_ref, 16)
    qb[...] = (q * sq).astype(BF16)
    kb[...] = k.astype(BF16)
    vb[...] = pb[:, 384:512].astype(BF16)

    pc = proj(1280, 1792)
    q = pc[:, 0:256]
    k = pc[:, 256:384]
    if rope:
        q = _rope(q, t64_ref, 16)
        k = _rope(k, t64_ref, 16)
    qc[...] = (q * sq).astype(BF16)
    kc[...] = k.astype(BF16)
    vc[...] = pc[:, 384:512].astype(BF16)

    pd = proj(1792, 2304)
    cq = _rms(pd[:, 0:256], gmq_ref[...])
    q = _dot(cq.astype(BF16), wqb_ref[...])
    if rope:
        q = _rope(q, tq_ref, 8)
    qd[...] = (q * (MLA_QK ** -0.5 * LOG2E)).astype(BF16)
    ckv = _rms(pd[:, 256:384], gmkv_ref[...])
    kvu = _dot(ckv.astype(BF16), wkvb_ref[...])
    kr = pd[:, 384:512]
    if rope:
        kr = _rope(kr, tk_ref, 8)
    kr = pltpu.roll(kr, MLA_NOPE, 1)
    kd[...] = jnp.concatenate(
        [kvu[:, j * LANES:(j + 1) * LANES] + kr for j in range(4)], axis=1).astype(BF16)
    vd[...] = kvu[:, 512:768].astype(BF16)


_PROJ_WIDTHS = (256, 256, 256, 256, 128, 128, 256, 128, 128, 512, 512, 256)


def _project(x2d, sc, sh, lw, tabs, tokens_per_batch):
    n, d = x2d.shape
    tm = ROW_TILE
    per_b = tokens_per_batch // tm
    rope = tabs is not None
    row = lambda i: (i, 0)
    full = lambda i: (0, 0)
    perb = lambda i: (i // per_b, 0, 0)
    in_specs = [pl.BlockSpec((tm, d), row),
                pl.BlockSpec((1, 1, d), perb), pl.BlockSpec((1, 1, d), perb),
                pl.BlockSpec(lw["w_in"].shape, full),
                pl.BlockSpec((1, 256), full), pl.BlockSpec((1, 128), full),
                pl.BlockSpec((1, 256), full), pl.BlockSpec((1, 128), full),
                pl.BlockSpec(lw["w_qb"].shape, full), pl.BlockSpec(lw["w_kvb"].shape, full),
                pl.BlockSpec((256, 256), full)]
    args = [x2d, sc, sh, lw["w_in"], lw["gq"], lw["gk"], lw["gmq"], lw["gmkv"],
            lw["w_qb"], lw["w_kvb"], lw["bd"]]
    if rope:
        tab = lambda i: (0, i % per_b, 0)
        in_specs += [pl.BlockSpec((3, tm, LANES), tab)] * 3
        args += list(tabs)
    return pl.pallas_call(
        functools.partial(_proj_kernel, rope),
        grid=(n // tm,),
        in_specs=in_specs,
        out_specs=[pl.BlockSpec((tm, w), row) for w in _PROJ_WIDTHS],
        out_shape=[jax.ShapeDtypeStruct((n, w), BF16) for w in _PROJ_WIDTHS],
        compiler_params=_cparams("parallel"),
    )(*args)


def _online(s, v, m, l, acc):
    m_new = jnp.maximum(m, jnp.max(s, axis=1, keepdims=True))
    alpha = jnp.exp2(m - m_new)
    p = jnp.exp2(s - m_new)
    l = alpha * l + jnp.sum(p, axis=1, keepdims=True)
    acc = alpha * acc + _dot(p.astype(BF16), v)
    return m_new, l, acc


def _attn_kernel(mode, bq, srcs, has_bias, has_sink, *refs):
    refs = list(refs)
    q_ref = refs.pop(0)
    kv = [(refs.pop(0), refs.pop(0)) for _ in srcs]
    bias_ref = refs.pop(0) if has_bias else None
    sink_ref = refs.pop(0) if has_sink else None
    o_ref = refs.pop(0)

    qi = pl.program_id(1)
    rows2 = 2 * bq
    lo = lax.broadcasted_iota(jnp.int32, (bq, LANES), 1) < HEAD_DIM

    for j in range(2):
        if mode == "mla":
            qblk = q_ref[0, :, 2 * LANES * j:2 * LANES * (j + 1)].astype(F32)
            first = lax.broadcasted_iota(jnp.int32, (bq, 2 * LANES), 1) < LANES
            qs = jnp.concatenate([jnp.where(first, qblk, 0.0), jnp.where(first, 0.0, qblk)], axis=0)
            kc0, kc1 = 2 * LANES * j, 2 * LANES * (j + 1)
            vc0 = LANES * j
        else:
            qblk = q_ref[0, :, LANES * j:LANES * (j + 1)].astype(F32)
            if mode == "mha":
                qe, qo = jnp.where(lo, qblk, 0.0), jnp.where(lo, 0.0, qblk)
                kc0 = vc0 = LANES * j
            else:
                rolled = pltpu.roll(qblk, HEAD_DIM, 1)
                if j == 0:
                    qe, qo = jnp.where(lo, qblk, 0.0), jnp.where(lo, rolled, 0.0)
                else:
                    qe, qo = jnp.where(lo, 0.0, rolled), jnp.where(lo, 0.0, qblk)
                kc0 = vc0 = 0
            kc1 = kc0 + LANES
            qs = jnp.concatenate([qe, qo], axis=0)
        qs = qs.astype(BF16)

        m = jnp.full((rows2, 1), NEG, F32)
        l = jnp.zeros((rows2, 1), F32)
        acc = jnp.zeros((rows2, LANES), F32)

        for (kind, nrows, tk), (k_ref, v_ref) in zip(srcs, kv):
            if kind == "full":
                def step(c, carry, k_ref=k_ref, v_ref=v_ref, tk=tk):
                    off = c * tk if isinstance(c, int) else pl.multiple_of(c * tk, tk)
                    kt = k_ref[0, pl.ds(off, tk), kc0:kc1]
                    vt = v_ref[0, pl.ds(off, tk), vc0:vc0 + LANES]
                    return _online(_dot_nt(qs, kt), vt, *carry)
                if nrows // tk == 1:
                    m, l, acc = step(0, (m, l, acc))
                else:
                    m, l, acc = lax.fori_loop(0, nrows // tk, step, (m, l, acc))
            elif kind == "win":
                span = bq + 2 * WINDOW
                start = pl.multiple_of(jnp.clip((qi - 1) * bq, 0, nrows - span), bq)
                kt = k_ref[0, pl.ds(start, span), kc0:kc1]
                vt = v_ref[0, pl.ds(start, span), vc0:vc0 + LANES]
                s = _dot_nt(qs, kt)
                r = lax.broadcasted_iota(jnp.int32, (rows2, span), 0)
                qpos = qi * bq + jnp.where(r >= bq, r - bq, r)
                kpos = start + lax.broadcasted_iota(jnp.int32, (rows2, span), 1)
                s = jnp.where(jnp.abs(qpos - kpos) <= WINDOW, s, NEG)
                m, l, acc = _online(s, vt, m, l, acc)
            else:
                span = NBR_SPAN_ROWS * GRID_W
                ks = jnp.clip(2 * qi - NA_ROWS // 2, 0, nrows // GRID_W - NBR_SPAN_ROWS)
                start = pl.multiple_of(ks * GRID_W, GRID_W)
                kt = k_ref[0, pl.ds(start, span), kc0:kc1]
                vt = v_ref[0, pl.ds(start, span), vc0:vc0 + LANES]
                s = _dot_nt(qs, kt) + bias_ref[0, j]
                m, l, acc = _online(s, vt, m, l, acc)

        if has_sink:
            sk = jnp.concatenate([jnp.full((bq, 1), sink_ref[2 * j] * LOG2E, F32),
                                  jnp.full((bq, 1), sink_ref[2 * j + 1] * LOG2E, F32)], axis=0)
            mf = jnp.maximum(m, sk)
            a = jnp.exp2(m - mf)
            l = l * a + jnp.exp2(sk - mf)
            acc = acc * a
        o = acc / l
        oe, oo = o[:bq], o[bq:]
        if mode == "gqa":
            if j == 0:
                oo = pltpu.roll(oo, HEAD_DIM, 1)
            else:
                oe = pltpu.roll(oe, HEAD_DIM, 1)
        o_ref[0, :, LANES * j:LANES * (j + 1)] = jnp.where(lo, oe, oo).astype(BF16)


def _attention(mode, q, sources, bias=None, sink=None):
    b, tq, wq = q.shape
    bq = BLOCK
    srcs = []
    in_specs = [pl.BlockSpec((1, bq, wq), lambda bi, qi: (bi, qi, 0))]
    args = [q]
    for kind, k, v in sources:
        s = k.shape[1]
        tk = KV_CHUNK if (kind == "full" and s % KV_CHUNK == 0) else s
        srcs.append((kind, s, tk))
        in_specs += [pl.BlockSpec((1, s, k.shape[2]), lambda bi, qi: (bi, 0, 0)),
                     pl.BlockSpec((1, s, v.shape[2]), lambda bi, qi: (bi, 0, 0))]
        args += [k, v]
    if bias is not None:
        npair = tq // bq

        def bias_idx(bi, qi):
            return (jnp.where(qi < 2, qi, jnp.where(qi >= npair - 2, qi - (npair - 2) + 3, 2)), 0, 0, 0)
        in_specs.append(pl.BlockSpec((1,) + bias.shape[1:], bias_idx))
        args.append(bias)
    if sink is not None:
        in_specs.append(pl.BlockSpec(memory_space=pltpu.SMEM))
        args.append(sink)
    return pl.pallas_call(
        functools.partial(_attn_kernel, mode, bq, tuple(srcs), bias is not None, sink is not None),
        grid=(b, tq // bq),
        in_specs=in_specs,
        out_specs=pl.BlockSpec((1, bq, 256), lambda bi, qi: (bi, qi, 0)),
        out_shape=jax.ShapeDtypeStruct((b, tq, 256), BF16),
        compiler_params=_cparams("parallel", "arbitrary"),
    )(*args)


def _route(logits):
    lane = lax.broadcasted_iota(jnp.int32, logits.shape, 1).astype(F32)
    big = float(LANES)
    is_g = lane < N_GROUPS
    lg = jnp.where(is_g, logits, NEG)
    gmax = jnp.max(lg, axis=1, keepdims=True)
    grp = jnp.min(jnp.where(lg == gmax, lane, big), axis=1, keepdims=True)
    den = jnp.sum(jnp.where(is_g, jnp.exp(lg - gmax), 0.0), axis=1, keepdims=True)
    p_grp = 1.0 / den
    e0 = N_GROUPS + EXPERTS_PER_GROUP * grp
    in_grp = (lane >= e0) & (lane < e0 + EXPERTS_PER_GROUP)
    le = jnp.where(in_grp, logits, NEG)
    v1 = jnp.max(le, axis=1, keepdims=True)
    i1 = jnp.min(jnp.where(in_grp & (le == v1), lane, big), axis=1, keepdims=True)
    rest = in_grp & (lane != i1)
    le2 = jnp.where(rest, logits, NEG)
    v2 = jnp.max(le2, axis=1, keepdims=True)
    i2 = jnp.min(jnp.where(rest & (le2 == v2), lane, big), axis=1, keepdims=True)
    t = jnp.exp(v2 - v1)
    w1 = p_grp / (1.0 + t)
    w2 = p_grp * t / (1.0 + t)
    first_lower = i1 < i2
    e_lo = jnp.where(first_lower, i1, i2) - N_GROUPS
    e_hi = jnp.where(first_lower, i2, i1) - N_GROUPS
    w_lo = jnp.where(first_lower, w1, w2)
    w_hi = jnp.where(first_lower, w2, w1)
    return jnp.where(lane == 0, e_lo, jnp.where(lane == 1, e_hi,
                     jnp.where(lane == 2, w_lo, jnp.where(lane == 3, w_hi, 0.0))))


def _out_kernel(alpha, oa, ob, oc, od, x_ref, g1_ref, sc2_ref, sh2_ref, w_ref, lng_ref, lnb_ref,
                wrh_ref, wrl_ref, br_ref, x1_ref, h2_ref, route_ref):
    mix = (_dot(oa[...], w_ref[0:256, :]) + _dot(ob[...], w_ref[256:512, :])
           + _dot(oc[...], w_ref[512:768, :]) + _dot(od[...], w_ref[768:1024, :]))
    x1 = _ln(alpha * x_ref[...] + g1_ref[0] * mix) * lng_ref[...] + lnb_ref[...]
    x1_ref[...] = x1
    h2 = _ln(x1) * (1.0 + sc2_ref[0]) + sh2_ref[0]
    h2_ref[...] = h2.astype(BF16)
    hh, hl = _split(h2)
    logits = _dot(hl, wrh_ref[...]) + _dot(hh, wrl_ref[...]) + _dot(hh, wrh_ref[...]) + br_ref[...]
    route_ref[...] = _route(logits)


def _out_proj(alpha, outs, x2d, g1, sc2, sh2, lw, tokens_per_batch):
    n, d = x2d.shape
    tm = ROW_TILE
    per_b = tokens_per_batch // tm
    row = lambda i: (i, 0)
    full = lambda i: (0, 0)
    perb = lambda i: (i // per_b, 0, 0)
    return pl.pallas_call(
        functools.partial(_out_kernel, alpha),
        grid=(n // tm,),
        in_specs=[pl.BlockSpec((tm, 256), row)] * 4 + [
            pl.BlockSpec((tm, d), row),
            pl.BlockSpec((1, 1, d), perb), pl.BlockSpec((1, 1, d), perb), pl.BlockSpec((1, 1, d), perb),
            pl.BlockSpec((d, d), full), pl.BlockSpec((1, d), full), pl.BlockSpec((1, d), full),
            pl.BlockSpec((d, LANES), full), pl.BlockSpec((d, LANES), full), pl.BlockSpec((1, LANES), full)],
        out_specs=[pl.BlockSpec((tm, d), row), pl.BlockSpec((tm, d), row), pl.BlockSpec((tm, LANES), row)],
        out_shape=[jax.ShapeDtypeStruct((n, d), F32), jax.ShapeDtypeStruct((n, d), BF16),
                   jax.ShapeDtypeStruct((n, LANES), F32)],
        compiler_params=_cparams("parallel"),
    )(*outs, x2d, g1, sc2, sh2, lw["w_out"], lw["ln1_g"], lw["ln1_b"], lw["wr_hi"], lw["wr_lo"], lw["br"])


def _moe_kernel(e0_ref, e1_ref, valid_ref, xs_ref, rt_ref, wg0, wg1, wu0, wu1, wd0, wd1, o_ref):
    t = pl.program_id(0)

    @pl.when(valid_ref[t] == 1)
    def _():
        x = xs_ref[...]

        def expert(wg, wu, wd):
            a = _silu(_dot(x, wg[0])) * _dot(x, wu[0])
            return _dot(a.astype(BF16), wd[0])

        rt = rt_ref[...]
        o_ref[...] = rt[:, 2:3] * expert(wg0, wu0, wd0) + rt[:, 3:4] * expert(wg1, wu1, wd1)

    @pl.when(valid_ref[t] == 0)
    def _():
        o_ref[...] = jnp.zeros_like(o_ref)


def _moe_experts(e0, e1, valid, xs, rts, wg, wu, wd):
    p, d = xs.shape
    tm = MOE_TILE
    de = wg.shape[2]
    row = lambda t, e0, e1, v: (t, 0)
    w0 = lambda t, e0, e1, v: (e0[t], 0, 0)
    w1 = lambda t, e0, e1, v: (e1[t], 0, 0)
    return pl.pallas_call(
        _moe_kernel,
        grid_spec=pltpu.PrefetchScalarGridSpec(
            num_scalar_prefetch=3,
            grid=(p // tm,),
            in_specs=[pl.BlockSpec((tm, d), row), pl.BlockSpec((tm, LANES), row),
                      pl.BlockSpec((1, d, de), w0), pl.BlockSpec((1, d, de), w1),
                      pl.BlockSpec((1, d, de), w0), pl.BlockSpec((1, d, de), w1),
                      pl.BlockSpec((1, de, d), w0), pl.BlockSpec((1, de, d), w1)],
            out_specs=pl.BlockSpec((tm, d), row)),
        out_shape=jax.ShapeDtypeStruct((p, d), F32),
        compiler_params=_cparams("arbitrary"),
    )(e0, e1, valid, xs, rts, wg, wg, wu, wu, wd, wd)


_PAIR_TABLE = ((0, 1), (0, 2), (0, 3), (1, 2), (1, 3), (2, 3))


def _moe(h2, route, wg, wu, wd):
    n = h2.shape[0]
    tm = MOE_TILE
    n_tiles = n // tm + N_BUCKETS
    e_lo = route[:, 0].astype(jnp.int32)
    e_hi = route[:, 1].astype(jnp.int32)
    grp = e_lo // EXPERTS_PER_GROUP
    a = e_lo % EXPERTS_PER_GROUP
    b = e_hi % EXPERTS_PER_GROUP
    pair = a * (2 * EXPERTS_PER_GROUP - a - 1) // 2 + (b - a - 1)
    bucket = grp * N_PAIRS + pair

    tok = jnp.arange(n, dtype=jnp.int32)
    _, order = lax.sort((bucket, tok), num_keys=1)
    counts = jnp.sum(bucket[:, None] == jnp.arange(N_BUCKETS, dtype=jnp.int32)[None, :], axis=0,
                     dtype=jnp.int32)
    tiles_per = (counts + tm - 1) // tm
    tile_end = jnp.cumsum(tiles_per)
    tile_start = tile_end - tiles_per
    src_off = jnp.cumsum(counts) - counts

    tile_ids = jnp.arange(n_tiles, dtype=jnp.int32)
    valid = (tile_ids < tile_end[-1]).astype(jnp.int32)
    tile_bucket = jnp.minimum(jnp.searchsorted(tile_end, tile_ids, side="right").astype(jnp.int32),
                              N_BUCKETS - 1)
    last_bucket = tile_bucket[jnp.maximum(tile_end[-1] - 1, 0)]
    tile_bucket = jnp.where(valid == 1, tile_bucket, last_bucket)
    pairs = jnp.asarray(_PAIR_TABLE, dtype=jnp.int32)
    tg = tile_bucket // N_PAIRS
    e0 = tg * EXPERTS_PER_GROUP + pairs[tile_bucket % N_PAIRS, 0]
    e1 = tg * EXPERTS_PER_GROUP + pairs[tile_bucket % N_PAIRS, 1]

    rows = jnp.arange(n_tiles * tm, dtype=jnp.int32)
    rb = jnp.repeat(tile_bucket, tm)
    within = rows - jnp.repeat(tile_start[tile_bucket], tm) * tm
    row_ok = (within < counts[rb]) & (jnp.repeat(valid, tm) == 1)
    src = jnp.where(row_ok, order[jnp.clip(src_off[rb] + within, 0, n - 1)], 0)

    xs = jnp.take(h2, src, axis=0)
    rts = jnp.where(row_ok[:, None], jnp.take(route, src, axis=0), 0.0)
    ys = _moe_experts(e0, e1, valid, xs, rts, wg, wu, wd)

    rank = jnp.zeros((n,), jnp.int32).at[order].set(tok)
    pos = tile_start[bucket] * tm + rank - src_off[bucket]
    return jnp.take(ys, pos, axis=0)


def _fin_kernel(alpha, x_ref, y_ref, g2_ref, lng_ref, lnb_ref, o_ref):
    o_ref[...] = _ln(alpha * x_ref[...] + g2_ref[0] * y_ref[...]) * lng_ref[...] + lnb_ref[...]


def _finish(alpha, x1, y, g2, ln_g, ln_b, tokens_per_batch):
    n, d = x1.shape
    tm = ROW_TILE
    per_b = tokens_per_batch // tm
    row = lambda i: (i, 0)
    full = lambda i: (0, 0)
    return pl.pallas_call(
        functools.partial(_fin_kernel, alpha),
        grid=(n // tm,),
        in_specs=[pl.BlockSpec((tm, d), row), pl.BlockSpec((tm, d), row),
                  pl.BlockSpec((1, 1, d), lambda i: (i // per_b, 0, 0)),
                  pl.BlockSpec((1, d), full), pl.BlockSpec((1, d), full)],
        out_specs=pl.BlockSpec((tm, d), row),
        out_shape=jax.ShapeDtypeStruct((n, d), F32),
        compiler_params=_cparams("parallel"),
    )(x1, y, g2, ln_g, ln_b)


def _rope_tables(t):
    pos = jnp.arange(t)
    row = (pos // GRID_W).astype(F32)[:, None]
    col = (pos % GRID_W).astype(F32)[:, None]
    lane = jnp.arange(LANES)

    def build(d, half, active):
        axis_col = (d // half) % 2 == 1
        w = d % half
        first = w < half // 2
        f = (w % (half // 2)).astype(F32)
        freq = ROPE_THETA ** (-(2.0 * f) / half)
        ang = jnp.where(axis_col[None, :], col, row) * freq[None, :]
        cos = jnp.where(active[None, :], jnp.cos(ang), 1.0)
        sin = jnp.where(active[None, :], jnp.sin(ang), 0.0)
        return jnp.stack([cos, jnp.where(first[None, :], 0.0, sin), jnp.where(first[None, :], -sin, 0.0)])

    t64 = build(lane % HEAD_DIM, HEAD_DIM // 2, lane >= 0)
    in_rope = (lane >= MLA_NOPE) & (lane < MLA_QK)
    tq = build(jnp.where(in_rope, lane - MLA_NOPE, 0), MLA_ROPE // 2, in_rope)
    in_kr = lane < MLA_ROPE
    tk = build(jnp.where(in_kr, lane, 0), MLA_ROPE // 2, in_kr)
    return t64, tq, tk


def _nbr_bias_tables(na_bias, rows):
    npair = rows // 2
    kr = min(NA_ROWS, rows)
    i = jnp.arange(2 * GRID_W)
    k = jnp.arange(NBR_SPAN_ROWS * GRID_W)
    qcol = (i % GRID_W)[:, None]
    kcol = (k % GRID_W)[None, :]
    c0 = jnp.clip(qcol - NA_COLS // 2, 0, GRID_W - NA_COLS)
    col_ok = (kcol >= c0) & (kcol < c0 + NA_COLS)
    col_off = jnp.clip(kcol - qcol + NA_COLS - 1, 0, 2 * NA_COLS - 2)
    tabs = []
    for p in (0, 1, 2, npair - 2, npair - 1):
        ks = min(max(2 * p - NA_ROWS // 2, 0), rows - NBR_SPAN_ROWS)
        r = (2 * p + i // GRID_W)[:, None]
        krow = (ks + k // GRID_W)[None, :]
        r0 = jnp.clip(r - kr // 2, 0, rows - kr)
        ok = col_ok & (krow >= r0) & (krow < r0 + kr)
        row_off = jnp.clip(krow - r + NA_ROWS - 1, 0, 2 * NA_ROWS - 2)
        vals = na_bias[:, row_off, col_off] * LOG2E
        tabs.append(jnp.where(ok[None], vals, NEG).reshape(2, 2 * 2 * GRID_W, k.shape[0]))
    return jnp.stack(tabs).astype(F32)


def _layer_weights(l, w_in, gqa_q_gain, gqa_k_gain, mla_q_gain, mla_w_qb, mla_kv_gain, mla_w_kvb,
                   w_out, ln1_g, ln1_b, router_group_w, router_group_b, router_expert_w, router_expert_b):
    d = w_in.shape[1]
    d_proj = w_in.shape[2]
    lw = {}
    lw["w_in"] = jnp.pad(w_in[l], ((0, 0), (0, 2304 - d_proj))).astype(BF16)
    lw["gq"] = jnp.tile(gqa_q_gain[l], 4)[None, :]
    lw["gk"] = jnp.tile(gqa_k_gain[l], 2)[None, :]
    lw["gmq"] = mla_q_gain[l][None, :]
    lw["gmkv"] = mla_kv_gain[l][None, :]
    wqb = mla_w_qb[l].reshape(-1, 4, MLA_QK)
    lw["w_qb"] = jnp.pad(wqb, ((0, 0), (0, 0), (0, LANES - MLA_QK))).reshape(-1, 4 * LANES).astype(BF16)
    wkvb = mla_w_kvb[l].reshape(-1, 4, 2 * HEAD_DIM)
    k_part = jnp.pad(wkvb[:, :, :MLA_NOPE], ((0, 0), (0, 0), (0, LANES - MLA_NOPE))).reshape(-1, 4 * LANES)
    v_part = wkvb[:, :, MLA_NOPE:].reshape(-1, 4 * HEAD_DIM)
    lw["w_kvb"] = jnp.concatenate([k_part, v_part], axis=1).astype(BF16)
    idx = jnp.arange(256) // HEAD_DIM
    lw["bd"] = (idx[:, None] == idx[None, :]).astype(BF16)
    lw["w_out"] = w_out[l].astype(BF16)
    lw["ln1_g"] = ln1_g[l][None, :]
    lw["ln1_b"] = ln1_b[l][None, :]
    wr = jnp.concatenate([router_group_w[l], router_expert_w[l]], axis=1)
    wr = jnp.pad(wr, ((0, 0), (0, LANES - wr.shape[1])))
    lw["wr_hi"] = wr.astype(BF16)
    lw["wr_lo"] = (wr - lw["wr_hi"].astype(F32)).astype(BF16)
    br = jnp.concatenate([router_group_b[l], router_expert_b[l]])
    lw["br"] = jnp.pad(br, (0, LANES - br.shape[0]))[None, :]
    return lw


def kernel(x, c, ctx, c_ctx, w_ada, b_ada, w_in, na_bias, gqa_q_gain, gqa_k_gain, win_sink, mla_q_gain, mla_w_qb, mla_kv_gain, mla_w_kvb, w_out, ln1_g, ln1_b, router_group_w, router_group_b, router_expert_w, router_expert_b, moe_w_gate, moe_w_up, moe_w_down, ln2_g, ln2_b):
    bsz, t, d = x.shape
    tc = ctx.shape[1]
    depth = w_ada.shape[0]
    rows = t // GRID_W
    assert t % (2 * GRID_W) == 0 and rows >= NBR_SPAN_ROWS and rows // 2 >= 5
    assert t >= BLOCK + 2 * WINDOW and t % ROW_TILE == 0 and tc % ROW_TILE == 0
    alpha = (2.0 * depth) ** 0.25

    ada_rows = -(-(bsz + 1) // 8) * 8
    cs = jnp.zeros((ada_rows, d), F32).at[:bsz].set(c).at[bsz].set(c_ctx)
    mod = _ada(cs, w_ada, b_ada)
    tabs = _rope_tables(t)

    xl = x.reshape(bsz * t, d)
    xc = ctx.reshape(bsz * tc, d)

    for l in range(depth):
        with_ctx = l < depth - 1
        lw = _layer_weights(l, w_in, gqa_q_gain, gqa_k_gain, mla_q_gain, mla_w_qb, mla_kv_gain,
                            mla_w_kvb, w_out, ln1_g, ln1_b, router_group_w, router_group_b,
                            router_expert_w, router_expert_b)
        wg = moe_w_gate[l].astype(BF16)
        wu = moe_w_up[l].astype(BF16)
        wd = moe_w_down[l].astype(BF16)
        ln2g, ln2b = ln2_g[l][None, :], ln2_b[l][None, :]

        def mods(lo, hi):
            parts = jnp.split(mod[l, lo:hi], 6, axis=-1)
            return [jnp.broadcast_to(p[:, None, :], (bsz, 1, d)) for p in parts]
        sh1, sc1, g1, sh2, sc2, g2 = mods(0, bsz)
        sh1c, sc1c, g1c, sh2c, sc2c, g2c = mods(bsz, bsz + 1)

        pl_lat = _project(xl, sc1, sh1, lw, tabs, t)
        pl_ctx = _project(xc, sc1c, sh1c, lw, None, tc)
        qa, ka, va, qb, kb, vb, qc, kc, vc, qd, kd, vd = [
            a.reshape(bsz, t, a.shape[1]) for a in pl_lat]
        qa_c, ka_c, va_c, qb_c, kb_c, vb_c, qc_c, kc_c, vc_c, qd_c, kd_c, vd_c = [
            a.reshape(bsz, tc, a.shape[1]) for a in pl_ctx]

        bias_tab = _nbr_bias_tables(na_bias[l], rows)
        out_a = _attention("mha", qa, [("nbr", ka, va), ("full", ka_c, va_c)], bias=bias_tab)
        out_b = _attention("gqa", qb, [("full", kb, vb), ("full", kb_c, vb_c)])
        out_c = _attention("gqa", qc, [("win", kc, vc), ("full", kc_c, vc_c)], sink=win_sink[l])
        out_d = _attention("mla", qd, [("full", kd, vd), ("full", kd_c, vd_c)])
        outs = [o.reshape(bsz * t, 256) for o in (out_a, out_b, out_c, out_d)]

        x1, h2, route = _out_proj(alpha, outs, xl, g1, sc2, sh2, lw, t)
        y = _moe(h2, route, wg, wu, wd)
        xl = _finish(alpha, x1, y, g2, ln2g, ln2b, t)

        if with_ctx:
            outs_c = [
                _attention("mha", qa_c, [("full", ka_c, va_c)]),
                _attention("gqa", qb_c, [("full", kb_c, vb_c)]),
                _attention("gqa", qc_c, [("full", kc_c, vc_c)], sink=win_sink[l]),
                _attention("mla", qd_c, [("full", kd_c, vd_c)]),
            ]
            outs_c = [o.reshape(bsz * tc, 256) for o in outs_c]
            x1c, h2c, route_c = _out_proj(alpha, outs_c, xc, g1c, sc2c, sh2c, lw, tc)
            yc = _moe(h2c, route_c, wg, wu, wd)
            xc = _finish(alpha, x1c, yc, g2c, ln2g, ln2b, tc)

    return xl.reshape(bsz, t, d)
```

```python
import functools

import jax
import jax.numpy as jnp
from jax import lax
from jax.experimental import pallas as pl
from jax.experimental.pallas import tpu as pltpu

F32 = jnp.float32
BF16 = jnp.bfloat16

GRID_W = 64
HEAD_DIM = 64
BLOCK = 128
WINDOW = 128
ROPE_THETA = 10000.0
NEG = -1e30
EPS = 1e-6
LOG2E = 1.4426950408889634

NA_ROWS = 8
NA_COLS = 16
NBR_SPAN_ROWS = 10
MLA_NOPE = 64
MLA_ROPE = 32
MLA_QK = MLA_NOPE + MLA_ROPE
N_GROUPS = 4
EXPERTS_PER_GROUP = 4
N_EXPERTS = 16
N_PAIRS = 6
N_BUCKETS = N_GROUPS * N_PAIRS

LANES = 128
VMEM_LIMIT = 48 * 1024 * 1024

ROW_TILE = 256
MOE_TILE = 256
KV_CHUNK = 1024


def _cparams(*sem):
    return pltpu.CompilerParams(dimension_semantics=sem, vmem_limit_bytes=VMEM_LIMIT)


def _dot(a, b):
    return jnp.dot(a, b, preferred_element_type=F32)


def _dot_nt(a, b):
    return lax.dot_general(a, b, (((1,), (1,)), ((), ())), preferred_element_type=F32)


def _split(a):
    hi = a.astype(BF16)
    lo = (a - hi.astype(F32)).astype(BF16)
    return hi, lo


def _dot3(a, b):
    ah, al = _split(a)
    bh, bl = _split(b)
    return _dot(al, bh) + _dot(ah, bl) + _dot(ah, bh)


def _ln(x):
    mu = jnp.mean(x, axis=-1, keepdims=True)
    xc = x - mu
    var = jnp.mean(xc * xc, axis=-1, keepdims=True)
    return xc * lax.rsqrt(var + EPS)


def _silu(g):
    return g / (1.0 + jnp.exp(-g))


def _ada_kernel(c_ref, w_ref, b_ref, o_ref):
    o_ref[0] = _dot3(_silu(c_ref[...]), w_ref[0]) + b_ref[0]


def _ada(cs, w_ada, b_ada):
    depth, d, n = w_ada.shape
    rows = cs.shape[0]
    tn = 1536
    return pl.pallas_call(
        _ada_kernel,
        grid=(depth, n // tn),
        in_specs=[pl.BlockSpec((rows, d), lambda l, j: (0, 0)),
                  pl.BlockSpec((1, d, tn), lambda l, j: (l, 0, j)),
                  pl.BlockSpec((1, 1, tn), lambda l, j: (l, 0, j))],
        out_specs=pl.BlockSpec((1, rows, tn), lambda l, j: (l, 0, j)),
        out_shape=jax.ShapeDtypeStruct((depth, rows, n), F32),
        compiler_params=_cparams("parallel", "parallel"),
    )(cs, w_ada, b_ada.reshape(depth, 1, n))


def _head_rms(x, bd, gain):
    hi, lo = _split(x * x)
    ss = _dot(lo, bd) + _dot(hi, bd)
    return x * lax.rsqrt(ss * (1.0 / HEAD_DIM) + EPS) * gain


def _rms(x, gain):
    return x * lax.rsqrt(jnp.mean(x * x, axis=-1, keepdims=True) + EPS) * gain


def _rope(x, tab_ref, shift):
    outs = []
    for j in range(x.shape[1] // LANES):
        xb = x[:, j * LANES:(j + 1) * LANES]
        outs.append(xb * tab_ref[0]
                    + pltpu.roll(xb, shift, 1) * tab_ref[1]
                    + pltpu.roll(xb, LANES - shift, 1) * tab_ref[2])
    return outs[0] if len(outs) == 1 else jnp.concatenate(outs, axis=1)


def _value_tiles(v):
    lane = lax.broadcasted_iota(jnp.int32, (v.shape[0], LANES), 1)
    fill = jnp.where(lane == HEAD_DIM, 1.0, 0.0)
    outs = []
    for j in range(v.shape[1] // LANES):
        blk = v[:, j * LANES:(j + 1) * LANES]
        outs.append(jnp.where(lane < HEAD_DIM, blk, fill))
        outs.append(jnp.where(lane < HEAD_DIM, pltpu.roll(blk, HEAD_DIM, 1), fill))
    return jnp.concatenate(outs, axis=1).astype(BF16)


def _proj_kernel(rope, *refs):
    (x_ref, sc_ref, sh_ref, w_ref, gq_ref, gk_ref, gmq_ref, gmkv_ref,
     wqb_ref, wkvb_ref, bd_ref) = refs[:11]
    refs = refs[11:]
    if rope:
        t64_ref, tq_ref, tk_ref = refs[:3]
        refs = refs[3:]
    qa, ka, va, qb, kb, vb, qc, kc, vc, qd, kd, vd = refs

    h = _ln(x_ref[...]) * (1.0 + sc_ref[0]) + sh_ref[0]
    hb = h.astype(BF16)
    sq = HEAD_DIM ** -0.5 * LOG2E

    def proj(a, b):
        return _dot(hb, w_ref[:, a:b])

    pa = proj(0, 768)
    qa[...] = (pa[:, 0:256] * sq).astype(BF16)
    ka[...] = pa[:, 256:512].astype(BF16)
    va[...] = _value_tiles(pa[:, 512:768])

    pb = proj(768, 1280)
    q = _head_rms(pb[:, 0:256], bd_ref[...], gq_ref[...])
    k = _head_rms(pb[:, 256:384], bd_ref[0:128, 0:128], gk_ref[...])
    if rope:
        q = _rope(q, t64_ref, 16)
        k = _rope(k, t64_ref, 16)
    qb[...] = (q * sq).astype(BF16)
    kb[...] = k.astype(BF16)
    vb[...] = _value_tiles(pb[:, 384:512])

    pc = proj(1280, 1792)
    q = pc[:, 0:256]
    k = pc[:, 256:384]
    if rope:
        q = _rope(q, t64_ref, 16)
        k = _rope(k, t64_ref, 16)
    qc[...] = (q * sq).astype(BF16)
    kc[...] = k.astype(BF16)
    vc[...] = _value_tiles(pc[:, 384:512])

    pd = proj(1792, 2304)
    cq = _rms(pd[:, 0:256], gmq_ref[...])
    q = _dot(cq.astype(BF16), wqb_ref[...])
    if rope:
        q = _rope(q, tq_ref, 8)
    qd[...] = (q * (MLA_QK ** -0.5 * LOG2E)).astype(BF16)
    ckv = _rms(pd[:, 256:384], gmkv_ref[...])
    kvu = _dot(ckv.astype(BF16), wkvb_ref[...])
    kr = pd[:, 384:512]
    if rope:
        kr = _rope(kr, tk_ref, 8)
    kr = pltpu.roll(kr, MLA_NOPE, 1)
    kd[...] = jnp.concatenate(
        [kvu[:, j * LANES:(j + 1) * LANES] + kr for j in range(4)], axis=1).astype(BF16)
    lane = lax.broadcasted_iota(jnp.int32, (kr.shape[0], LANES), 1)
    fill = jnp.where(lane == HEAD_DIM, 1.0, 0.0)
    vd[...] = jnp.concatenate(
        [kvu[:, (4 + j) * LANES:(5 + j) * LANES] + fill for j in range(4)], axis=1).astype(BF16)


_PROJ_WIDTHS = (256, 256, 512, 256, 128, 256, 256, 128, 256, 512, 512, 512)


def _project(x2d, sc, sh, lw, tabs, tokens_per_batch):
    n, d = x2d.shape
    tm = ROW_TILE
    per_b = tokens_per_batch // tm
    rope = tabs is not None
    row = lambda i: (i, 0)
    full = lambda i: (0, 0)
    perb = lambda i: (i // per_b, 0, 0)
    in_specs = [pl.BlockSpec((tm, d), row),
                pl.BlockSpec((1, 1, d), perb), pl.BlockSpec((1, 1, d), perb),
                pl.BlockSpec(lw["w_in"].shape, full),
                pl.BlockSpec((1, 256), full), pl.BlockSpec((1, 128), full),
                pl.BlockSpec((1, 256), full), pl.BlockSpec((1, 128), full),
                pl.BlockSpec(lw["w_qb"].shape, full), pl.BlockSpec(lw["w_kvb"].shape, full),
                pl.BlockSpec((256, 256), full)]
    args = [x2d, sc, sh, lw["w_in"], lw["gq"], lw["gk"], lw["gmq"], lw["gmkv"],
            lw["w_qb"], lw["w_kvb"], lw["bd"]]
    if rope:
        tab = lambda i: (0, i % per_b, 0)
        in_specs += [pl.BlockSpec((3, tm, LANES), tab)] * 3
        args += list(tabs)
    return pl.pallas_call(
        functools.partial(_proj_kernel, rope),
        grid=(n // tm,),
        in_specs=in_specs,
        out_specs=[pl.BlockSpec((tm, w), row) for w in _PROJ_WIDTHS],
        out_shape=[jax.ShapeDtypeStruct((n, w), BF16) for w in _PROJ_WIDTHS],
        compiler_params=_cparams("parallel"),
    )(*args)


def _softmax_step(s, vts, bq, m, acc):
    m_new = jnp.maximum(m, jnp.max(s, axis=1, keepdims=True))
    alpha = jnp.exp2(m - m_new)
    p = jnp.exp2(s - m_new).astype(BF16)
    if len(vts) == 1:
        pv = _dot(p, vts[0])
    else:
        pv = jnp.concatenate([_dot(p[:bq], vts[0]), _dot(p[bq:], vts[1])], axis=0)
    return [m_new, alpha * acc + pv]


def _attn_kernel(mode, bq, srcs, has_bias, has_sink, *refs):
    refs = list(refs)
    q_ref = refs.pop(0)
    kv = [(refs.pop(0), refs.pop(0)) for _ in srcs]
    bias_ref = refs.pop(0) if has_bias else None
    sink_ref = refs.pop(0) if has_sink else None
    o_ref = refs.pop(0)

    qi = pl.program_id(1)
    rows2 = 2 * bq
    lo = lax.broadcasted_iota(jnp.int32, (bq, LANES), 1) < HEAD_DIM

    qss, kcs, vcs = [], [], []
    for j in range(2):
        if mode == "mla":
            qblk = q_ref[0, :, 2 * LANES * j:2 * LANES * (j + 1)].astype(F32)
            first = lax.broadcasted_iota(jnp.int32, (bq, 2 * LANES), 1) < LANES
            qs = jnp.concatenate([jnp.where(first, qblk, 0.0), jnp.where(first, 0.0, qblk)], axis=0)
            kcs.append((2 * LANES * j, 2 * LANES * (j + 1)))
            vcs.append((2 * LANES * j, 2 * LANES * j + LANES))
        else:
            qblk = q_ref[0, :, LANES * j:LANES * (j + 1)].astype(F32)
            if mode == "mha":
                qe, qo = jnp.where(lo, qblk, 0.0), jnp.where(lo, 0.0, qblk)
                kcs.append((LANES * j, LANES * (j + 1)))
                vcs.append((2 * LANES * j, 2 * LANES * j + LANES))
            else:
                rolled = pltpu.roll(qblk, HEAD_DIM, 1)
                if j == 0:
                    qe, qo = jnp.where(lo, qblk, 0.0), jnp.where(lo, rolled, 0.0)
                else:
                    qe, qo = jnp.where(lo, 0.0, rolled), jnp.where(lo, 0.0, qblk)
                kcs.append((0, LANES))
                vcs.append((LANES * j,))
            qs = jnp.concatenate([qe, qo], axis=0)
        qss.append(qs.astype(BF16))

    state = [jnp.full((rows2, 1), NEG, F32), jnp.zeros((rows2, LANES), F32)] * 2

    for (kind, nrows, tk), (k_ref, v_ref) in zip(srcs, kv):
        def tiles(j, off, n, k_ref=k_ref, v_ref=v_ref):
            kt = k_ref[0, pl.ds(off, n), kcs[j][0]:kcs[j][1]]
            return kt, [v_ref[0, pl.ds(off, n), v0:v0 + LANES] for v0 in vcs[j]]

        if kind == "full":
            def step(c, carry, tiles=tiles, tk=tk):
                off = c * tk if isinstance(c, int) else pl.multiple_of(c * tk, tk)
                out = []
                for j in range(2):
                    kt, vts = tiles(j, off, tk)
                    out += _softmax_step(_dot_nt(qss[j], kt), vts, bq, carry[2 * j], carry[2 * j + 1])
                return tuple(out)
            if nrows // tk == 1:
                state = list(step(0, tuple(state)))
            else:
                state = list(lax.fori_loop(0, nrows // tk, step, tuple(state)))
        elif kind == "win":
            span = bq + 2 * WINDOW
            start = pl.multiple_of(jnp.clip((qi - 1) * bq, 0, nrows - span), bq)
            r = lax.broadcasted_iota(jnp.int32, (rows2, span), 0)
            qpos = qi * bq + jnp.where(r >= bq, r - bq, r)
            kpos = start + lax.broadcasted_iota(jnp.int32, (rows2, span), 1)
            inside = jnp.abs(qpos - kpos) <= WINDOW
            for j in range(2):
                kt, vts = tiles(j, start, span)
                s = jnp.where(inside, _dot_nt(qss[j], kt), NEG)
                state[2 * j:2 * j + 2] = _softmax_step(s, vts, bq, state[2 * j], state[2 * j + 1])
        else:
            span = NBR_SPAN_ROWS * GRID_W
            ks = jnp.clip(2 * qi - NA_ROWS // 2, 0, nrows // GRID_W - NBR_SPAN_ROWS)
            start = pl.multiple_of(ks * GRID_W, GRID_W)
            for j in range(2):
                kt, vts = tiles(j, start, span)
                s = _dot_nt(qss[j], kt) + bias_ref[0, j]
                state[2 * j:2 * j + 2] = _softmax_step(s, vts, bq, state[2 * j], state[2 * j + 1])

    for j in range(2):
        m, acc = state[2 * j], state[2 * j + 1]
        l = acc[:, HEAD_DIM:HEAD_DIM + 1]
        if has_sink:
            sk = jnp.concatenate([jnp.full((bq, 1), sink_ref[2 * j] * LOG2E, F32),
                                  jnp.full((bq, 1), sink_ref[2 * j + 1] * LOG2E, F32)], axis=0)
            mf = jnp.maximum(m, sk)
            a = jnp.exp2(m - mf)
            l = l * a + jnp.exp2(sk - mf)
            acc = acc * a
        o = acc / l
        o_ref[0, :, LANES * j:LANES * (j + 1)] = jnp.where(
            lo, o[:bq], pltpu.roll(o[bq:], HEAD_DIM, 1)).astype(BF16)


def _attention(mode, q, sources, bias=None, sink=None):
    b, tq, wq = q.shape
    bq = BLOCK
    srcs = []
    in_specs = [pl.BlockSpec((1, bq, wq), lambda bi, qi: (bi, qi, 0))]
    args = [q]
    for kind, k, v in sources:
        s = k.shape[1]
        tk = KV_CHUNK if (kind == "full" and s % KV_CHUNK == 0) else s
        srcs.append((kind, s, tk))
        in_specs += [pl.BlockSpec((1, s, k.shape[2]), lambda bi, qi: (bi, 0, 0)),
                     pl.BlockSpec((1, s, v.shape[2]), lambda bi, qi: (bi, 0, 0))]
        args += [k, v]
    if bias is not None:
        npair = tq // bq

        def bias_idx(bi, qi):
            return (jnp.where(qi < 2, qi, jnp.where(qi >= npair - 2, qi - (npair - 2) + 3, 2)), 0, 0, 0)
        in_specs.append(pl.BlockSpec((1,) + bias.shape[1:], bias_idx))
        args.append(bias)
    if sink is not None:
        in_specs.append(pl.BlockSpec(memory_space=pltpu.SMEM))
        args.append(sink)
    return pl.pallas_call(
        functools.partial(_attn_kernel, mode, bq, tuple(srcs), bias is not None, sink is not None),
        grid=(b, tq // bq),
        in_specs=in_specs,
        out_specs=pl.BlockSpec((1, bq, 256), lambda bi, qi: (bi, qi, 0)),
        out_shape=jax.ShapeDtypeStruct((b, tq, 256), BF16),
        compiler_params=_cparams("parallel", "arbitrary"),
    )(*args)


def _route(logits):
    lane = lax.broadcasted_iota(jnp.int32, logits.shape, 1).astype(F32)
    big = float(LANES)
    is_g = lane < N_GROUPS
    lg = jnp.where(is_g, logits, NEG)
    gmax = jnp.max(lg, axis=1, keepdims=True)
    grp = jnp.min(jnp.where(lg == gmax, lane, big), axis=1, keepdims=True)
    den = jnp.sum(jnp.where(is_g, jnp.exp(lg - gmax), 0.0), axis=1, keepdims=True)
    p_grp = 1.0 / den
    e0 = N_GROUPS + EXPERTS_PER_GROUP * grp
    in_grp = (lane >= e0) & (lane < e0 + EXPERTS_PER_GROUP)
    le = jnp.where(in_grp, logits, NEG)
    v1 = jnp.max(le, axis=1, keepdims=True)
    i1 = jnp.min(jnp.where(in_grp & (le == v1), lane, big), axis=1, keepdims=True)
    rest = in_grp & (lane != i1)
    le2 = jnp.where(rest, logits, NEG)
    v2 = jnp.max(le2, axis=1, keepdims=True)
    i2 = jnp.min(jnp.where(rest & (le2 == v2), lane, big), axis=1, keepdims=True)
    t = jnp.exp(v2 - v1)
    w1 = p_grp / (1.0 + t)
    w2 = p_grp * t / (1.0 + t)
    first_lower = i1 < i2
    e_lo = jnp.where(first_lower, i1, i2) - N_GROUPS
    e_hi = jnp.where(first_lower, i2, i1) - N_GROUPS
    w_lo = jnp.where(first_lower, w1, w2)
    w_hi = jnp.where(first_lower, w2, w1)
    return jnp.where(lane == 0, e_lo, jnp.where(lane == 1, e_hi,
                     jnp.where(lane == 2, w_lo, jnp.where(lane == 3, w_hi, 0.0))))


def _out_kernel(alpha, oa, ob, oc, od, x_ref, g1_ref, sc2_ref, sh2_ref, w_ref, lng_ref, lnb_ref,
                wrh_ref, wrl_ref, br_ref, x1_ref, h2_ref, route_ref):
    mix = (_dot(oa[...], w_ref[0:256, :]) + _dot(ob[...], w_ref[256:512, :])
           + _dot(oc[...], w_ref[512:768, :]) + _dot(od[...], w_ref[768:1024, :]))
    x1 = _ln(alpha * x_ref[...] + g1_ref[0] * mix) * lng_ref[...] + lnb_ref[...]
    x1_ref[...] = x1
    h2 = _ln(x1) * (1.0 + sc2_ref[0]) + sh2_ref[0]
    h2_ref[...] = h2.astype(BF16)
    hh, hl = _split(h2)
    logits = _dot(hl, wrh_ref[...]) + _dot(hh, wrl_ref[...]) + _dot(hh, wrh_ref[...]) + br_ref[...]
    route_ref[...] = _route(logits)


def _out_proj(alpha, outs, x2d, g1, sc2, sh2, lw, tokens_per_batch):
    n, d = x2d.shape
    tm = ROW_TILE
    per_b = tokens_per_batch // tm
    row = lambda i: (i, 0)
    full = lambda i: (0, 0)
    perb = lambda i: (i // per_b, 0, 0)
    return pl.pallas_call(
        functools.partial(_out_kernel, alpha),
        grid=(n // tm,),
        in_specs=[pl.BlockSpec((tm, 256), row)] * 4 + [
            pl.BlockSpec((tm, d), row),
            pl.BlockSpec((1, 1, d), perb), pl.BlockSpec((1, 1, d), perb), pl.BlockSpec((1, 1, d), perb),
            pl.BlockSpec((d, d), full), pl.BlockSpec((1, d), full), pl.BlockSpec((1, d), full),
            pl.BlockSpec((d, LANES), full), pl.BlockSpec((d, LANES), full), pl.BlockSpec((1, LANES), full)],
        out_specs=[pl.BlockSpec((tm, d), row), pl.BlockSpec((tm, d), row), pl.BlockSpec((tm, LANES), row)],
        out_shape=[jax.ShapeDtypeStruct((n, d), F32), jax.ShapeDtypeStruct((n, d), BF16),
                   jax.ShapeDtypeStruct((n, LANES), F32)],
        compiler_params=_cparams("parallel"),
    )(*outs, x2d, g1, sc2, sh2, lw["w_out"], lw["ln1_g"], lw["ln1_b"], lw["wr_hi"], lw["wr_lo"], lw["br"])


def _moe_kernel(e0_ref, e1_ref, valid_ref, xs_ref, rt_ref, wg0, wg1, wu0, wu1, wd0, wd1, o_ref):
    t = pl.program_id(0)

    @pl.when(valid_ref[t] == 1)
    def _():
        x = xs_ref[...]

        def expert(wg, wu, wd):
            a = _silu(_dot(x, wg[0])) * _dot(x, wu[0])
            return _dot(a.astype(BF16), wd[0])

        rt = rt_ref[...]
        y = rt[:, 2:3] * expert(wg0, wu0, wd0) + rt[:, 3:4] * expert(wg1, wu1, wd1)
        o_ref[...] = y.astype(o_ref.dtype)

    @pl.when(valid_ref[t] == 0)
    def _():
        o_ref[...] = jnp.zeros_like(o_ref)


def _moe_experts(e0, e1, valid, xs, rts, wg, wu, wd):
    p, d = xs.shape
    tm = MOE_TILE
    de = wg.shape[2]
    row = lambda t, e0, e1, v: (t, 0)
    w0 = lambda t, e0, e1, v: (e0[t], 0, 0)
    w1 = lambda t, e0, e1, v: (e1[t], 0, 0)
    return pl.pallas_call(
        _moe_kernel,
        grid_spec=pltpu.PrefetchScalarGridSpec(
            num_scalar_prefetch=3,
            grid=(p // tm,),
            in_specs=[pl.BlockSpec((tm, d), row), pl.BlockSpec((tm, LANES), row),
                      pl.BlockSpec((1, d, de), w0), pl.BlockSpec((1, d, de), w1),
                      pl.BlockSpec((1, d, de), w0), pl.BlockSpec((1, d, de), w1),
                      pl.BlockSpec((1, de, d), w0), pl.BlockSpec((1, de, d), w1)],
            out_specs=pl.BlockSpec((tm, d), row)),
        out_shape=jax.ShapeDtypeStruct((p, d), BF16),
        compiler_params=_cparams("arbitrary"),
    )(e0, e1, valid, xs, rts, wg, wg, wu, wu, wd, wd)


_PAIR_TABLE = ((0, 1), (0, 2), (0, 3), (1, 2), (1, 3), (2, 3))


def _moe(h2, route, wg, wu, wd):
    n = h2.shape[0]
    tm = MOE_TILE
    n_tiles = n // tm + N_BUCKETS
    e_lo = route[:, 0].astype(jnp.int32)
    e_hi = route[:, 1].astype(jnp.int32)
    grp = e_lo // EXPERTS_PER_GROUP
    a = e_lo % EXPERTS_PER_GROUP
    b = e_hi % EXPERTS_PER_GROUP
    pair = a * (2 * EXPERTS_PER_GROUP - a - 1) // 2 + (b - a - 1)
    bucket = grp * N_PAIRS + pair

    tok = jnp.arange(n, dtype=jnp.int32)
    sorted_bucket, order = lax.sort((bucket, tok), num_keys=1)
    edges = jnp.searchsorted(sorted_bucket, jnp.arange(N_BUCKETS + 1, dtype=jnp.int32)).astype(jnp.int32)
    counts = edges[1:] - edges[:-1]
    tiles_per = (counts + tm - 1) // tm
    tile_end = jnp.cumsum(tiles_per)
    tile_start = tile_end - tiles_per
    src_off = jnp.cumsum(counts) - counts

    tile_ids = jnp.arange(n_tiles, dtype=jnp.int32)
    valid = (tile_ids < tile_end[-1]).astype(jnp.int32)
    tile_bucket = jnp.minimum(jnp.searchsorted(tile_end, tile_ids, side="right").astype(jnp.int32),
                              N_BUCKETS - 1)
    last_bucket = tile_bucket[jnp.maximum(tile_end[-1] - 1, 0)]
    tile_bucket = jnp.where(valid == 1, tile_bucket, last_bucket)
    pairs = jnp.asarray(_PAIR_TABLE, dtype=jnp.int32)
    tg = tile_bucket // N_PAIRS
    e0 = tg * EXPERTS_PER_GROUP + pairs[tile_bucket % N_PAIRS, 0]
    e1 = tg * EXPERTS_PER_GROUP + pairs[tile_bucket % N_PAIRS, 1]

    rows = jnp.arange(n_tiles * tm, dtype=jnp.int32)
    rb = jnp.repeat(tile_bucket, tm)
    within = rows - jnp.repeat(tile_start[tile_bucket], tm) * tm
    row_ok = (within < counts[rb]) & (jnp.repeat(valid, tm) == 1)
    src = jnp.where(row_ok, order[jnp.clip(src_off[rb] + within, 0, n - 1)], 0)

    xs = jnp.take(h2, src, axis=0)
    rts = jnp.where(row_ok[:, None], jnp.take(route, src, axis=0), 0.0)
    ys = _moe_experts(e0, e1, valid, xs, rts, wg, wu, wd)

    _, rank = lax.sort((order, tok), num_keys=1)
    pos = tile_start[bucket] * tm + rank - src_off[bucket]
    return jnp.take(ys, pos, axis=0)


def _fin_kernel(alpha, x_ref, y_ref, g2_ref, lng_ref, lnb_ref, o_ref):
    o_ref[...] = _ln(alpha * x_ref[...] + g2_ref[0] * y_ref[...]) * lng_ref[...] + lnb_ref[...]


def _finish(alpha, x1, y, g2, ln_g, ln_b, tokens_per_batch):
    n, d = x1.shape
    tm = ROW_TILE
    per_b = tokens_per_batch // tm
    row = lambda i: (i, 0)
    full = lambda i: (0, 0)
    return pl.pallas_call(
        functools.partial(_fin_kernel, alpha),
        grid=(n // tm,),
        in_specs=[pl.BlockSpec((tm, d), row), pl.BlockSpec((tm, d), row),
                  pl.BlockSpec((1, 1, d), lambda i: (i // per_b, 0, 0)),
                  pl.BlockSpec((1, d), full), pl.BlockSpec((1, d), full)],
        out_specs=pl.BlockSpec((tm, d), row),
        out_shape=jax.ShapeDtypeStruct((n, d), F32),
        compiler_params=_cparams("parallel"),
    )(x1, y, g2, ln_g, ln_b)


def _rope_tables(t):
    pos = jnp.arange(t)
    row = (pos // GRID_W).astype(F32)[:, None]
    col = (pos % GRID_W).astype(F32)[:, None]
    lane = jnp.arange(LANES)

    def build(d, half, active):
        axis_col = (d // half) % 2 == 1
        w = d % half
        first = w < half // 2
        f = (w % (half // 2)).astype(F32)
        freq = ROPE_THETA ** (-(2.0 * f) / half)
        ang = jnp.where(axis_col[None, :], col, row) * freq[None, :]
        cos = jnp.where(active[None, :], jnp.cos(ang), 1.0)
        sin = jnp.where(active[None, :], jnp.sin(ang), 0.0)
        return jnp.stack([cos, jnp.where(first[None, :], 0.0, sin), jnp.where(first[None, :], -sin, 0.0)])

    t64 = build(lane % HEAD_DIM, HEAD_DIM // 2, lane >= 0)
    in_rope = (lane >= MLA_NOPE) & (lane < MLA_QK)
    tq = build(jnp.where(in_rope, lane - MLA_NOPE, 0), MLA_ROPE // 2, in_rope)
    in_kr = lane < MLA_ROPE
    tk = build(jnp.where(in_kr, lane, 0), MLA_ROPE // 2, in_kr)
    return t64, tq, tk


def _nbr_bias_tables(na_bias, rows):
    heads = na_bias.shape[0]
    npair = rows // 2
    kr = min(NA_ROWS, rows)
    qcol = jnp.arange(GRID_W)[:, None]
    kcol = jnp.arange(GRID_W)[None, :]
    c0 = jnp.clip(qcol - NA_COLS // 2, 0, GRID_W - NA_COLS)
    col_ok = (kcol >= c0) & (kcol < c0 + NA_COLS)
    col_off = kcol - qcol + NA_COLS - 1
    onehot = (col_off[None] == jnp.arange(2 * NA_COLS - 1)[:, None, None]).astype(F32)
    colmat = jnp.einsum("hdc,cqk->hdqk", na_bias, onehot, precision=lax.Precision.HIGHEST) * LOG2E
    colmat = jnp.where(col_ok[None, None], colmat, NEG)
    outside = jnp.full((heads, GRID_W, GRID_W), NEG, F32)
    tabs = []
    for p in (0, 1, 2, npair - 2, npair - 1):
        ks = min(max(2 * p - NA_ROWS // 2, 0), rows - NBR_SPAN_ROWS)
        per_qrow = []
        for qr in range(2):
            r = 2 * p + qr
            r0 = min(max(r - kr // 2, 0), rows - kr)
            blocks = []
            for kri in range(NBR_SPAN_ROWS):
                krow = ks + kri
                blocks.append(colmat[:, krow - r + NA_ROWS - 1] if r0 <= krow < r0 + kr else outside)
            per_qrow.append(jnp.concatenate(blocks, axis=-1))
        tab = jnp.concatenate(per_qrow, axis=1)
        tabs.append(tab.reshape(heads // 2, 4 * GRID_W, NBR_SPAN_ROWS * GRID_W))
    return jnp.stack(tabs)


def _layer_weights(l, w_in, gqa_q_gain, gqa_k_gain, mla_q_gain, mla_w_qb, mla_kv_gain, mla_w_kvb,
                   w_out, ln1_g, ln1_b, router_group_w, router_group_b, router_expert_w, router_expert_b):
    d = w_in.shape[1]
    d_proj = w_in.shape[2]
    lw = {}
    lw["w_in"] = jnp.pad(w_in[l], ((0, 0), (0, 2304 - d_proj))).astype(BF16)
    lw["gq"] = jnp.tile(gqa_q_gain[l], 4)[None, :]
    lw["gk"] = jnp.tile(gqa_k_gain[l], 2)[None, :]
    lw["gmq"] = mla_q_gain[l][None, :]
    lw["gmkv"] = mla_kv_gain[l][None, :]
    wqb = mla_w_qb[l].reshape(-1, 4, MLA_QK)
    lw["w_qb"] = jnp.pad(wqb, ((0, 0), (0, 0), (0, LANES - MLA_QK))).reshape(-1, 4 * LANES).astype(BF16)
    wkvb = mla_w_kvb[l].reshape(-1, 4, 2 * HEAD_DIM)
    k_part = jnp.pad(wkvb[:, :, :MLA_NOPE], ((0, 0), (0, 0), (0, LANES - MLA_NOPE))).reshape(-1, 4 * LANES)
    v_part = jnp.pad(wkvb[:, :, MLA_NOPE:], ((0, 0), (0, 0), (0, LANES - HEAD_DIM))).reshape(-1, 4 * LANES)
    lw["w_kvb"] = jnp.concatenate([k_part, v_part], axis=1).astype(BF16)
    idx = jnp.arange(256) // HEAD_DIM
    lw["bd"] = (idx[:, None] == idx[None, :]).astype(BF16)
    lw["w_out"] = w_out[l].astype(BF16)
    lw["ln1_g"] = ln1_g[l][None, :]
    lw["ln1_b"] = ln1_b[l][None, :]
    wr = jnp.concatenate([router_group_w[l], router_expert_w[l]], axis=1)
    wr = jnp.pad(wr, ((0, 0), (0, LANES - wr.shape[1])))
    lw["wr_hi"] = wr.astype(BF16)
    lw["wr_lo"] = (wr - lw["wr_hi"].astype(F32)).astype(BF16)
    br = jnp.concatenate([router_group_b[l], router_expert_b[l]])
    lw["br"] = jnp.pad(br, (0, LANES - br.shape[0]))[None, :]
    return lw


def kernel(x, c, ctx, c_ctx, w_ada, b_ada, w_in, na_bias, gqa_q_gain, gqa_k_gain, win_sink, mla_q_gain, mla_w_qb, mla_kv_gain, mla_w_kvb, w_out, ln1_g, ln1_b, router_group_w, router_group_b, router_expert_w, router_expert_b, moe_w_gate, moe_w_up, moe_w_down, ln2_g, ln2_b):
    bsz, t, d = x.shape
    tc = ctx.shape[1]
    depth = w_ada.shape[0]
    rows = t // GRID_W
    assert t % (2 * GRID_W) == 0 and rows >= NBR_SPAN_ROWS and rows // 2 >= 5
    assert t >= BLOCK + 2 * WINDOW and t % ROW_TILE == 0 and tc % ROW_TILE == 0
    alpha = (2.0 * depth) ** 0.25

    ada_rows = -(-(bsz + 1) // 8) * 8
    cs = jnp.zeros((ada_rows, d), F32).at[:bsz].set(c).at[bsz].set(c_ctx)
    mod = _ada(cs, w_ada, b_ada)
    tabs = _rope_tables(t)

    xl = x.reshape(bsz * t, d)
    xc = ctx.reshape(bsz * tc, d)

    for l in range(depth):
        with_ctx = l < depth - 1
        lw = _layer_weights(l, w_in, gqa_q_gain, gqa_k_gain, mla_q_gain, mla_w_qb, mla_kv_gain,
                            mla_w_kvb, w_out, ln1_g, ln1_b, router_group_w, router_group_b,
                            router_expert_w, router_expert_b)
        wg = moe_w_gate[l].astype(BF16)
        wu = moe_w_up[l].astype(BF16)
        wd = moe_w_down[l].astype(BF16)
        ln2g, ln2b = ln2_g[l][None, :], ln2_b[l][None, :]

        def mods(lo, hi):
            parts = jnp.split(mod[l, lo:hi], 6, axis=-1)
            return [jnp.broadcast_to(p[:, None, :], (bsz, 1, d)) for p in parts]
        sh1, sc1, g1, sh2, sc2, g2 = mods(0, bsz)
        sh1c, sc1c, g1c, sh2c, sc2c, g2c = mods(bsz, bsz + 1)

        pl_lat = _project(xl, sc1, sh1, lw, tabs, t)
        pl_ctx = _project(xc, sc1c, sh1c, lw, None, tc)
        qa, ka, va, qb, kb, vb, qc, kc, vc, qd, kd, vd = [
            a.reshape(bsz, t, a.shape[1]) for a in pl_lat]
        qa_c, ka_c, va_c, qb_c, kb_c, vb_c, qc_c, kc_c, vc_c, qd_c, kd_c, vd_c = [
            a.reshape(bsz, tc, a.shape[1]) for a in pl_ctx]

        bias_tab = _nbr_bias_tables(na_bias[l], rows)
        out_a = _attention("mha", qa, [("nbr", ka, va), ("full", ka_c, va_c)], bias=bias_tab)
        out_b = _attention("gqa", qb, [("full", kb, vb), ("full", kb_c, vb_c)])
        out_c = _attention("gqa", qc, [("win", kc, vc), ("full", kc_c, vc_c)], sink=win_sink[l])
        out_d = _attention("mla", qd, [("full", kd, vd), ("full", kd_c, vd_c)])
        outs = [o.reshape(bsz * t, 256) for o in (out_a, out_b, out_c, out_d)]

        x1, h2, route = _out_proj(alpha, outs, xl, g1, sc2, sh2, lw, t)
        y = _moe(h2, route, wg, wu, wd)
        xl = _finish(alpha, x1, y, g2, ln2g, ln2b, t)

        if with_ctx:
            outs_c = [
                _attention("mha", qa_c, [("full", ka_c, va_c)]),
                _attention("gqa", qb_c, [("full", kb_c, vb_c)]),
                _attention("gqa", qc_c, [("full", kc_c, vc_c)], sink=win_sink[l]),
                _attention("mla", qd_c, [("full", kd_c, vd_c)]),
            ]
            outs_c = [o.reshape(bsz * tc, 256) for o in outs_c]
            x1c, h2c, route_c = _out_proj(alpha, outs_c, xc, g1c, sc2c, sh2c, lw, tc)
            yc = _moe(h2c, route_c, wg, wu, wd)
            xc = _finish(alpha, x1c, yc, g2c, ln2g, ln2b, tc)

    return xl.reshape(bsz, t, d)
```

```python
import functools

import jax
import jax.numpy as jnp
from jax import lax
from jax.experimental import pallas as pl
from jax.experimental.pallas import tpu as pltpu

F32 = jnp.float32
BF16 = jnp.bfloat16

GRID_W = 64
HEAD_DIM = 64
BLOCK = 128
WINDOW = 128
ROPE_THETA = 10000.0
NEG = -1e30
EPS = 1e-6
LOG2E = 1.4426950408889634

NA_ROWS = 8
NA_COLS = 16
NBR_SPAN_ROWS = 10
MLA_NOPE = 64
MLA_ROPE = 32
MLA_QK = MLA_NOPE + MLA_ROPE
N_GROUPS = 4
EXPERTS_PER_GROUP = 4
N_EXPERTS = 16
N_PAIRS = 6
N_BUCKETS = N_GROUPS * N_PAIRS

LANES = 128
VMEM_LIMIT = 48 * 1024 * 1024

ROW_TILE = 256
MOE_TILE = 256
KV_CHUNK = 2048
GLOBAL_BQ = 256
H2_EXT = 256


def _cparams(*sem):
    return pltpu.CompilerParams(dimension_semantics=sem, vmem_limit_bytes=VMEM_LIMIT)


def _dot(a, b):
    return jnp.dot(a, b, preferred_element_type=F32)


def _dot_nt(a, b):
    return lax.dot_general(a, b, (((1,), (1,)), ((), ())), preferred_element_type=F32)


def _split(a):
    hi = a.astype(BF16)
    lo = (a - hi.astype(F32)).astype(BF16)
    return hi, lo


def _dot3(a, b):
    ah, al = _split(a)
    bh, bl = _split(b)
    return _dot(al, bh) + _dot(ah, bl) + _dot(ah, bh)


def _ln(x):
    mu = jnp.mean(x, axis=-1, keepdims=True)
    xc = x - mu
    var = jnp.mean(xc * xc, axis=-1, keepdims=True)
    return xc * lax.rsqrt(var + EPS)


def _silu(g):
    return g / (1.0 + jnp.exp(-g))


def _ada_kernel(c_ref, w_ref, b_ref, o_ref):
    o_ref[0] = _dot3(_silu(c_ref[...]), w_ref[0]) + b_ref[0]


def _ada(cs, w_ada, b_ada):
    depth, d, n = w_ada.shape
    rows = cs.shape[0]
    tn = 1536
    return pl.pallas_call(
        _ada_kernel,
        grid=(depth, n // tn),
        in_specs=[pl.BlockSpec((rows, d), lambda l, j: (0, 0)),
                  pl.BlockSpec((1, d, tn), lambda l, j: (l, 0, j)),
                  pl.BlockSpec((1, 1, tn), lambda l, j: (l, 0, j))],
        out_specs=pl.BlockSpec((1, rows, tn), lambda l, j: (l, 0, j)),
        out_shape=jax.ShapeDtypeStruct((depth, rows, n), F32),
        compiler_params=_cparams("parallel", "parallel"),
    )(cs, w_ada, b_ada.reshape(depth, 1, n))


def _head_rms(x, bd, gain):
    hi, lo = _split(x * x)
    ss = _dot(lo, bd) + _dot(hi, bd)
    return x * lax.rsqrt(ss * (1.0 / HEAD_DIM) + EPS) * gain


def _rms(x, gain):
    return x * lax.rsqrt(jnp.mean(x * x, axis=-1, keepdims=True) + EPS) * gain


def _rope(x, tab_ref, shift):
    outs = []
    for j in range(x.shape[1] // LANES):
        xb = x[:, j * LANES:(j + 1) * LANES]
        outs.append(xb * tab_ref[0]
                    + pltpu.roll(xb, shift, 1) * tab_ref[1]
                    + pltpu.roll(xb, LANES - shift, 1) * tab_ref[2])
    return outs[0] if len(outs) == 1 else jnp.concatenate(outs, axis=1)


def _value_tiles(v):
    lane = lax.broadcasted_iota(jnp.int32, (v.shape[0], LANES), 1)
    fill = jnp.where(lane == HEAD_DIM, 1.0, 0.0)
    outs = []
    for j in range(v.shape[1] // LANES):
        blk = v[:, j * LANES:(j + 1) * LANES]
        outs.append(jnp.where(lane < HEAD_DIM, blk, fill))
        outs.append(jnp.where(lane < HEAD_DIM, pltpu.roll(blk, HEAD_DIM, 1), fill))
    return jnp.concatenate(outs, axis=1).astype(BF16)


def _proj_kernel(rope, fin_alpha, *refs):
    refs = list(refs)
    if fin_alpha is None:
        x = refs.pop(0)[...]
    else:
        x1_ref, y_ref, g2_ref, lng_ref, lnb_ref = refs[:5]
        refs = refs[5:]
        x = _ln(fin_alpha * x1_ref[...] + g2_ref[0] * y_ref[...]) * lng_ref[...] + lnb_ref[...]
    (sc_ref, sh_ref, w_ref, gq_ref, gk_ref, gmq_ref, gmkv_ref,
     wqb_ref, wkvb_ref, bd_ref) = refs[:10]
    refs = refs[10:]
    if rope:
        t64_ref, tq_ref, tk_ref = refs[:3]
        refs = refs[3:]
    if fin_alpha is not None:
        refs.pop(0)[...] = x
    qa, ka, va, qb, kb, vb, qc, kc, vc, qd, kd, vd = refs

    h = _ln(x) * (1.0 + sc_ref[0]) + sh_ref[0]
    hb = h.astype(BF16)
    sq = HEAD_DIM ** -0.5 * LOG2E

    def proj(a, b):
        return _dot(hb, w_ref[:, a:b])

    pa = proj(0, 768)
    qa[...] = (pa[:, 0:256] * sq).astype(BF16)
    ka[...] = pa[:, 256:512].astype(BF16)
    va[...] = _value_tiles(pa[:, 512:768])

    pb = proj(768, 1280)
    q = _head_rms(pb[:, 0:256], bd_ref[...], gq_ref[...])
    k = _head_rms(pb[:, 256:384], bd_ref[0:128, 0:128], gk_ref[...])
    if rope:
        q = _rope(q, t64_ref, 16)
        k = _rope(k, t64_ref, 16)
    qb[...] = (q * sq).astype(BF16)
    kb[...] = k.astype(BF16)
    vb[...] = _value_tiles(pb[:, 384:512])

    pc = proj(1280, 1792)
    q = pc[:, 0:256]
    k = pc[:, 256:384]
    if rope:
        q = _rope(q, t64_ref, 16)
        k = _rope(k, t64_ref, 16)
    qc[...] = (q * sq).astype(BF16)
    kc[...] = k.astype(BF16)
    vc[...] = _value_tiles(pc[:, 384:512])

    pd = proj(1792, 2304)
    cq = _rms(pd[:, 0:256], gmq_ref[...])
    q = _dot(cq.astype(BF16), wqb_ref[...])
    if rope:
        q = _rope(q, tq_ref, 8)
    qd[...] = (q * (MLA_QK ** -0.5 * LOG2E)).astype(BF16)
    ckv = _rms(pd[:, 256:384], gmkv_ref[...])
    kvu = _dot(ckv.astype(BF16), wkvb_ref[...])
    kr = pd[:, 384:512]
    if rope:
        kr = _rope(kr, tk_ref, 8)
    kr = pltpu.roll(kr, MLA_NOPE, 1)
    kd[...] = jnp.concatenate(
        [kvu[:, j * LANES:(j + 1) * LANES] + kr for j in range(4)], axis=1).astype(BF16)
    lane = lax.broadcasted_iota(jnp.int32, (kr.shape[0], LANES), 1)
    fill = jnp.where(lane == HEAD_DIM, 1.0, 0.0)
    vd[...] = jnp.concatenate(
        [kvu[:, (4 + j) * LANES:(5 + j) * LANES] + fill for j in range(4)], axis=1).astype(BF16)


_PROJ_WIDTHS = (256, 256, 512, 256, 128, 256, 256, 128, 256, 512, 512, 512)


def _project(x2d, fin, sc, sh, lw, tabs, tokens_per_batch):
    n, d = (x2d if fin is None else fin[1]).shape
    tm = ROW_TILE
    per_b = tokens_per_batch // tm
    rope = tabs is not None
    row = lambda i: (i, 0)
    full = lambda i: (0, 0)
    perb = lambda i: (i // per_b, 0, 0)
    if fin is None:
        in_specs = [pl.BlockSpec((tm, d), row)]
        args = [x2d]
    else:
        in_specs = [pl.BlockSpec((tm, d), row), pl.BlockSpec((tm, d), row), pl.BlockSpec((1, 1, d), perb),
                    pl.BlockSpec((1, d), full), pl.BlockSpec((1, d), full)]
        args = list(fin[1:])
    in_specs += [pl.BlockSpec((1, 1, d), perb), pl.BlockSpec((1, 1, d), perb),
                 pl.BlockSpec(lw["w_in"].shape, full),
                 pl.BlockSpec((1, 256), full), pl.BlockSpec((1, 128), full),
                 pl.BlockSpec((1, 256), full), pl.BlockSpec((1, 128), full),
                 pl.BlockSpec(lw["w_qb"].shape, full), pl.BlockSpec(lw["w_kvb"].shape, full),
                 pl.BlockSpec((256, 256), full)]
    args += [sc, sh, lw["w_in"], lw["gq"], lw["gk"], lw["gmq"], lw["gmkv"],
             lw["w_qb"], lw["w_kvb"], lw["bd"]]
    if rope:
        tab = lambda i: (0, i % per_b, 0)
        in_specs += [pl.BlockSpec((3, tm, LANES), tab)] * 3
        args += list(tabs)
    out_specs = [pl.BlockSpec((tm, w), row) for w in _PROJ_WIDTHS]
    out_shape = [jax.ShapeDtypeStruct((n, w), BF16) for w in _PROJ_WIDTHS]
    if fin is not None:
        out_specs.insert(0, pl.BlockSpec((tm, d), row))
        out_shape.insert(0, jax.ShapeDtypeStruct((n, d), F32))
    outs = pl.pallas_call(
        functools.partial(_proj_kernel, rope, None if fin is None else fin[0]),
        grid=(n // tm,),
        in_specs=in_specs,
        out_specs=out_specs,
        out_shape=out_shape,
        compiler_params=_cparams("parallel"),
    )(*args)
    return (x2d, list(outs)) if fin is None else (outs[0], list(outs[1:]))


def _softmax_step(s, vts, bq, m, acc):
    m_new = jnp.maximum(m, jnp.max(s, axis=1, keepdims=True))
    alpha = jnp.exp2(m - m_new)
    p = jnp.exp2(s - m_new).astype(BF16)
    if len(vts) == 1:
        pv = _dot(p, vts[0])
    else:
        pv = jnp.concatenate([_dot(p[:bq], vts[0]), _dot(p[bq:], vts[1])], axis=0)
    return [m_new, alpha * acc + pv]


def _attn_kernel(mode, bq, srcs, has_bias, has_sink, *refs):
    refs = list(refs)
    q_ref = refs.pop(0)
    kv = [(refs.pop(0), refs.pop(0)) for _ in srcs]
    bias_ref = refs.pop(0) if has_bias else None
    sink_ref = refs.pop(0) if has_sink else None
    o_ref = refs.pop(0)

    qi = pl.program_id(1)
    rows2 = 2 * bq
    lo = lax.broadcasted_iota(jnp.int32, (bq, LANES), 1) < HEAD_DIM

    qss, kcs, vcs = [], [], []
    for j in range(2):
        if mode == "mla":
            qblk = q_ref[0, :, 2 * LANES * j:2 * LANES * (j + 1)].astype(F32)
            first = lax.broadcasted_iota(jnp.int32, (bq, 2 * LANES), 1) < LANES
            qs = jnp.concatenate([jnp.where(first, qblk, 0.0), jnp.where(first, 0.0, qblk)], axis=0)
            kcs.append((2 * LANES * j, 2 * LANES * (j + 1)))
            vcs.append((2 * LANES * j, 2 * LANES * j + LANES))
        else:
            qblk = q_ref[0, :, LANES * j:LANES * (j + 1)].astype(F32)
            if mode == "mha":
                qe, qo = jnp.where(lo, qblk, 0.0), jnp.where(lo, 0.0, qblk)
                kcs.append((LANES * j, LANES * (j + 1)))
                vcs.append((2 * LANES * j, 2 * LANES * j + LANES))
            else:
                rolled = pltpu.roll(qblk, HEAD_DIM, 1)
                if j == 0:
                    qe, qo = jnp.where(lo, qblk, 0.0), jnp.where(lo, rolled, 0.0)
                else:
                    qe, qo = jnp.where(lo, 0.0, rolled), jnp.where(lo, 0.0, qblk)
                kcs.append((0, LANES))
                vcs.append((LANES * j,))
            qs = jnp.concatenate([qe, qo], axis=0)
        qss.append(qs.astype(BF16))

    def pv(p, vts):
        if len(vts) == 1:
            return _dot(p, vts[0])
        return jnp.concatenate([_dot(p[:bq], vts[0]), _dot(p[bq:], vts[1])], axis=0)

    def tiles(j, k_ref, v_ref, off, n):
        kt = k_ref[0, pl.ds(off, n), kcs[j][0]:kcs[j][1]]
        return kt, [v_ref[0, pl.ds(off, n), v0:v0 + LANES] for v0 in vcs[j]]

    def block_scores(j, kind, nrows, k_ref, v_ref):
        if kind == "full":
            kt, vts = tiles(j, k_ref, v_ref, 0, nrows)
            return _dot_nt(qss[j], kt), vts
        if kind == "win":
            span = bq + 2 * WINDOW
            start = pl.multiple_of(jnp.clip((qi - 1) * bq, 0, nrows - span), bq)
            r = lax.broadcasted_iota(jnp.int32, (rows2, span), 0)
            qpos = qi * bq + jnp.where(r >= bq, r - bq, r)
            kpos = start + lax.broadcasted_iota(jnp.int32, (rows2, span), 1)
            kt, vts = tiles(j, k_ref, v_ref, start, span)
            return jnp.where(jnp.abs(qpos - kpos) <= WINDOW, _dot_nt(qss[j], kt), NEG), vts
        span = NBR_SPAN_ROWS * GRID_W
        ks = jnp.clip(2 * qi - NA_ROWS // 2, 0, nrows // GRID_W - NBR_SPAN_ROWS)
        kt, vts = tiles(j, k_ref, v_ref, pl.multiple_of(ks * GRID_W, GRID_W), span)
        return _dot_nt(qss[j], kt) + bias_ref[0, j], vts

    if all(nrows == tk for _, nrows, tk in srcs):
        state = []
        for j in range(2):
            parts = [block_scores(j, kind, nrows, k_ref, v_ref)
                     for (kind, nrows, _), (k_ref, v_ref) in zip(srcs, kv)]
            m = functools.reduce(jnp.maximum, [jnp.max(sc, axis=1, keepdims=True) for sc, _ in parts])
            acc = functools.reduce(lambda a, b: a + b,
                                   [pv(jnp.exp2(sc - m).astype(BF16), vts) for sc, vts in parts])
            state += [m, acc]
    else:
        state = [jnp.full((rows2, 1), NEG, F32), jnp.zeros((rows2, LANES), F32)] * 2
        for (kind, nrows, tk), (k_ref, v_ref) in zip(srcs, kv):
            if nrows == tk:
                for j in range(2):
                    sc, vts = block_scores(j, kind, nrows, k_ref, v_ref)
                    state[2 * j:2 * j + 2] = _softmax_step(sc, vts, bq, state[2 * j], state[2 * j + 1])
            else:
                def step(c, carry, k_ref=k_ref, v_ref=v_ref, tk=tk):
                    off = pl.multiple_of(c * tk, tk)
                    out = []
                    for j in range(2):
                        kt, vts = tiles(j, k_ref, v_ref, off, tk)
                        out += _softmax_step(_dot_nt(qss[j], kt), vts, bq, carry[2 * j], carry[2 * j + 1])
                    return tuple(out)
                state = list(lax.fori_loop(0, nrows // tk, step, tuple(state)))

    for j in range(2):
        m, acc = state[2 * j], state[2 * j + 1]
        l = acc[:, HEAD_DIM:HEAD_DIM + 1]
        if has_sink:
            sk = jnp.concatenate([jnp.full((bq, 1), sink_ref[2 * j] * LOG2E, F32),
                                  jnp.full((bq, 1), sink_ref[2 * j + 1] * LOG2E, F32)], axis=0)
            mf = jnp.maximum(m, sk)
            a = jnp.exp2(m - mf)
            l = l * a + jnp.exp2(sk - mf)
            acc = acc * a
        o = acc / l
        o_ref[0, :, LANES * j:LANES * (j + 1)] = jnp.where(
            lo, o[:bq], pltpu.roll(o[bq:], HEAD_DIM, 1)).astype(BF16)


def _attention(mode, q, sources, bias=None, sink=None, bq=BLOCK):
    b, tq, wq = q.shape
    srcs = []
    in_specs = [pl.BlockSpec((1, bq, wq), lambda bi, qi: (bi, qi, 0))]
    args = [q]
    for kind, k, v in sources:
        s = k.shape[1]
        tk = KV_CHUNK if (kind == "full" and s % KV_CHUNK == 0) else s
        srcs.append((kind, s, tk))
        in_specs += [pl.BlockSpec((1, s, k.shape[2]), lambda bi, qi: (bi, 0, 0)),
                     pl.BlockSpec((1, s, v.shape[2]), lambda bi, qi: (bi, 0, 0))]
        args += [k, v]
    if bias is not None:
        npair = tq // bq

        def bias_idx(bi, qi):
            return (jnp.where(qi < 2, qi, jnp.where(qi >= npair - 2, qi - (npair - 2) + 3, 2)), 0, 0, 0)
        in_specs.append(pl.BlockSpec((1,) + bias.shape[1:], bias_idx))
        args.append(bias)
    if sink is not None:
        in_specs.append(pl.BlockSpec(memory_space=pltpu.SMEM))
        args.append(sink)
    return pl.pallas_call(
        functools.partial(_attn_kernel, mode, bq, tuple(srcs), bias is not None, sink is not None),
        grid=(b, tq // bq),
        in_specs=in_specs,
        out_specs=pl.BlockSpec((1, bq, 256), lambda bi, qi: (bi, qi, 0)),
        out_shape=jax.ShapeDtypeStruct((b, tq, 256), BF16),
        compiler_params=_cparams("parallel", "arbitrary"),
    )(*args)


def _route(logits):
    lane = lax.broadcasted_iota(jnp.int32, logits.shape, 1).astype(F32)
    big = float(LANES)
    is_g = lane < N_GROUPS
    lg = jnp.where(is_g, logits, NEG)
    gmax = jnp.max(lg, axis=1, keepdims=True)
    grp = jnp.min(jnp.where(lg == gmax, lane, big), axis=1, keepdims=True)
    den = jnp.sum(jnp.where(is_g, jnp.exp(lg - gmax), 0.0), axis=1, keepdims=True)
    p_grp = 1.0 / den
    e0 = N_GROUPS + EXPERTS_PER_GROUP * grp
    in_grp = (lane >= e0) & (lane < e0 + EXPERTS_PER_GROUP)
    le = jnp.where(in_grp, logits, NEG)
    v1 = jnp.max(le, axis=1, keepdims=True)
    i1 = jnp.min(jnp.where(in_grp & (le == v1), lane, big), axis=1, keepdims=True)
    rest = in_grp & (lane != i1)
    le2 = jnp.where(rest, logits, NEG)
    v2 = jnp.max(le2, axis=1, keepdims=True)
    i2 = jnp.min(jnp.where(rest & (le2 == v2), lane, big), axis=1, keepdims=True)
    t = jnp.exp(v2 - v1)
    w1 = p_grp / (1.0 + t)
    w2 = p_grp * t / (1.0 + t)
    first_lower = i1 < i2
    e_lo = jnp.where(first_lower, i1, i2) - N_GROUPS
    e_hi = jnp.where(first_lower, i2, i1) - N_GROUPS
    w_lo = jnp.where(first_lower, w1, w2)
    w_hi = jnp.where(first_lower, w2, w1)
    a = e_lo - EXPERTS_PER_GROUP * grp
    b = e_hi - EXPERTS_PER_GROUP * grp
    bucket = grp * N_PAIRS + a * (2 * EXPERTS_PER_GROUP - 1 - a) * 0.5 + (b - a - 1.0)
    return jnp.where(lane == 0, e_lo, jnp.where(lane == 1, e_hi, jnp.where(lane == 2, w_lo,
                     jnp.where(lane == 3, w_hi, jnp.where(lane == 4, bucket, 0.0)))))


def _out_kernel(alpha, oa, ob, oc, od, x_ref, g1_ref, sc2_ref, sh2_ref, w_ref, lng_ref, lnb_ref,
                wrh_ref, wrl_ref, br_ref, x1_ref, h2_ref, ids_ref):
    mix = (_dot(oa[...], w_ref[0:256, :]) + _dot(ob[...], w_ref[256:512, :])
           + _dot(oc[...], w_ref[512:768, :]) + _dot(od[...], w_ref[768:1024, :]))
    x1 = _ln(alpha * x_ref[...] + g1_ref[0] * mix) * lng_ref[...] + lnb_ref[...]
    x1_ref[...] = x1
    h2 = _ln(x1) * (1.0 + sc2_ref[0]) + sh2_ref[0]
    hh, hl = _split(h2)
    logits = _dot(hl, wrh_ref[...]) + _dot(hh, wrl_ref[...]) + _dot(hh, wrh_ref[...]) + br_ref[...]
    route = _route(logits)
    rh, rl = _split(route)
    d = h2.shape[1]
    h2_ref[:, 0:d] = hh
    h2_ref[:, d:d + LANES] = rh
    h2_ref[:, d + LANES:d + 2 * LANES] = rl
    pick = ((lax.broadcasted_iota(jnp.int32, (8, LANES), 0) == 0)
            & (lax.broadcasted_iota(jnp.int32, (8, LANES), 1) == 4)).astype(BF16)
    ids_ref[...] = _dot_nt(pick, rh)


def _out_proj(alpha, outs, x2d, g1, sc2, sh2, lw, tokens_per_batch):
    n, d = x2d.shape
    tm = ROW_TILE
    per_b = tokens_per_batch // tm
    row = lambda i: (i, 0)
    full = lambda i: (0, 0)
    perb = lambda i: (i // per_b, 0, 0)
    return pl.pallas_call(
        functools.partial(_out_kernel, alpha),
        grid=(n // tm,),
        in_specs=[pl.BlockSpec((tm, 256), row)] * 4 + [
            pl.BlockSpec((tm, d), row),
            pl.BlockSpec((1, 1, d), perb), pl.BlockSpec((1, 1, d), perb), pl.BlockSpec((1, 1, d), perb),
            pl.BlockSpec((d, d), full), pl.BlockSpec((1, d), full), pl.BlockSpec((1, d), full),
            pl.BlockSpec((d, LANES), full), pl.BlockSpec((d, LANES), full), pl.BlockSpec((1, LANES), full)],
        out_specs=[pl.BlockSpec((tm, d), row), pl.BlockSpec((tm, d + H2_EXT), row),
                   pl.BlockSpec((8, tm), lambda i: (0, i))],
        out_shape=[jax.ShapeDtypeStruct((n, d), F32), jax.ShapeDtypeStruct((n, d + H2_EXT), BF16),
                   jax.ShapeDtypeStruct((8, n), F32)],
        compiler_params=_cparams("parallel"),
    )(*outs, x2d, g1, sc2, sh2, lw["w_out"], lw["ln1_g"], lw["ln1_b"], lw["wr_hi"], lw["wr_lo"], lw["br"])


def _moe_kernel(e0_ref, e1_ref, valid_ref, xs_ref, wg0, wg1, wu0, wu1, wd0, wd1, o_ref):
    t = pl.program_id(0)
    d = o_ref.shape[1]

    @pl.when(valid_ref[t] == 1)
    def _():
        x = xs_ref[:, 0:d]
        rt = xs_ref[:, d:d + LANES].astype(F32) + xs_ref[:, d + LANES:d + 2 * LANES].astype(F32)

        def expert(wg, wu, wd):
            a = _silu(_dot(x, wg[0])) * _dot(x, wu[0])
            return _dot(a.astype(BF16), wd[0])

        y = rt[:, 2:3] * expert(wg0, wu0, wd0) + rt[:, 3:4] * expert(wg1, wu1, wd1)
        o_ref[...] = y.astype(o_ref.dtype)

    @pl.when(valid_ref[t] == 0)
    def _():
        o_ref[...] = jnp.zeros_like(o_ref)


def _moe_experts(e0, e1, valid, xs, wg, wu, wd):
    p, dx = xs.shape
    d = dx - H2_EXT
    tm = MOE_TILE
    de = wg.shape[2]
    row = lambda t, e0, e1, v: (t, 0)
    w0 = lambda t, e0, e1, v: (e0[t], 0, 0)
    w1 = lambda t, e0, e1, v: (e1[t], 0, 0)
    return pl.pallas_call(
        _moe_kernel,
        grid_spec=pltpu.PrefetchScalarGridSpec(
            num_scalar_prefetch=3,
            grid=(p // tm,),
            in_specs=[pl.BlockSpec((tm, dx), row),
                      pl.BlockSpec((1, d, de), w0), pl.BlockSpec((1, d, de), w1),
                      pl.BlockSpec((1, d, de), w0), pl.BlockSpec((1, d, de), w1),
                      pl.BlockSpec((1, de, d), w0), pl.BlockSpec((1, de, d), w1)],
            out_specs=pl.BlockSpec((tm, d), row)),
        out_shape=jax.ShapeDtypeStruct((p, d), BF16),
        compiler_params=_cparams("arbitrary"),
    )(e0, e1, valid, xs, wg, wg, wu, wu, wd, wd)


_PAIR_TABLE = ((0, 1), (0, 2), (0, 3), (1, 2), (1, 3), (2, 3))


def _moe(h2x, ids, wg, wu, wd):
    n = h2x.shape[0]
    tm = MOE_TILE
    n_tiles = n // tm + N_BUCKETS
    bucket = ids[0].astype(jnp.int32)
    tok = jnp.arange(n, dtype=jnp.int32)
    sorted_bucket, order = lax.sort((bucket, tok), num_keys=1)
    _, rank = lax.sort((order, tok), num_keys=1)

    bids = jnp.arange(N_BUCKETS + 1, dtype=jnp.int32)
    edges = jnp.sum(sorted_bucket[None, :] < bids[:, None], axis=1, dtype=jnp.int32)
    counts = edges[1:] - edges[:-1]
    src_off = edges[:-1]
    tiles_per = (counts + tm - 1) // tm
    tile_end = jnp.cumsum(tiles_per)
    tile_start = tile_end - tiles_per

    tile_ids = jnp.arange(n_tiles, dtype=jnp.int32)
    valid = (tile_ids < tile_end[-1]).astype(jnp.int32)
    tile_bucket = jnp.sum(tile_ids[:, None] >= tile_end[None, :], axis=1, dtype=jnp.int32)
    last_bucket = jnp.sum(tile_end[-1] - 1 >= tile_end, dtype=jnp.int32)
    tile_bucket = jnp.minimum(jnp.where(valid == 1, tile_bucket, last_bucket), N_BUCKETS - 1)
    onehot_tb = tile_bucket[:, None] == bids[None, :N_BUCKETS]
    pick = lambda table: jnp.sum(jnp.where(onehot_tb, table[None, :], 0), axis=1)
    pairs = jnp.asarray(_PAIR_TABLE, dtype=jnp.int32)
    tg = tile_bucket // N_PAIRS
    e0 = tg * EXPERTS_PER_GROUP + pairs[tile_bucket % N_PAIRS, 0]
    e1 = tg * EXPERTS_PER_GROUP + pairs[tile_bucket % N_PAIRS, 1]

    first_row = (tile_ids - pick(tile_start)) * tm
    within = first_row[:, None] + jnp.arange(tm, dtype=jnp.int32)[None, :]
    row_ok = (within < pick(counts)[:, None]) & (valid[:, None] == 1)
    idx = jnp.clip(pick(src_off)[:, None] + within, 0, n - 1)
    src = jnp.where(row_ok, jnp.take(order, idx.reshape(-1)).reshape(idx.shape), 0).reshape(-1)

    xs = jnp.take(h2x, src, axis=0)
    ys = _moe_experts(e0, e1, valid, xs, wg, wu, wd)

    onehot_b = bucket[None, :] == bids[:N_BUCKETS, None]
    base = jnp.sum(jnp.where(onehot_b, (tile_start * tm - src_off)[:, None], 0), axis=0)
    return jnp.take(ys, base + rank, axis=0)


def _fin_kernel(alpha, x_ref, y_ref, g2_ref, lng_ref, lnb_ref, o_ref):
    o_ref[...] = _ln(alpha * x_ref[...] + g2_ref[0] * y_ref[...]) * lng_ref[...] + lnb_ref[...]


def _finish(alpha, x1, y, g2, ln_g, ln_b, tokens_per_batch):
    n, d = x1.shape
    tm = ROW_TILE
    per_b = tokens_per_batch // tm
    row = lambda i: (i, 0)
    full = lambda i: (0, 0)
    return pl.pallas_call(
        functools.partial(_fin_kernel, alpha),
        grid=(n // tm,),
        in_specs=[pl.BlockSpec((tm, d), row), pl.BlockSpec((tm, d), row),
                  pl.BlockSpec((1, 1, d), lambda i: (i // per_b, 0, 0)),
                  pl.BlockSpec((1, d), full), pl.BlockSpec((1, d), full)],
        out_specs=pl.BlockSpec((tm, d), row),
        out_shape=jax.ShapeDtypeStruct((n, d), F32),
        compiler_params=_cparams("parallel"),
    )(x1, y, g2, ln_g, ln_b)


def _rope_tables(t):
    pos = jnp.arange(t)
    row = (pos // GRID_W).astype(F32)[:, None]
    col = (pos % GRID_W).astype(F32)[:, None]
    lane = jnp.arange(LANES)

    def build(d, half, active):
        axis_col = (d // half) % 2 == 1
        w = d % half
        first = w < half // 2
        f = (w % (half // 2)).astype(F32)
        freq = ROPE_THETA ** (-(2.0 * f) / half)
        ang = jnp.where(axis_col[None, :], col, row) * freq[None, :]
        cos = jnp.where(active[None, :], jnp.cos(ang), 1.0)
        sin = jnp.where(active[None, :], jnp.sin(ang), 0.0)
        return jnp.stack([cos, jnp.where(first[None, :], 0.0, sin), jnp.where(first[None, :], -sin, 0.0)])

    t64 = build(lane % HEAD_DIM, HEAD_DIM // 2, lane >= 0)
    in_rope = (lane >= MLA_NOPE) & (lane < MLA_QK)
    tq = build(jnp.where(in_rope, lane - MLA_NOPE, 0), MLA_ROPE // 2, in_rope)
    in_kr = lane < MLA_ROPE
    tk = build(jnp.where(in_kr, lane, 0), MLA_ROPE // 2, in_kr)
    return t64, tq, tk


def _nbr_bias_tables(na_bias, rows):
    heads = na_bias.shape[0]
    npair = rows // 2
    kr = min(NA_ROWS, rows)
    qcol = jnp.arange(GRID_W)[:, None]
    kcol = jnp.arange(GRID_W)[None, :]
    c0 = jnp.clip(qcol - NA_COLS // 2, 0, GRID_W - NA_COLS)
    col_ok = (kcol >= c0) & (kcol < c0 + NA_COLS)
    col_off = kcol - qcol + NA_COLS - 1
    onehot = (col_off[None] == jnp.arange(2 * NA_COLS - 1)[:, None, None]).astype(F32)
    colmat = jnp.einsum("hdc,cqk->hdqk", na_bias, onehot, precision=lax.Precision.HIGHEST) * LOG2E
    colmat = jnp.where(col_ok[None, None], colmat, NEG)
    outside = jnp.full((heads, GRID_W, GRID_W), NEG, F32)
    tabs = []
    for p in (0, 1, 2, npair - 2, npair - 1):
        ks = min(max(2 * p - NA_ROWS // 2, 0), rows - NBR_SPAN_ROWS)
        per_qrow = []
        for qr in range(2):
            r = 2 * p + qr
            r0 = min(max(r - kr // 2, 0), rows - kr)
            blocks = []
            for kri in range(NBR_SPAN_ROWS):
                krow = ks + kri
                blocks.append(colmat[:, krow - r + NA_ROWS - 1] if r0 <= krow < r0 + kr else outside)
            per_qrow.append(jnp.concatenate(blocks, axis=-1))
        tab = jnp.concatenate(per_qrow, axis=1)
        tabs.append(tab.reshape(heads // 2, 4 * GRID_W, NBR_SPAN_ROWS * GRID_W))
    return jnp.stack(tabs)


def _layer_weights(l, w_in, gqa_q_gain, gqa_k_gain, mla_q_gain, mla_w_qb, mla_kv_gain, mla_w_kvb,
                   w_out, ln1_g, ln1_b, router_group_w, router_group_b, router_expert_w, router_expert_b):
    d = w_in.shape[1]
    d_proj = w_in.shape[2]
    lw = {}
    lw["w_in"] = jnp.pad(w_in[l], ((0, 0), (0, 2304 - d_proj))).astype(BF16)
    lw["gq"] = jnp.tile(gqa_q_gain[l], 4)[None, :]
    lw["gk"] = jnp.tile(gqa_k_gain[l], 2)[None, :]
    lw["gmq"] = mla_q_gain[l][None, :]
    lw["gmkv"] = mla_kv_gain[l][None, :]
    wqb = mla_w_qb[l].reshape(-1, 4, MLA_QK)
    lw["w_qb"] = jnp.pad(wqb, ((0, 0), (0, 0), (0, LANES - MLA_QK))).reshape(-1, 4 * LANES).astype(BF16)
    wkvb = mla_w_kvb[l].reshape(-1, 4, 2 * HEAD_DIM)
    k_part = jnp.pad(wkvb[:, :, :MLA_NOPE], ((0, 0), (0, 0), (0, LANES - MLA_NOPE))).reshape(-1, 4 * LANES)
    v_part = jnp.pad(wkvb[:, :, MLA_NOPE:], ((0, 0), (0, 0), (0, LANES - HEAD_DIM))).reshape(-1, 4 * LANES)
    lw["w_kvb"] = jnp.concatenate([k_part, v_part], axis=1).astype(BF16)
    idx = jnp.arange(256) // HEAD_DIM
    lw["bd"] = (idx[:, None] == idx[None, :]).astype(BF16)
    lw["w_out"] = w_out[l].astype(BF16)
    lw["ln1_g"] = ln1_g[l][None, :]
    lw["ln1_b"] = ln1_b[l][None, :]
    wr = jnp.concatenate([router_group_w[l], router_expert_w[l]], axis=1)
    wr = jnp.pad(wr, ((0, 0), (0, LANES - wr.shape[1])))
    lw["wr_hi"] = wr.astype(BF16)
    lw["wr_lo"] = (wr - lw["wr_hi"].astype(F32)).astype(BF16)
    br = jnp.concatenate([router_group_b[l], router_expert_b[l]])
    lw["br"] = jnp.pad(br, (0, LANES - br.shape[0]))[None, :]
    return lw


def kernel(x, c, ctx, c_ctx, w_ada, b_ada, w_in, na_bias, gqa_q_gain, gqa_k_gain, win_sink, mla_q_gain, mla_w_qb, mla_kv_gain, mla_w_kvb, w_out, ln1_g, ln1_b, router_group_w, router_group_b, router_expert_w, router_expert_b, moe_w_gate, moe_w_up, moe_w_down, ln2_g, ln2_b):
    bsz, t, d = x.shape
    tc = ctx.shape[1]
    depth = w_ada.shape[0]
    rows = t // GRID_W
    assert t % (2 * GRID_W) == 0 and rows >= NBR_SPAN_ROWS and rows // 2 >= 5
    assert t >= BLOCK + 2 * WINDOW and t % ROW_TILE == 0 and tc % ROW_TILE == 0
    alpha = (2.0 * depth) ** 0.25

    ada_rows = -(-(bsz + 1) // 8) * 8
    cs = jnp.zeros((ada_rows, d), F32).at[:bsz].set(c).at[bsz].set(c_ctx)
    mod = _ada(cs, w_ada, b_ada)
    tabs = _rope_tables(t)

    xl = x.reshape(bsz * t, d)
    xc = ctx.reshape(bsz * tc, d)
    fin_l = fin_c = None

    for l in range(depth):
        with_ctx = l < depth - 1
        lw = _layer_weights(l, w_in, gqa_q_gain, gqa_k_gain, mla_q_gain, mla_w_qb, mla_kv_gain,
                            mla_w_kvb, w_out, ln1_g, ln1_b, router_group_w, router_group_b,
                            router_expert_w, router_expert_b)
        wg = moe_w_gate[l].astype(BF16)
        wu = moe_w_up[l].astype(BF16)
        wd = moe_w_down[l].astype(BF16)
        ln2g, ln2b = ln2_g[l][None, :], ln2_b[l][None, :]

        def mods(lo, hi):
            parts = jnp.split(mod[l, lo:hi], 6, axis=-1)
            return [jnp.broadcast_to(p[:, None, :], (bsz, 1, d)) for p in parts]
        sh1, sc1, g1, sh2, sc2, g2 = mods(0, bsz)
        sh1c, sc1c, g1c, sh2c, sc2c, g2c = mods(bsz, bsz + 1)

        xl, pl_lat = _project(xl, fin_l, sc1, sh1, lw, tabs, t)
        xc, pl_ctx = _project(xc, fin_c, sc1c, sh1c, lw, None, tc)
        qa, ka, va, qb, kb, vb, qc, kc, vc, qd, kd, vd = [
            a.reshape(bsz, t, a.shape[1]) for a in pl_lat]
        qa_c, ka_c, va_c, qb_c, kb_c, vb_c, qc_c, kc_c, vc_c, qd_c, kd_c, vd_c = [
            a.reshape(bsz, tc, a.shape[1]) for a in pl_ctx]

        bias_tab = _nbr_bias_tables(na_bias[l], rows)
        out_a = _attention("mha", qa, [("nbr", ka, va), ("full", ka_c, va_c)], bias=bias_tab)
        out_b = _attention("gqa", qb, [("full", kb, vb), ("full", kb_c, vb_c)], bq=GLOBAL_BQ)
        out_c = _attention("gqa", qc, [("win", kc, vc), ("full", kc_c, vc_c)], sink=win_sink[l])
        out_d = _attention("mla", qd, [("full", kd, vd), ("full", kd_c, vd_c)], bq=GLOBAL_BQ)
        outs = [o.reshape(bsz * t, 256) for o in (out_a, out_b, out_c, out_d)]

        x1, h2x, ids = _out_proj(alpha, outs, xl, g1, sc2, sh2, lw, t)
        fin_l = (alpha, x1, _moe(h2x, ids, wg, wu, wd), g2, ln2g, ln2b)

        if with_ctx:
            outs_c = [
                _attention("mha", qa_c, [("full", ka_c, va_c)]),
                _attention("gqa", qb_c, [("full", kb_c, vb_c)]),
                _attention("gqa", qc_c, [("full", kc_c, vc_c)], sink=win_sink[l]),
                _attention("mla", qd_c, [("full", kd_c, vd_c)]),
            ]
            outs_c = [o.reshape(bsz * tc, 256) for o in outs_c]
            x1c, h2xc, ids_c = _out_proj(alpha, outs_c, xc, g1c, sc2c, sh2c, lw, tc)
            fin_c = (alpha, x1c, _moe(h2xc, ids_c, wg, wu, wd), g2c, ln2g, ln2b)

    return _finish(*fin_l, t).reshape(bsz, t, d)
```

```python
import functools

import jax
import jax.numpy as jnp
from jax import lax
from jax.experimental import pallas as pl
from jax.experimental.pallas import tpu as pltpu

F32 = jnp.float32
BF16 = jnp.bfloat16

GRID_W = 64
HEAD_DIM = 64
BLOCK = 128
WINDOW = 128
ROPE_THETA = 10000.0
NEG = -1e30
EPS = 1e-6
LOG2E = 1.4426950408889634

NA_ROWS = 8
NA_COLS = 16
NBR_SPAN_ROWS = 10
MLA_NOPE = 64
MLA_ROPE = 32
MLA_QK = MLA_NOPE + MLA_ROPE
N_GROUPS = 4
EXPERTS_PER_GROUP = 4
N_EXPERTS = 16
N_PAIRS = 6
N_BUCKETS = N_GROUPS * N_PAIRS

LANES = 128
VMEM_LIMIT = 48 * 1024 * 1024

ROW_TILE = 256
MOE_TILE = 256
KV_CHUNK = 2560
GLOBAL_BQ = 256
LOCAL_NSUB = 4
H2_EXT = 256


def _cparams(*sem):
    return pltpu.CompilerParams(dimension_semantics=sem, vmem_limit_bytes=VMEM_LIMIT)


def _dot(a, b):
    return jnp.dot(a, b, preferred_element_type=F32)


def _dot_nt(a, b):
    return lax.dot_general(a, b, (((1,), (1,)), ((), ())), preferred_element_type=F32)


def _split(a):
    hi = a.astype(BF16)
    lo = (a - hi.astype(F32)).astype(BF16)
    return hi, lo


def _dot3(a, b):
    ah, al = _split(a)
    bh, bl = _split(b)
    return _dot(al, bh) + _dot(ah, bl) + _dot(ah, bh)


def _ln(x):
    mu = jnp.mean(x, axis=-1, keepdims=True)
    xc = x - mu
    var = jnp.mean(xc * xc, axis=-1, keepdims=True)
    return xc * lax.rsqrt(var + EPS)


def _silu(g):
    return g / (1.0 + jnp.exp(-g))


def _ada_kernel(c_ref, w_ref, b_ref, o_ref):
    o_ref[0] = _dot3(_silu(c_ref[...]), w_ref[0]) + b_ref[0]


def _ada(cs, w_ada, b_ada):
    depth, d, n = w_ada.shape
    rows = cs.shape[0]
    tn = 1536
    return pl.pallas_call(
        _ada_kernel,
        grid=(depth, n // tn),
        in_specs=[pl.BlockSpec((rows, d), lambda l, j: (0, 0)),
                  pl.BlockSpec((1, d, tn), lambda l, j: (l, 0, j)),
                  pl.BlockSpec((1, 1, tn), lambda l, j: (l, 0, j))],
        out_specs=pl.BlockSpec((1, rows, tn), lambda l, j: (l, 0, j)),
        out_shape=jax.ShapeDtypeStruct((depth, rows, n), F32),
        compiler_params=_cparams("parallel", "parallel"),
    )(cs, w_ada, b_ada.reshape(depth, 1, n))


def _head_rms(x, bd, gain):
    hi, lo = _split(x * x)
    ss = _dot(lo, bd) + _dot(hi, bd)
    return x * lax.rsqrt(ss * (1.0 / HEAD_DIM) + EPS) * gain


def _rms(x, gain):
    return x * lax.rsqrt(jnp.mean(x * x, axis=-1, keepdims=True) + EPS) * gain


def _rope(x, tab_ref, shift):
    outs = []
    for j in range(x.shape[1] // LANES):
        xb = x[:, j * LANES:(j + 1) * LANES]
        outs.append(xb * tab_ref[0]
                    + pltpu.roll(xb, shift, 1) * tab_ref[1]
                    + pltpu.roll(xb, LANES - shift, 1) * tab_ref[2])
    return outs[0] if len(outs) == 1 else jnp.concatenate(outs, axis=1)


def _value_tiles(v):
    lane = lax.broadcasted_iota(jnp.int32, (v.shape[0], LANES), 1)
    fill = jnp.where(lane == HEAD_DIM, 1.0, 0.0)
    outs = []
    for j in range(v.shape[1] // LANES):
        blk = v[:, j * LANES:(j + 1) * LANES]
        outs.append(jnp.where(lane < HEAD_DIM, blk, fill))
        outs.append(jnp.where(lane < HEAD_DIM, pltpu.roll(blk, HEAD_DIM, 1), fill))
    return jnp.concatenate(outs, axis=1).astype(BF16)


def _proj_kernel(rope, fin_alpha, n_aliased, *refs):
    refs = list(refs)
    if fin_alpha is None:
        x = refs.pop(0)[...]
    else:
        x1_ref, y_ref, g2_ref, lng_ref, lnb_ref = refs[:5]
        refs = refs[5:]
        x = _ln(fin_alpha * x1_ref[...] + g2_ref[0] * y_ref[...]) * lng_ref[...] + lnb_ref[...]
    (sc_ref, sh_ref, w_ref, gq_ref, gk_ref, gmq_ref, gmkv_ref,
     wqb_ref, wkvb_ref, bd_ref) = refs[:10]
    refs = refs[10:]
    if rope:
        t64_ref, tq_ref, tk_ref = refs[:3]
        refs = refs[3:]
    refs = refs[n_aliased:]
    if fin_alpha is not None:
        refs.pop(0)[...] = x
    qa, ka, va, qb, kb, vb, qc, kc, vc, qd, kd, vd = refs

    h = _ln(x) * (1.0 + sc_ref[0]) + sh_ref[0]
    hb = h.astype(BF16)
    sq = HEAD_DIM ** -0.5 * LOG2E

    def proj(a, b):
        return _dot(hb, w_ref[:, a:b])

    pa = proj(0, 768)
    qa[...] = (pa[:, 0:256] * sq).astype(BF16)
    ka[...] = pa[:, 256:512].astype(BF16)
    va[...] = _value_tiles(pa[:, 512:768])

    pb = proj(768, 1280)
    q = _head_rms(pb[:, 0:256], bd_ref[...], gq_ref[...])
    k = _head_rms(pb[:, 256:384], bd_ref[0:128, 0:128], gk_ref[...])
    if rope:
        q = _rope(q, t64_ref, 16)
        k = _rope(k, t64_ref, 16)
    qb[...] = (q * sq).astype(BF16)
    kb[...] = k.astype(BF16)
    vb[...] = _value_tiles(pb[:, 384:512])

    pc = proj(1280, 1792)
    q = pc[:, 0:256]
    k = pc[:, 256:384]
    if rope:
        q = _rope(q, t64_ref, 16)
        k = _rope(k, t64_ref, 16)
    qc[...] = (q * sq).astype(BF16)
    kc[...] = k.astype(BF16)
    vc[...] = _value_tiles(pc[:, 384:512])

    pd = proj(1792, 2304)
    cq = _rms(pd[:, 0:256], gmq_ref[...])
    q = _dot(cq.astype(BF16), wqb_ref[...])
    if rope:
        q = _rope(q, tq_ref, 8)
    qd[...] = (q * (MLA_QK ** -0.5 * LOG2E)).astype(BF16)
    ckv = _rms(pd[:, 256:384], gmkv_ref[...])
    kvu = _dot(ckv.astype(BF16), wkvb_ref[...])
    kr = pd[:, 384:512]
    if rope:
        kr = _rope(kr, tk_ref, 8)
    kr = pltpu.roll(kr, MLA_NOPE, 1)
    kd[...] = jnp.concatenate(
        [kvu[:, j * LANES:(j + 1) * LANES] + kr for j in range(4)], axis=1).astype(BF16)
    lane = lax.broadcasted_iota(jnp.int32, (kr.shape[0], LANES), 1)
    fill = jnp.where(lane == HEAD_DIM, 1.0, 0.0)
    vd[...] = jnp.concatenate(
        [kvu[:, (4 + j) * LANES:(5 + j) * LANES] + fill for j in range(4)], axis=1).astype(BF16)


_PROJ_WIDTHS = (256, 256, 512, 256, 128, 256, 256, 128, 256, 512, 512, 512)


_Q_OUTS = (0, 3, 6, 9)
_KV_OUTS = (1, 2, 4, 5, 7, 8, 10, 11)


def _project(x2d, fin, sc, sh, lw, tabs, tokens_per_batch, kv_rows, kv_row0, kv_prev=None):
    n, d = (x2d if fin is None else fin[1]).shape
    tm = ROW_TILE
    per_b = tokens_per_batch // tm
    bsz = n // tokens_per_batch
    rope = tabs is not None
    row = lambda i: (i, 0)
    full = lambda i: (0, 0)
    perb = lambda i: (i // per_b, 0, 0)
    kvrow = lambda i: ((i // per_b) * (kv_rows // tm) + kv_row0 // tm + i % per_b, 0)
    if fin is None:
        in_specs = [pl.BlockSpec((tm, d), row)]
        args = [x2d]
    else:
        in_specs = [pl.BlockSpec((tm, d), row), pl.BlockSpec((tm, d), row), pl.BlockSpec((1, 1, d), perb),
                    pl.BlockSpec((1, d), full), pl.BlockSpec((1, d), full)]
        args = list(fin[1:])
    in_specs += [pl.BlockSpec((1, 1, d), perb), pl.BlockSpec((1, 1, d), perb),
                 pl.BlockSpec(lw["w_in"].shape, full),
                 pl.BlockSpec((1, 256), full), pl.BlockSpec((1, 128), full),
                 pl.BlockSpec((1, 256), full), pl.BlockSpec((1, 128), full),
                 pl.BlockSpec(lw["w_qb"].shape, full), pl.BlockSpec(lw["w_kvb"].shape, full),
                 pl.BlockSpec((256, 256), full)]
    args += [sc, sh, lw["w_in"], lw["gq"], lw["gk"], lw["gmq"], lw["gmkv"],
             lw["w_qb"], lw["w_kvb"], lw["bd"]]
    if rope:
        tab = lambda i: (0, i % per_b, 0)
        in_specs += [pl.BlockSpec((3, tm, LANES), tab)] * 3
        args += list(tabs)
    out_specs = [pl.BlockSpec((tm, w), kvrow if i in _KV_OUTS else row) for i, w in enumerate(_PROJ_WIDTHS)]
    out_shape = [jax.ShapeDtypeStruct((bsz * kv_rows if i in _KV_OUTS else n, w), BF16)
                 for i, w in enumerate(_PROJ_WIDTHS)]
    lead = 0 if fin is None else 1
    if fin is not None:
        out_specs.insert(0, pl.BlockSpec((tm, d), row))
        out_shape.insert(0, jax.ShapeDtypeStruct((n, d), F32))
    aliases = {}
    if kv_prev is not None:
        for a, i in zip(kv_prev, _KV_OUTS):
            aliases[len(args)] = lead + i
            in_specs.append(pl.BlockSpec(memory_space=pl.ANY))
            args.append(a)
    outs = pl.pallas_call(
        functools.partial(_proj_kernel, rope, None if fin is None else fin[0], len(aliases)),
        grid=(n // tm,),
        in_specs=in_specs,
        out_specs=out_specs,
        out_shape=out_shape,
        input_output_aliases=aliases,
        compiler_params=_cparams("parallel"),
    )(*args)
    return (x2d, list(outs)) if fin is None else (outs[0], list(outs[1:]))


def _softmax_step(s, vts, bq, m, acc):
    m_new = jnp.maximum(m, jnp.max(s, axis=1, keepdims=True))
    alpha = jnp.exp2(m - m_new)
    p = jnp.exp2(s - m_new).astype(BF16)
    if len(vts) == 1:
        pv = _dot(p, vts[0])
    else:
        pv = jnp.concatenate([_dot(p[:bq], vts[0]), _dot(p[bq:], vts[1])], axis=0)
    return [m_new, alpha * acc + pv]


def _attn_kernel(mode, bq, nsub, srcs, single_shot, has_bias, has_sink, *refs):
    refs = list(refs)
    q_ref = refs.pop(0)
    k_ref, v_ref = refs.pop(0), refs.pop(0)
    bias_refs = [refs.pop(0) for _ in range(nsub)] if has_bias else None
    sink_ref = refs.pop(0) if has_sink else None
    o_ref = refs.pop(0)

    for u in range(nsub):
        qi = pl.program_id(1) * nsub + u
        rows = slice(u * bq, (u + 1) * bq)
        rows2 = 2 * bq
        lo = lax.broadcasted_iota(jnp.int32, (bq, LANES), 1) < HEAD_DIM

        qss, kcs, vcs = [], [], []
        for j in range(2):
            if mode == "mla":
                qblk = q_ref[0, rows, 2 * LANES * j:2 * LANES * (j + 1)].astype(F32)
                first = lax.broadcasted_iota(jnp.int32, (bq, 2 * LANES), 1) < LANES
                qs = jnp.concatenate([jnp.where(first, qblk, 0.0), jnp.where(first, 0.0, qblk)], axis=0)
                kcs.append((2 * LANES * j, 2 * LANES * (j + 1)))
                vcs.append((2 * LANES * j, 2 * LANES * j + LANES))
            else:
                qblk = q_ref[0, rows, LANES * j:LANES * (j + 1)].astype(F32)
                if mode == "mha":
                    qe, qo = jnp.where(lo, qblk, 0.0), jnp.where(lo, 0.0, qblk)
                    kcs.append((LANES * j, LANES * (j + 1)))
                    vcs.append((2 * LANES * j, 2 * LANES * j + LANES))
                else:
                    rolled = pltpu.roll(qblk, HEAD_DIM, 1)
                    if j == 0:
                        qe, qo = jnp.where(lo, qblk, 0.0), jnp.where(lo, rolled, 0.0)
                    else:
                        qe, qo = jnp.where(lo, 0.0, rolled), jnp.where(lo, 0.0, qblk)
                    kcs.append((0, LANES))
                    vcs.append((LANES * j,))
                qs = jnp.concatenate([qe, qo], axis=0)
            qss.append(qs.astype(BF16))

        def pv(p, vts):
            if len(vts) == 1:
                return _dot(p, vts[0])
            return jnp.concatenate([_dot(p[:bq], vts[0]), _dot(p[bq:], vts[1])], axis=0)

        def tiles(j, off, n):
            kt = k_ref[0, pl.ds(off, n), kcs[j][0]:kcs[j][1]]
            return kt, [v_ref[0, pl.ds(off, n), v0:v0 + LANES] for v0 in vcs[j]]

        def block_scores(j, kind, row0, nrows):
            if kind == "full":
                kt, vts = tiles(j, row0, nrows)
                return _dot_nt(qss[j], kt), vts
            if kind == "win":
                span = bq + 2 * WINDOW
                start = pl.multiple_of(jnp.clip((qi - 1) * bq, 0, nrows - span), bq)
                r = lax.broadcasted_iota(jnp.int32, (rows2, span), 0)
                qpos = qi * bq + jnp.where(r >= bq, r - bq, r)
                kpos = start + lax.broadcasted_iota(jnp.int32, (rows2, span), 1)
                kt, vts = tiles(j, row0 + start, span)
                return jnp.where(jnp.abs(qpos - kpos) <= WINDOW, _dot_nt(qss[j], kt), NEG), vts
            span = NBR_SPAN_ROWS * GRID_W
            ks = jnp.clip(2 * qi - NA_ROWS // 2, 0, nrows // GRID_W - NBR_SPAN_ROWS)
            kt, vts = tiles(j, row0 + pl.multiple_of(ks * GRID_W, GRID_W), span)
            return _dot_nt(qss[j], kt) + bias_refs[u][0, j], vts

        if single_shot:
            state = []
            for j in range(2):
                parts = [block_scores(j, kind, row0, nrows) for kind, row0, nrows in srcs]
                m = functools.reduce(jnp.maximum, [jnp.max(sc, axis=1, keepdims=True) for sc, _ in parts])
                acc = functools.reduce(lambda a, b: a + b,
                                       [pv(jnp.exp2(sc - m).astype(BF16), vts) for sc, vts in parts])
                state += [m, acc]
        else:
            state = [jnp.full((rows2, 1), NEG, F32), jnp.zeros((rows2, LANES), F32)] * 2
            for kind, row0, nrows in srcs:
                for j in range(2):
                    sc, vts = block_scores(j, kind, row0, nrows)
                    state[2 * j:2 * j + 2] = _softmax_step(sc, vts, bq, state[2 * j], state[2 * j + 1])

        for j in range(2):
            m, acc = state[2 * j], state[2 * j + 1]
            l = acc[:, HEAD_DIM:HEAD_DIM + 1]
            if has_sink:
                sk = jnp.concatenate([jnp.full((bq, 1), sink_ref[2 * j] * LOG2E, F32),
                                      jnp.full((bq, 1), sink_ref[2 * j + 1] * LOG2E, F32)], axis=0)
                mf = jnp.maximum(m, sk)
                a = jnp.exp2(m - mf)
                l = l * a + jnp.exp2(sk - mf)
                acc = acc * a
            o = acc / l
            o_ref[0, rows, LANES * j:LANES * (j + 1)] = jnp.where(
                lo, o[:bq], pltpu.roll(o[bq:], HEAD_DIM, 1)).astype(BF16)


def _key_ranges(nrows):
    n = -(-nrows // KV_CHUNK)
    size = max(nrows // n // 256, 1) * 256
    starts = [i * size for i in range(n)]
    return [("full", r0, (nrows if i == n - 1 else r0 + size) - r0) for i, r0 in enumerate(starts)]


def _attention(mode, q, k, v, parts, kv_block=None, bias=None, sink=None, bq=BLOCK, nsub=1, single_shot=True):
    b, tq, wq = q.shape
    srcs = tuple(parts)
    rows, blk = (k.shape[1], 0) if kv_block is None else kv_block
    in_specs = [pl.BlockSpec((1, nsub * bq, wq), lambda bi, qi: (bi, qi, 0)),
                pl.BlockSpec((1, rows, k.shape[2]), lambda bi, qi: (bi, blk, 0)),
                pl.BlockSpec((1, rows, v.shape[2]), lambda bi, qi: (bi, blk, 0))]
    args = [q, k, v]
    if bias is not None:
        npair = tq // bq

        for u in range(nsub):
            def bias_idx(bi, qi, u=u):
                p = qi * nsub + u
                return (jnp.where(p < 2, p, jnp.where(p >= npair - 2, p - (npair - 2) + 3, 2)), 0, 0, 0)
            in_specs.append(pl.BlockSpec((1,) + bias.shape[1:], bias_idx))
            args.append(bias)
    if sink is not None:
        in_specs.append(pl.BlockSpec(memory_space=pltpu.SMEM))
        args.append(sink)
    return pl.pallas_call(
        functools.partial(_attn_kernel, mode, bq, nsub, srcs, single_shot, bias is not None, sink is not None),
        grid=(b, tq // (nsub * bq)),
        in_specs=in_specs,
        out_specs=pl.BlockSpec((1, nsub * bq, 256), lambda bi, qi: (bi, qi, 0)),
        out_shape=jax.ShapeDtypeStruct((b, tq, 256), BF16),
        compiler_params=_cparams("parallel", "arbitrary"),
    )(*args)


def _route(logits):
    lane = lax.broadcasted_iota(jnp.int32, logits.shape, 1).astype(F32)
    big = float(LANES)
    is_g = lane < N_GROUPS
    lg = jnp.where(is_g, logits, NEG)
    gmax = jnp.max(lg, axis=1, keepdims=True)
    grp = jnp.min(jnp.where(lg == gmax, lane, big), axis=1, keepdims=True)
    den = jnp.sum(jnp.where(is_g, jnp.exp(lg - gmax), 0.0), axis=1, keepdims=True)
    p_grp = 1.0 / den
    e0 = N_GROUPS + EXPERTS_PER_GROUP * grp
    in_grp = (lane >= e0) & (lane < e0 + EXPERTS_PER_GROUP)
    le = jnp.where(in_grp, logits, NEG)
    v1 = jnp.max(le, axis=1, keepdims=True)
    i1 = jnp.min(jnp.where(in_grp & (le == v1), lane, big), axis=1, keepdims=True)
    rest = in_grp & (lane != i1)
    le2 = jnp.where(rest, logits, NEG)
    v2 = jnp.max(le2, axis=1, keepdims=True)
    i2 = jnp.min(jnp.where(rest & (le2 == v2), lane, big), axis=1, keepdims=True)
    t = jnp.exp(v2 - v1)
    w1 = p_grp / (1.0 + t)
    w2 = p_grp * t / (1.0 + t)
    first_lower = i1 < i2
    e_lo = jnp.where(first_lower, i1, i2) - N_GROUPS
    e_hi = jnp.where(first_lower, i2, i1) - N_GROUPS
    w_lo = jnp.where(first_lower, w1, w2)
    w_hi = jnp.where(first_lower, w2, w1)
    a = e_lo - EXPERTS_PER_GROUP * grp
    b = e_hi - EXPERTS_PER_GROUP * grp
    bucket = grp * N_PAIRS + a * (2 * EXPERTS_PER_GROUP - 1 - a) * 0.5 + (b - a - 1.0)
    return jnp.where(lane == 0, e_lo, jnp.where(lane == 1, e_hi, jnp.where(lane == 2, w_lo,
                     jnp.where(lane == 3, w_hi, jnp.where(lane == 4, bucket, 0.0)))))


def _out_kernel(alpha, oa, ob, oc, od, x_ref, g1_ref, sc2_ref, sh2_ref, w_ref, lng_ref, lnb_ref,
                wrh_ref, wrl_ref, br_ref, x1_ref, h2_ref, ids_ref):
    mix = (_dot(oa[...], w_ref[0:256, :]) + _dot(ob[...], w_ref[256:512, :])
           + _dot(oc[...], w_ref[512:768, :]) + _dot(od[...], w_ref[768:1024, :]))
    x1 = _ln(alpha * x_ref[...] + g1_ref[0] * mix) * lng_ref[...] + lnb_ref[...]
    x1_ref[...] = x1
    h2 = _ln(x1) * (1.0 + sc2_ref[0]) + sh2_ref[0]
    hh, hl = _split(h2)
    logits = _dot(hl, wrh_ref[...]) + _dot(hh, wrl_ref[...]) + _dot(hh, wrh_ref[...]) + br_ref[...]
    route = _route(logits)
    rh, rl = _split(route)
    d = h2.shape[1]
    h2_ref[:, 0:d] = hh
    h2_ref[:, d:d + LANES] = rh
    h2_ref[:, d + LANES:d + 2 * LANES] = rl
    pick = ((lax.broadcasted_iota(jnp.int32, (8, LANES), 0) == 0)
            & (lax.broadcasted_iota(jnp.int32, (8, LANES), 1) == 4)).astype(BF16)
    ids_ref[...] = _dot_nt(pick, rh)


def _out_proj(alpha, outs, x2d, g1, sc2, sh2, lw, tokens_per_batch):
    n, d = x2d.shape
    tm = ROW_TILE
    per_b = tokens_per_batch // tm
    row = lambda i: (i, 0)
    full = lambda i: (0, 0)
    perb = lambda i: (i // per_b, 0, 0)
    return pl.pallas_call(
        functools.partial(_out_kernel, alpha),
        grid=(n // tm,),
        in_specs=[pl.BlockSpec((tm, 256), row)] * 4 + [
            pl.BlockSpec((tm, d), row),
            pl.BlockSpec((1, 1, d), perb), pl.BlockSpec((1, 1, d), perb), pl.BlockSpec((1, 1, d), perb),
            pl.BlockSpec((d, d), full), pl.BlockSpec((1, d), full), pl.BlockSpec((1, d), full),
            pl.BlockSpec((d, LANES), full), pl.BlockSpec((d, LANES), full), pl.BlockSpec((1, LANES), full)],
        out_specs=[pl.BlockSpec((tm, d), row), pl.BlockSpec((tm, d + H2_EXT), row),
                   pl.BlockSpec((8, tm), lambda i: (0, i))],
        out_shape=[jax.ShapeDtypeStruct((n, d), F32), jax.ShapeDtypeStruct((n, d + H2_EXT), BF16),
                   jax.ShapeDtypeStruct((8, n), F32)],
        compiler_params=_cparams("parallel"),
    )(*outs, x2d, g1, sc2, sh2, lw["w_out"], lw["ln1_g"], lw["ln1_b"], lw["wr_hi"], lw["wr_lo"], lw["br"])


def _moe_kernel(e0_ref, e1_ref, valid_ref, xs_ref, wg0, wg1, wu0, wu1, wd0, wd1, o_ref, *wb):
    t = pl.program_id(0)
    d = o_ref.shape[1]
    prev = jnp.maximum(t - 1, 0)
    changed = (t == 0) | (e0_ref[t] != e0_ref[prev]) | (e1_ref[t] != e1_ref[prev])

    @pl.when(changed)
    def _():
        for src, dst in zip((wg0, wg1, wu0, wu1, wd0, wd1), wb):
            dst[...] = src[0, 0].astype(BF16)

    @pl.when(valid_ref[t] == 1)
    def _():
        x = xs_ref[:, 0:d]
        rt = xs_ref[:, d:d + LANES].astype(F32) + xs_ref[:, d + LANES:d + 2 * LANES].astype(F32)

        def expert(wg, wu, wd):
            a = _silu(_dot(x, wg[...])) * _dot(x, wu[...])
            return _dot(a.astype(BF16), wd[...])

        y = rt[:, 2:3] * expert(wb[0], wb[2], wb[4]) + rt[:, 3:4] * expert(wb[1], wb[3], wb[5])
        o_ref[...] = y.astype(o_ref.dtype)

    @pl.when(valid_ref[t] == 0)
    def _():
        o_ref[...] = jnp.zeros_like(o_ref)


def _moe_experts(layer, e0, e1, valid, xs, wg, wu, wd):
    p, dx = xs.shape
    d = dx - H2_EXT
    tm = MOE_TILE
    de = wg.shape[3]
    row = lambda t, e0, e1, v: (t, 0)
    w0 = lambda t, e0, e1, v: (layer, e0[t], 0, 0)
    w1 = lambda t, e0, e1, v: (layer, e1[t], 0, 0)
    return pl.pallas_call(
        _moe_kernel,
        grid_spec=pltpu.PrefetchScalarGridSpec(
            num_scalar_prefetch=3,
            grid=(p // tm,),
            in_specs=[pl.BlockSpec((tm, dx), row),
                      pl.BlockSpec((1, 1, d, de), w0), pl.BlockSpec((1, 1, d, de), w1),
                      pl.BlockSpec((1, 1, d, de), w0), pl.BlockSpec((1, 1, d, de), w1),
                      pl.BlockSpec((1, 1, de, d), w0), pl.BlockSpec((1, 1, de, d), w1)],
            out_specs=pl.BlockSpec((tm, d), row),
            scratch_shapes=[pltpu.VMEM((d, de), BF16)] * 4 + [pltpu.VMEM((de, d), BF16)] * 2),
        out_shape=jax.ShapeDtypeStruct((p, d), BF16),
        compiler_params=_cparams("arbitrary"),
    )(e0, e1, valid, xs, wg, wg, wu, wu, wd, wd)


_PAIR_TABLE = ((0, 1), (0, 2), (0, 3), (1, 2), (1, 3), (2, 3))


def _moe(layer, h2x, ids, wg, wu, wd):
    n = h2x.shape[0]
    tm = MOE_TILE
    n_tiles = n // tm + N_BUCKETS
    bucket = ids[0].astype(jnp.int32)
    tok = jnp.arange(n, dtype=jnp.int32)
    sorted_bucket, order = lax.sort((bucket, tok), num_keys=1)
    _, rank = lax.sort((order, tok), num_keys=1)

    bids = jnp.arange(N_BUCKETS + 1, dtype=jnp.int32)
    edges = jnp.sum(sorted_bucket[None, :] < bids[:, None], axis=1, dtype=jnp.int32)
    counts = edges[1:] - edges[:-1]
    src_off = edges[:-1]
    tiles_per = (counts + tm - 1) // tm
    tile_end = jnp.cumsum(tiles_per)
    tile_start = tile_end - tiles_per

    tile_ids = jnp.arange(n_tiles, dtype=jnp.int32)
    valid = (tile_ids < tile_end[-1]).astype(jnp.int32)
    tile_bucket = jnp.sum(tile_ids[:, None] >= tile_end[None, :], axis=1, dtype=jnp.int32)
    last_bucket = jnp.sum(tile_end[-1] - 1 >= tile_end, dtype=jnp.int32)
    tile_bucket = jnp.minimum(jnp.where(valid == 1, tile_bucket, last_bucket), N_BUCKETS - 1)
    onehot_tb = tile_bucket[:, None] == bids[None, :N_BUCKETS]
    pick = lambda table: jnp.sum(jnp.where(onehot_tb, table[None, :], 0), axis=1)
    pairs = jnp.asarray(_PAIR_TABLE, dtype=jnp.int32)
    tg = tile_bucket // N_PAIRS
    e0 = tg * EXPERTS_PER_GROUP + pairs[tile_bucket % N_PAIRS, 0]
    e1 = tg * EXPERTS_PER_GROUP + pairs[tile_bucket % N_PAIRS, 1]

    first_row = (tile_ids - pick(tile_start)) * tm
    within = first_row[:, None] + jnp.arange(tm, dtype=jnp.int32)[None, :]
    row_ok = (within < pick(counts)[:, None]) & (valid[:, None] == 1)
    idx = jnp.clip(pick(src_off)[:, None] + within, 0, n - 1)
    src = jnp.where(row_ok, order.at[idx.reshape(-1)].get(mode="promise_in_bounds").reshape(idx.shape), 0).reshape(-1)

    xs = h2x.at[src].get(mode="promise_in_bounds")
    ys = _moe_experts(layer, e0, e1, valid, xs, wg, wu, wd)

    onehot_b = bucket[None, :] == bids[:N_BUCKETS, None]
    base = jnp.sum(jnp.where(onehot_b, (tile_start * tm - src_off)[:, None], 0), axis=0)
    return ys.at[base + rank].get(mode="promise_in_bounds")


def _fin_kernel(alpha, x_ref, y_ref, g2_ref, lng_ref, lnb_ref, o_ref):
    o_ref[...] = _ln(alpha * x_ref[...] + g2_ref[0] * y_ref[...]) * lng_ref[...] + lnb_ref[...]


def _finish(alpha, x1, y, g2, ln_g, ln_b, tokens_per_batch):
    n, d = x1.shape
    tm = ROW_TILE
    per_b = tokens_per_batch // tm
    row = lambda i: (i, 0)
    full = lambda i: (0, 0)
    return pl.pallas_call(
        functools.partial(_fin_kernel, alpha),
        grid=(n // tm,),
        in_specs=[pl.BlockSpec((tm, d), row), pl.BlockSpec((tm, d), row),
                  pl.BlockSpec((1, 1, d), lambda i: (i // per_b, 0, 0)),
                  pl.BlockSpec((1, d), full), pl.BlockSpec((1, d), full)],
        out_specs=pl.BlockSpec((tm, d), row),
        out_shape=jax.ShapeDtypeStruct((n, d), F32),
        compiler_params=_cparams("parallel"),
    )(x1, y, g2, ln_g, ln_b)


def _rope_tables(t):
    pos = jnp.arange(t)
    row = (pos // GRID_W).astype(F32)[:, None]
    col = (pos % GRID_W).astype(F32)[:, None]
    lane = jnp.arange(LANES)

    def build(d, half, active):
        axis_col = (d // half) % 2 == 1
        w = d % half
        first = w < half // 2
        f = (w % (half // 2)).astype(F32)
        freq = ROPE_THETA ** (-(2.0 * f) / half)
        ang = jnp.where(axis_col[None, :], col, row) * freq[None, :]
        cos = jnp.where(active[None, :], jnp.cos(ang), 1.0)
        sin = jnp.where(active[None, :], jnp.sin(ang), 0.0)
        return jnp.stack([cos, jnp.where(first[None, :], 0.0, sin), jnp.where(first[None, :], -sin, 0.0)])

    t64 = build(lane % HEAD_DIM, HEAD_DIM // 2, lane >= 0)
    in_rope = (lane >= MLA_NOPE) & (lane < MLA_QK)
    tq = build(jnp.where(in_rope, lane - MLA_NOPE, 0), MLA_ROPE // 2, in_rope)
    in_kr = lane < MLA_ROPE
    tk = build(jnp.where(in_kr, lane, 0), MLA_ROPE // 2, in_kr)
    return t64, tq, tk


def _nbr_bias_tables(na_bias, rows):
    heads = na_bias.shape[0]
    npair = rows // 2
    kr = min(NA_ROWS, rows)
    qcol = jnp.arange(GRID_W)[:, None]
    kcol = jnp.arange(GRID_W)[None, :]
    c0 = jnp.clip(qcol - NA_COLS // 2, 0, GRID_W - NA_COLS)
    col_ok = (kcol >= c0) & (kcol < c0 + NA_COLS)
    col_off = kcol - qcol + NA_COLS - 1
    onehot = (col_off[None] == jnp.arange(2 * NA_COLS - 1)[:, None, None]).astype(F32)
    colmat = jnp.einsum("hdc,cqk->hdqk", na_bias, onehot, precision=lax.Precision.HIGHEST) * LOG2E
    colmat = jnp.where(col_ok[None, None], colmat, NEG)
    outside = jnp.full((heads, GRID_W, GRID_W), NEG, F32)
    tabs = []
    for p in (0, 1, 2, npair - 2, npair - 1):
        ks = min(max(2 * p - NA_ROWS // 2, 0), rows - NBR_SPAN_ROWS)
        per_qrow = []
        for qr in range(2):
            r = 2 * p + qr
            r0 = min(max(r - kr // 2, 0), rows - kr)
            blocks = []
            for kri in range(NBR_SPAN_ROWS):
                krow = ks + kri
                blocks.append(colmat[:, krow - r + NA_ROWS - 1] if r0 <= krow < r0 + kr else outside)
            per_qrow.append(jnp.concatenate(blocks, axis=-1))
        tab = jnp.concatenate(per_qrow, axis=1)
        tabs.append(tab.reshape(heads // 2, 4 * GRID_W, NBR_SPAN_ROWS * GRID_W))
    return jnp.stack(tabs)


def _layer_weights(l, w_in, gqa_q_gain, gqa_k_gain, mla_q_gain, mla_w_qb, mla_kv_gain, mla_w_kvb,
                   w_out, ln1_g, ln1_b, router_group_w, router_group_b, router_expert_w, router_expert_b):
    d = w_in.shape[1]
    d_proj = w_in.shape[2]
    lw = {}
    lw["w_in"] = jnp.pad(w_in[l], ((0, 0), (0, 2304 - d_proj))).astype(BF16)
    lw["gq"] = jnp.tile(gqa_q_gain[l], 4)[None, :]
    lw["gk"] = jnp.tile(gqa_k_gain[l], 2)[None, :]
    lw["gmq"] = mla_q_gain[l][None, :]
    lw["gmkv"] = mla_kv_gain[l][None, :]
    wqb = mla_w_qb[l].reshape(-1, 4, MLA_QK)
    lw["w_qb"] = jnp.pad(wqb, ((0, 0), (0, 0), (0, LANES - MLA_QK))).reshape(-1, 4 * LANES).astype(BF16)
    wkvb = mla_w_kvb[l].reshape(-1, 4, 2 * HEAD_DIM)
    k_part = jnp.pad(wkvb[:, :, :MLA_NOPE], ((0, 0), (0, 0), (0, LANES - MLA_NOPE))).reshape(-1, 4 * LANES)
    v_part = jnp.pad(wkvb[:, :, MLA_NOPE:], ((0, 0), (0, 0), (0, LANES - HEAD_DIM))).reshape(-1, 4 * LANES)
    lw["w_kvb"] = jnp.concatenate([k_part, v_part], axis=1).astype(BF16)
    idx = jnp.arange(256) // HEAD_DIM
    lw["bd"] = (idx[:, None] == idx[None, :]).astype(BF16)
    lw["w_out"] = w_out[l].astype(BF16)
    lw["ln1_g"] = ln1_g[l][None, :]
    lw["ln1_b"] = ln1_b[l][None, :]
    wr = jnp.concatenate([router_group_w[l], router_expert_w[l]], axis=1)
    wr = jnp.pad(wr, ((0, 0), (0, LANES - wr.shape[1])))
    lw["wr_hi"] = wr.astype(BF16)
    lw["wr_lo"] = (wr - lw["wr_hi"].astype(F32)).astype(BF16)
    br = jnp.concatenate([router_group_b[l], router_expert_b[l]])
    lw["br"] = jnp.pad(br, (0, LANES - br.shape[0]))[None, :]
    return lw


def kernel(x, c, ctx, c_ctx, w_ada, b_ada, w_in, na_bias, gqa_q_gain, gqa_k_gain, win_sink, mla_q_gain, mla_w_qb, mla_kv_gain, mla_w_kvb, w_out, ln1_g, ln1_b, router_group_w, router_group_b, router_expert_w, router_expert_b, moe_w_gate, moe_w_up, moe_w_down, ln2_g, ln2_b):
    bsz, t, d = x.shape
    tc = ctx.shape[1]
    depth = w_ada.shape[0]
    rows = t // GRID_W
    assert t % (2 * GRID_W) == 0 and rows >= NBR_SPAN_ROWS and rows // 2 >= 5
    assert t >= BLOCK + 2 * WINDOW and t % ROW_TILE == 0 and tc % ROW_TILE == 0 and t % tc == 0
    assert t % (LOCAL_NSUB * BLOCK) == 0 and t % GLOBAL_BQ == 0
    s_tot = t + tc
    alpha = (2.0 * depth) ** 0.25

    ada_rows = -(-(bsz + 1) // 8) * 8
    cs = jnp.zeros((ada_rows, d), F32).at[:bsz].set(c).at[bsz].set(c_ctx)
    mod = _ada(cs, w_ada, b_ada)
    tabs = _rope_tables(t)

    xl = x.reshape(bsz * t, d)
    xc = ctx.reshape(bsz * tc, d)
    fin_l = fin_c = None

    for l in range(depth):
        with_ctx = l < depth - 1
        lw = _layer_weights(l, w_in, gqa_q_gain, gqa_k_gain, mla_q_gain, mla_w_qb, mla_kv_gain,
                            mla_w_kvb, w_out, ln1_g, ln1_b, router_group_w, router_group_b,
                            router_expert_w, router_expert_b)
        ln2g, ln2b = ln2_g[l][None, :], ln2_b[l][None, :]

        def mods(lo, hi):
            parts = jnp.split(mod[l, lo:hi], 6, axis=-1)
            return [jnp.broadcast_to(p[:, None, :], (bsz, 1, d)) for p in parts]
        sh1, sc1, g1, sh2, sc2, g2 = mods(0, bsz)
        sh1c, sc1c, g1c, sh2c, sc2c, g2c = mods(bsz, bsz + 1)

        xl, pl_lat = _project(xl, fin_l, sc1, sh1, lw, tabs, t, s_tot, 0)
        xc, pl_ctx = _project(xc, fin_c, sc1c, sh1c, lw, None, tc, s_tot, t,
                              kv_prev=[pl_lat[i] for i in _KV_OUTS])
        qa, qb, qc, qd = [pl_lat[i].reshape(bsz, t, -1) for i in _Q_OUTS]
        qa_c, qb_c, qc_c, qd_c = [pl_ctx[i].reshape(bsz, tc, -1) for i in _Q_OUTS]
        ka, va, kb, vb, kc, vc, kd, vd = [pl_ctx[i].reshape(bsz, s_tot, -1) for i in _KV_OUTS]

        bias_tab = _nbr_bias_tables(na_bias[l], rows)
        everything = dict(parts=_key_ranges(s_tot), bq=GLOBAL_BQ, single_shot=False)
        out_a = _attention("mha", qa, ka, va, [("nbr", 0, t), ("full", t, tc)], bias=bias_tab, nsub=LOCAL_NSUB)
        out_b = _attention("gqa", qb, kb, vb, **everything)
        out_c = _attention("gqa", qc, kc, vc, [("win", 0, t), ("full", t, tc)], sink=win_sink[l],
                           nsub=LOCAL_NSUB)
        out_d = _attention("mla", qd, kd, vd, **everything)
        outs = [o.reshape(bsz * t, 256) for o in (out_a, out_b, out_c, out_d)]

        x1, h2x, ids = _out_proj(alpha, outs, xl, g1, sc2, sh2, lw, t)
        fin_l = (alpha, x1, _moe(l, h2x, ids, moe_w_gate, moe_w_up, moe_w_down), g2, ln2g, ln2b)

        if with_ctx:
            ctx_only = dict(parts=[("full", 0, tc)], kv_block=(tc, t // tc))
            outs_c = [
                _attention("mha", qa_c, ka, va, **ctx_only),
                _attention("gqa", qb_c, kb, vb, **ctx_only),
                _attention("gqa", qc_c, kc, vc, sink=win_sink[l], **ctx_only),
                _attention("mla", qd_c, kd, vd, **ctx_only),
            ]
            outs_c = [o.reshape(bsz * tc, 256) for o in outs_c]
            x1c, h2xc, ids_c = _out_proj(alpha, outs_c, xc, g1c, sc2c, sh2c, lw, tc)
            fin_c = (alpha, x1c, _moe(l, h2xc, ids_c, moe_w_gate, moe_w_up, moe_w_down), g2c, ln2g, ln2b)

    return _finish(*fin_l, t).reshape(bsz, t, d)
```

```python
import functools

import jax
import jax.numpy as jnp
from jax import lax
from jax.experimental import pallas as pl
from jax.experimental.pallas import tpu as pltpu

F32 = jnp.float32
BF16 = jnp.bfloat16

GRID_W = 64
HEAD_DIM = 64
BLOCK = 128
WINDOW = 128
ROPE_THETA = 10000.0
NEG = -1e30
EPS = 1e-6
LOG2E = 1.4426950408889634

NA_ROWS = 8
NA_COLS = 16
NBR_SPAN_ROWS = 10
MLA_NOPE = 64
MLA_ROPE = 32
MLA_QK = MLA_NOPE + MLA_ROPE
N_GROUPS = 4
EXPERTS_PER_GROUP = 4
N_EXPERTS = 16
N_PAIRS = 6
N_BUCKETS = N_GROUPS * N_PAIRS

LANES = 128
VMEM_LIMIT = 48 * 1024 * 1024

ROW_TILE = 512
ROW_SPLIT = 2
OUT_TILE = 1024
MOE_TILE = 256
MOE_CHUNKS = 4
KV_CHUNK = 2560
GLOBAL_BQ = 256
LOCAL_NSUB = 4
H2_EXT = 256


def _cparams(*sem):
    return pltpu.CompilerParams(dimension_semantics=sem, vmem_limit_bytes=VMEM_LIMIT)


def _dot(a, b):
    return jnp.dot(a, b, preferred_element_type=F32)


def _dot_nt(a, b):
    return lax.dot_general(a, b, (((1,), (1,)), ((), ())), preferred_element_type=F32)


def _split(a):
    hi = a.astype(BF16)
    lo = (a - hi.astype(F32)).astype(BF16)
    return hi, lo


def _dot3(a, b):
    ah, al = _split(a)
    bh, bl = _split(b)
    return _dot(al, bh) + _dot(ah, bl) + _dot(ah, bh)


def _ln(x):
    mu = jnp.mean(x, axis=-1, keepdims=True)
    xc = x - mu
    var = jnp.mean(xc * xc, axis=-1, keepdims=True)
    return xc * lax.rsqrt(var + EPS)


def _silu(g):
    return g / (1.0 + jnp.exp(-g))


def _ada_kernel(c_ref, w_ref, b_ref, o_ref):
    o_ref[0] = _dot3(_silu(c_ref[...]), w_ref[0]) + b_ref[0]


def _ada(cs, w_ada, b_ada):
    depth, d, n = w_ada.shape
    rows = cs.shape[0]
    tn = 1536
    return pl.pallas_call(
        _ada_kernel,
        grid=(depth, n // tn),
        in_specs=[pl.BlockSpec((rows, d), lambda l, j: (0, 0)),
                  pl.BlockSpec((1, d, tn), lambda l, j: (l, 0, j)),
                  pl.BlockSpec((1, 1, tn), lambda l, j: (l, 0, j))],
        out_specs=pl.BlockSpec((1, rows, tn), lambda l, j: (l, 0, j)),
        out_shape=jax.ShapeDtypeStruct((depth, rows, n), F32),
        compiler_params=_cparams("parallel", "parallel"),
    )(cs, w_ada, b_ada.reshape(depth, 1, n))


def _head_rms(x, bd, gain):
    hi, lo = _split(x * x)
    ss = _dot(lo, bd) + _dot(hi, bd)
    return x * lax.rsqrt(ss * (1.0 / HEAD_DIM) + EPS) * gain


def _rms(x, gain):
    return x * lax.rsqrt(jnp.mean(x * x, axis=-1, keepdims=True) + EPS) * gain


def _rope(x, tab_ref, shift):
    outs = []
    for j in range(x.shape[1] // LANES):
        xb = x[:, j * LANES:(j + 1) * LANES]
        outs.append(xb * tab_ref[0]
                    + pltpu.roll(xb, shift, 1) * tab_ref[1]
                    + pltpu.roll(xb, LANES - shift, 1) * tab_ref[2])
    return outs[0] if len(outs) == 1 else jnp.concatenate(outs, axis=1)


def _value_tiles(v):
    lane = lax.broadcasted_iota(jnp.int32, (v.shape[0], LANES), 1)
    fill = jnp.where(lane == HEAD_DIM, 1.0, 0.0)
    outs = []
    for j in range(v.shape[1] // LANES):
        blk = v[:, j * LANES:(j + 1) * LANES]
        outs.append(jnp.where(lane < HEAD_DIM, blk, fill))
        outs.append(jnp.where(lane < HEAD_DIM, pltpu.roll(blk, HEAD_DIM, 1), fill))
    return jnp.concatenate(outs, axis=1).astype(BF16)


def _proj_kernel(rope, fin_alpha, n_aliased, *refs):
    refs = list(refs)
    if fin_alpha is None:
        x_ref = refs.pop(0)
    else:
        x1_ref, y_ref, g2_ref, lng_ref, lnb_ref = refs[:5]
        refs = refs[5:]
    (sc_ref, sh_ref, w_ref, gq_ref, gk_ref, gmq_ref, gmkv_ref,
     wqb_ref, wkvb_ref, bd_ref) = refs[:10]
    refs = refs[10:]
    tabs = None
    if rope:
        tabs = refs[:3]
        refs = refs[3:]
    refs = refs[n_aliased:]
    if fin_alpha is not None:
        xo_ref = refs.pop(0)
    refs = [r.at[0] if len(r.shape) == 3 else r for r in refs]
    group = refs[0].shape[0] // ROW_SPLIT
    for u in range(ROW_SPLIT):
        rows = slice(u * group, (u + 1) * group)
        if fin_alpha is None:
            x = x_ref[rows, :]
        else:
            x = (_ln(fin_alpha * x1_ref[rows, :] + g2_ref[0] * y_ref[rows, :]) * lng_ref[...]
                 + lnb_ref[...])
            xo_ref[rows, :] = x
        _proj_rows(rope, x, rows, sc_ref, sh_ref, w_ref, gq_ref, gk_ref, gmq_ref, gmkv_ref, wqb_ref,
                   wkvb_ref, bd_ref, tabs, refs)


def _proj_rows(rope, x, rows, sc_ref, sh_ref, w_ref, gq_ref, gk_ref, gmq_ref, gmkv_ref, wqb_ref, wkvb_ref,
               bd_ref, tabs, outs):
    qa, ka, va, qb, kb, vb, qc, kc, vc, qd, kd, vd = outs
    if rope:
        t64_ref, tq_ref, tk_ref = [t.at[:, rows, :] for t in tabs]
    h = _ln(x) * (1.0 + sc_ref[0]) + sh_ref[0]
    hb = h.astype(BF16)
    sq = HEAD_DIM ** -0.5 * LOG2E

    def proj(a, b):
        return _dot(hb, w_ref[:, a:b])

    pa = proj(0, 768)
    qa[rows, :] = (pa[:, 0:256] * sq).astype(BF16)
    ka[rows, :] = pa[:, 256:512].astype(BF16)
    va[rows, :] = _value_tiles(pa[:, 512:768])

    pb = proj(768, 1280)
    q = _head_rms(pb[:, 0:256], bd_ref[...], gq_ref[...])
    k = _head_rms(pb[:, 256:384], bd_ref[0:128, 0:128], gk_ref[...])
    if rope:
        q = _rope(q, t64_ref, 16)
        k = _rope(k, t64_ref, 16)
    qb[rows, :] = (q * sq).astype(BF16)
    kb[rows, :] = k.astype(BF16)
    vb[rows, :] = _value_tiles(pb[:, 384:512])

    pc = proj(1280, 1792)
    q = pc[:, 0:256]
    k = pc[:, 256:384]
    if rope:
        q = _rope(q, t64_ref, 16)
        k = _rope(k, t64_ref, 16)
    qc[rows, :] = (q * sq).astype(BF16)
    kc[rows, :] = k.astype(BF16)
    vc[rows, :] = _value_tiles(pc[:, 384:512])

    pd = proj(1792, 2304)
    cq = _rms(pd[:, 0:256], gmq_ref[...])
    q = _dot(cq.astype(BF16), wqb_ref[...])
    if rope:
        q = _rope(q, tq_ref, 8)
    qd[rows, :] = (q * (MLA_QK ** -0.5 * LOG2E)).astype(BF16)
    ckv = _rms(pd[:, 256:384], gmkv_ref[...])
    kvu = _dot(ckv.astype(BF16), wkvb_ref[...])
    kr = pd[:, 384:512]
    if rope:
        kr = _rope(kr, tk_ref, 8)
    kr = pltpu.roll(kr, MLA_NOPE, 1)
    kd[rows, :] = jnp.concatenate(
        [kvu[:, j * LANES:(j + 1) * LANES] + kr for j in range(4)], axis=1).astype(BF16)
    lane = lax.broadcasted_iota(jnp.int32, (kr.shape[0], LANES), 1)
    fill = jnp.where(lane == HEAD_DIM, 1.0, 0.0)
    vd[rows, :] = jnp.concatenate(
        [kvu[:, (4 + j) * LANES:(5 + j) * LANES] + fill for j in range(4)], axis=1).astype(BF16)


_PROJ_WIDTHS = (256, 256, 512, 256, 128, 256, 256, 128, 256, 512, 512, 512)


_Q_OUTS = (0, 3, 6, 9)
_KV_OUTS = (1, 2, 4, 5, 7, 8, 10, 11)


def _project(x2d, fin, sc, sh, lw, tabs, tokens_per_batch, kv_rows, kv_row0, kv_prev=None):
    n, d = (x2d if fin is None else fin[1]).shape
    tm = min(ROW_TILE, tokens_per_batch)
    per_b = tokens_per_batch // tm
    bsz = n // tokens_per_batch
    rope = tabs is not None
    row = lambda i: (i, 0)
    full = lambda i: (0, 0)
    perb = lambda i: (i // per_b, 0, 0)
    kvrow = lambda i: (i // per_b, kv_row0 // tm + i % per_b, 0)
    if fin is None:
        in_specs = [pl.BlockSpec((tm, d), row)]
        args = [x2d]
    else:
        in_specs = [pl.BlockSpec((tm, d), row), pl.BlockSpec((tm, d), row), pl.BlockSpec((1, 1, d), perb),
                    pl.BlockSpec((1, d), full), pl.BlockSpec((1, d), full)]
        args = list(fin[1:])
    in_specs += [pl.BlockSpec((1, 1, d), perb), pl.BlockSpec((1, 1, d), perb),
                 pl.BlockSpec(lw["w_in"].shape, full),
                 pl.BlockSpec((1, 256), full), pl.BlockSpec((1, 128), full),
                 pl.BlockSpec((1, 256), full), pl.BlockSpec((1, 128), full),
                 pl.BlockSpec(lw["w_qb"].shape, full), pl.BlockSpec(lw["w_kvb"].shape, full),
                 pl.BlockSpec((256, 256), full)]
    args += [sc, sh, lw["w_in"], lw["gq"], lw["gk"], lw["gmq"], lw["gmkv"],
             lw["w_qb"], lw["w_kvb"], lw["bd"]]
    if rope:
        tab = lambda i: (0, i % per_b, 0)
        in_specs += [pl.BlockSpec((3, tm, LANES), tab)] * 3
        args += list(tabs)
    out_specs = [pl.BlockSpec((1, tm, w), kvrow) if i in _KV_OUTS else pl.BlockSpec((tm, w), row)
                 for i, w in enumerate(_PROJ_WIDTHS)]
    out_shape = [jax.ShapeDtypeStruct((bsz, kv_rows, w) if i in _KV_OUTS else (n, w), BF16)
                 for i, w in enumerate(_PROJ_WIDTHS)]
    lead = 0 if fin is None else 1
    if fin is not None:
        out_specs.insert(0, pl.BlockSpec((tm, d), row))
        out_shape.insert(0, jax.ShapeDtypeStruct((n, d), F32))
    aliases = {}
    if kv_prev is not None:
        for a, i in zip(kv_prev, _KV_OUTS):
            aliases[len(args)] = lead + i
            in_specs.append(pl.BlockSpec(memory_space=pl.ANY))
            args.append(a)
    outs = pl.pallas_call(
        functools.partial(_proj_kernel, rope, None if fin is None else fin[0], len(aliases)),
        grid=(n // tm,),
        in_specs=in_specs,
        out_specs=out_specs,
        out_shape=out_shape,
        input_output_aliases=aliases,
        compiler_params=_cparams("parallel"),
    )(*args)
    return (x2d, list(outs)) if fin is None else (outs[0], list(outs[1:]))


def _softmax_step(s, vts, bq, m, acc):
    m_new = jnp.maximum(m, jnp.max(s, axis=1, keepdims=True))
    alpha = jnp.exp2(m - m_new)
    p = jnp.exp2(s - m_new).astype(BF16)
    if len(vts) == 1:
        pv = _dot(p, vts[0])
    else:
        pv = jnp.concatenate([_dot(p[:bq], vts[0]), _dot(p[bq:], vts[1])], axis=0)
    return [m_new, alpha * acc + pv]


def _attn_kernel(mode, bq, nsub, srcs, single_shot, has_bias, has_sink, *refs):
    refs = list(refs)
    q_ref = refs.pop(0)
    k_ref, v_ref = refs.pop(0), refs.pop(0)
    bias_refs = [refs.pop(0) for _ in range(nsub)] if has_bias else None
    sink_ref = refs.pop(0) if has_sink else None
    o_ref = refs.pop(0)

    for u in range(nsub):
        qi = pl.program_id(1) * nsub + u
        rows = slice(u * bq, (u + 1) * bq)
        rows2 = 2 * bq
        lo = lax.broadcasted_iota(jnp.int32, (bq, LANES), 1) < HEAD_DIM

        qss, kcs, vcs = [], [], []
        for j in range(2):
            if mode == "mla":
                qblk = q_ref[0, rows, 2 * LANES * j:2 * LANES * (j + 1)].astype(F32)
                first = lax.broadcasted_iota(jnp.int32, (bq, 2 * LANES), 1) < LANES
                qs = jnp.concatenate([jnp.where(first, qblk, 0.0), jnp.where(first, 0.0, qblk)], axis=0)
                kcs.append((2 * LANES * j, 2 * LANES * (j + 1)))
                vcs.append((2 * LANES * j, 2 * LANES * j + LANES))
            else:
                qblk = q_ref[0, rows, LANES * j:LANES * (j + 1)].astype(F32)
                if mode == "mha":
                    qe, qo = jnp.where(lo, qblk, 0.0), jnp.where(lo, 0.0, qblk)
                    kcs.append((LANES * j, LANES * (j + 1)))
                    vcs.append((2 * LANES * j, 2 * LANES * j + LANES))
                else:
                    rolled = pltpu.roll(qblk, HEAD_DIM, 1)
                    if j == 0:
                        qe, qo = jnp.where(lo, qblk, 0.0), jnp.where(lo, rolled, 0.0)
                    else:
                        qe, qo = jnp.where(lo, 0.0, rolled), jnp.where(lo, 0.0, qblk)
                    kcs.append((0, LANES))
                    vcs.append((LANES * j,))
                qs = jnp.concatenate([qe, qo], axis=0)
            qss.append(qs.astype(BF16))

        def pv(p, vts):
            if len(vts) == 1:
                return _dot(p, vts[0])
            return jnp.concatenate([_dot(p[:bq], vts[0]), _dot(p[bq:], vts[1])], axis=0)

        def tiles(j, off, n):
            kt = k_ref[0, pl.ds(off, n), kcs[j][0]:kcs[j][1]]
            return kt, [v_ref[0, pl.ds(off, n), v0:v0 + LANES] for v0 in vcs[j]]

        def block_scores(j, kind, row0, nrows):
            if kind == "full":
                kt, vts = tiles(j, row0, nrows)
                return _dot_nt(qss[j], kt), vts
            if kind == "win":
                span = bq + 2 * WINDOW
                start = pl.multiple_of(jnp.clip((qi - 1) * bq, 0, nrows - span), bq)
                r = lax.broadcasted_iota(jnp.int32, (rows2, span), 0)
                qpos = qi * bq + jnp.where(r >= bq, r - bq, r)
                kpos = start + lax.broadcasted_iota(jnp.int32, (rows2, span), 1)
                kt, vts = tiles(j, row0 + start, span)
                return jnp.where(jnp.abs(qpos - kpos) <= WINDOW, _dot_nt(qss[j], kt), NEG), vts
            span = NBR_SPAN_ROWS * GRID_W
            ks = jnp.clip(2 * qi - NA_ROWS // 2, 0, nrows // GRID_W - NBR_SPAN_ROWS)
            kt, vts = tiles(j, row0 + pl.multiple_of(ks * GRID_W, GRID_W), span)
            return _dot_nt(qss[j], kt) + bias_refs[u][0, j], vts

        if single_shot:
            state = []
            for j in range(2):
                parts = [block_scores(j, kind, row0, nrows) for kind, row0, nrows in srcs]
                m = functools.reduce(jnp.maximum, [jnp.max(sc, axis=1, keepdims=True) for sc, _ in parts])
                acc = functools.reduce(lambda a, b: a + b,
                                       [pv(jnp.exp2(sc - m).astype(BF16), vts) for sc, vts in parts])
                state += [m, acc]
        else:
            state = [jnp.full((rows2, 1), NEG, F32), jnp.zeros((rows2, LANES), F32)] * 2
            for kind, row0, nrows in srcs:
                for j in range(2):
                    sc, vts = block_scores(j, kind, row0, nrows)
                    state[2 * j:2 * j + 2] = _softmax_step(sc, vts, bq, state[2 * j], state[2 * j + 1])

        for j in range(2):
            m, acc = state[2 * j], state[2 * j + 1]
            l = acc[:, HEAD_DIM:HEAD_DIM + 1]
            if has_sink:
                sk = jnp.concatenate([jnp.full((bq, 1), sink_ref[2 * j] * LOG2E, F32),
                                      jnp.full((bq, 1), sink_ref[2 * j + 1] * LOG2E, F32)], axis=0)
                mf = jnp.maximum(m, sk)
                a = jnp.exp2(m - mf)
                l = l * a + jnp.exp2(sk - mf)
                acc = acc * a
            o = acc / l
            o_ref[0, rows, LANES * j:LANES * (j + 1)] = jnp.where(
                lo, o[:bq], pltpu.roll(o[bq:], HEAD_DIM, 1)).astype(BF16)


def _key_ranges(nrows):
    n = -(-nrows // KV_CHUNK)
    size = max(nrows // n // 256, 1) * 256
    starts = [i * size for i in range(n)]
    return [("full", r0, (nrows if i == n - 1 else r0 + size) - r0) for i, r0 in enumerate(starts)]


def _attention(mode, q, k, v, parts, kv_block=None, bias=None, sink=None, bq=BLOCK, nsub=1, single_shot=True):
    b, tq, wq = q.shape
    srcs = tuple(parts)
    rows, blk = (k.shape[1], 0) if kv_block is None else kv_block
    in_specs = [pl.BlockSpec((1, nsub * bq, wq), lambda bi, qi: (bi, qi, 0)),
                pl.BlockSpec((1, rows, k.shape[2]), lambda bi, qi: (bi, blk, 0)),
                pl.BlockSpec((1, rows, v.shape[2]), lambda bi, qi: (bi, blk, 0))]
    args = [q, k, v]
    if bias is not None:
        npair = tq // bq

        for u in range(nsub):
            def bias_idx(bi, qi, u=u):
                p = qi * nsub + u
                return (jnp.where(p < 2, p, jnp.where(p >= npair - 2, p - (npair - 2) + 3, 2)), 0, 0, 0)
            in_specs.append(pl.BlockSpec((1,) + bias.shape[1:], bias_idx))
            args.append(bias)
    if sink is not None:
        in_specs.append(pl.BlockSpec(memory_space=pltpu.SMEM))
        args.append(sink)
    return pl.pallas_call(
        functools.partial(_attn_kernel, mode, bq, nsub, srcs, single_shot, bias is not None, sink is not None),
        grid=(b, tq // (nsub * bq)),
        in_specs=in_specs,
        out_specs=pl.BlockSpec((1, nsub * bq, 256), lambda bi, qi: (bi, qi, 0)),
        out_shape=jax.ShapeDtypeStruct((b, tq, 256), BF16),
        compiler_params=_cparams("parallel", "arbitrary"),
    )(*args)


def _route(logits):
    lane = lax.broadcasted_iota(jnp.int32, logits.shape, 1).astype(F32)
    big = float(LANES)
    is_g = lane < N_GROUPS
    lg = jnp.where(is_g, logits, NEG)
    gmax = jnp.max(lg, axis=1, keepdims=True)
    grp = jnp.min(jnp.where(lg == gmax, lane, big), axis=1, keepdims=True)
    den = jnp.sum(jnp.where(is_g, jnp.exp(lg - gmax), 0.0), axis=1, keepdims=True)
    p_grp = 1.0 / den
    e0 = N_GROUPS + EXPERTS_PER_GROUP * grp
    in_grp = (lane >= e0) & (lane < e0 + EXPERTS_PER_GROUP)
    le = jnp.where(in_grp, logits, NEG)
    v1 = jnp.max(le, axis=1, keepdims=True)
    i1 = jnp.min(jnp.where(in_grp & (le == v1), lane, big), axis=1, keepdims=True)
    rest = in_grp & (lane != i1)
    le2 = jnp.where(rest, logits, NEG)
    v2 = jnp.max(le2, axis=1, keepdims=True)
    i2 = jnp.min(jnp.where(rest & (le2 == v2), lane, big), axis=1, keepdims=True)
    t = jnp.exp(v2 - v1)
    w1 = p_grp / (1.0 + t)
    w2 = p_grp * t / (1.0 + t)
    first_lower = i1 < i2
    e_lo = jnp.where(first_lower, i1, i2) - N_GROUPS
    e_hi = jnp.where(first_lower, i2, i1) - N_GROUPS
    w_lo = jnp.where(first_lower, w1, w2)
    w_hi = jnp.where(first_lower, w2, w1)
    a = e_lo - EXPERTS_PER_GROUP * grp
    b = e_hi - EXPERTS_PER_GROUP * grp
    bucket = grp * N_PAIRS + a * (2 * EXPERTS_PER_GROUP - 1 - a) * 0.5 + (b - a - 1.0)
    return jnp.where(lane == 0, e_lo, jnp.where(lane == 1, e_hi, jnp.where(lane == 2, w_lo,
                     jnp.where(lane == 3, w_hi, jnp.where(lane == 4, bucket, 0.0)))))


def _out_kernel(alpha, oa, ob, oc, od, x_ref, g1_ref, sc2_ref, sh2_ref, w_ref, lng_ref, lnb_ref,
                wrh_ref, wrl_ref, br_ref, x1_ref, h2_ref, ids_ref):
    mix = (_dot(oa[...], w_ref[0:256, :]) + _dot(ob[...], w_ref[256:512, :])
           + _dot(oc[...], w_ref[512:768, :]) + _dot(od[...], w_ref[768:1024, :]))
    x1 = _ln(alpha * x_ref[...] + g1_ref[0] * mix) * lng_ref[...] + lnb_ref[...]
    x1_ref[...] = x1
    h2 = _ln(x1) * (1.0 + sc2_ref[0]) + sh2_ref[0]
    hh, hl = _split(h2)
    logits = _dot(hl, wrh_ref[...]) + _dot(hh, wrl_ref[...]) + _dot(hh, wrh_ref[...]) + br_ref[...]
    rh, rl = _split(_route(logits))
    d = h2.shape[1]
    h2_ref[:, 0:d] = hh
    h2_ref[:, d:d + LANES] = rh
    h2_ref[:, d + LANES:d + 2 * LANES] = rl
    pick = ((lax.broadcasted_iota(jnp.int32, (8, LANES), 0) == 0)
            & (lax.broadcasted_iota(jnp.int32, (8, LANES), 1) == 4)).astype(BF16)
    ids_ref[...] = _dot_nt(pick, rh)


def _out_proj(alpha, outs, x2d, g1, sc2, sh2, lw, tokens_per_batch):
    n, d = x2d.shape
    tm = min(OUT_TILE, tokens_per_batch)
    per_b = tokens_per_batch // tm
    row = lambda i: (i, 0)
    full = lambda i: (0, 0)
    perb = lambda i: (i // per_b, 0, 0)
    return pl.pallas_call(
        functools.partial(_out_kernel, alpha),
        grid=(n // tm,),
        in_specs=[pl.BlockSpec((tm, 256), row)] * 4 + [
            pl.BlockSpec((tm, d), row),
            pl.BlockSpec((1, 1, d), perb), pl.BlockSpec((1, 1, d), perb), pl.BlockSpec((1, 1, d), perb),
            pl.BlockSpec((d, d), full), pl.BlockSpec((1, d), full), pl.BlockSpec((1, d), full),
            pl.BlockSpec((d, LANES), full), pl.BlockSpec((d, LANES), full), pl.BlockSpec((1, LANES), full)],
        out_specs=[pl.BlockSpec((tm, d), row), pl.BlockSpec((tm, d + H2_EXT), row),
                   pl.BlockSpec((8, tm), lambda i: (0, i))],
        out_shape=[jax.ShapeDtypeStruct((n, d), F32), jax.ShapeDtypeStruct((n, d + H2_EXT), BF16),
                   jax.ShapeDtypeStruct((8, n), F32)],
        compiler_params=_cparams("parallel"),
    )(*outs, x2d, g1, sc2, sh2, lw["w_out"], lw["ln1_g"], lw["ln1_b"], lw["wr_hi"], lw["wr_lo"], lw["br"])


def _moe_kernel(tile0, e0_ref, e1_ref, valid_ref, xs_ref, wg0, wg1, wu0, wu1, wd0, wd1, *rest):
    o_ref, wb = rest[-7], rest[-6:]
    t = pl.program_id(0) + tile0
    d = o_ref.shape[1]
    prev = jnp.maximum(t - 1, 0)
    changed = (pl.program_id(0) == 0) | (e0_ref[t] != e0_ref[prev]) | (e1_ref[t] != e1_ref[prev])

    @pl.when(changed)
    def _():
        for src, dst in zip((wg0, wg1, wu0, wu1, wd0, wd1), wb):
            dst[...] = src[0, 0].astype(BF16)

    @pl.when(valid_ref[t] == 1)
    def _():
        x = xs_ref[:, 0:d]
        rt = xs_ref[:, d:d + LANES].astype(F32) + xs_ref[:, d + LANES:d + 2 * LANES].astype(F32)

        def expert(wg, wu, wd):
            a = _silu(_dot(x, wg[...])) * _dot(x, wu[...])
            return _dot(a.astype(BF16), wd[...])

        y = rt[:, 2:3] * expert(wb[0], wb[2], wb[4]) + rt[:, 3:4] * expert(wb[1], wb[3], wb[5])
        o_ref[...] = y.astype(o_ref.dtype)

    @pl.when(valid_ref[t] == 0)
    def _():
        o_ref[...] = jnp.zeros_like(o_ref)


def _moe_experts(layer, tile0, e0, e1, valid, xs, wg, wu, wd, ys_prev):
    p, dx = xs.shape
    d = dx - H2_EXT
    tm = MOE_TILE
    de = wg.shape[3]
    w0 = lambda t, e0, e1, v: (layer, e0[t + tile0], 0, 0)
    w1 = lambda t, e0, e1, v: (layer, e1[t + tile0], 0, 0)
    in_specs = [pl.BlockSpec((tm, dx), lambda t, e0, e1, v: (t, 0)),
                pl.BlockSpec((1, 1, d, de), w0), pl.BlockSpec((1, 1, d, de), w1),
                pl.BlockSpec((1, 1, d, de), w0), pl.BlockSpec((1, 1, d, de), w1),
                pl.BlockSpec((1, 1, de, d), w0), pl.BlockSpec((1, 1, de, d), w1)]
    args = [e0, e1, valid, xs, wg, wg, wu, wu, wd, wd]
    aliases = {}
    if ys_prev is not None:
        aliases[len(args)] = 0
        in_specs.append(pl.BlockSpec(memory_space=pl.ANY))
        args.append(ys_prev)
    return pl.pallas_call(
        functools.partial(_moe_kernel, tile0),
        grid_spec=pltpu.PrefetchScalarGridSpec(
            num_scalar_prefetch=3,
            grid=(p // tm,),
            in_specs=in_specs,
            out_specs=pl.BlockSpec((tm, d), lambda t, e0, e1, v: (t + tile0, 0)),
            scratch_shapes=[pltpu.VMEM((d, de), BF16)] * 4 + [pltpu.VMEM((de, d), BF16)] * 2),
        out_shape=jax.ShapeDtypeStruct((e0.shape[0] * tm, d), BF16),
        input_output_aliases=aliases,
        compiler_params=_cparams("arbitrary"),
    )(*args)


_PAIR_TABLE = ((0, 1), (0, 2), (0, 3), (1, 2), (1, 3), (2, 3))


def _moe(layer, h2x, ids, wg, wu, wd, chunks):
    n = h2x.shape[0]
    tm = MOE_TILE
    n_tiles = -(-(n // tm + N_BUCKETS) // chunks) * chunks
    bucket = ids[0].astype(jnp.int32)
    tok = jnp.arange(n, dtype=jnp.int32)
    sorted_bucket, order = lax.sort((bucket, tok), num_keys=1)
    _, rank = lax.sort((order, tok), num_keys=1)

    bids = jnp.arange(N_BUCKETS + 1, dtype=jnp.int32)
    edges = jnp.sum(sorted_bucket[None, :] < bids[:, None], axis=1, dtype=jnp.int32)
    counts = edges[1:] - edges[:-1]
    src_off = edges[:-1]
    tiles_per = (counts + tm - 1) // tm
    tile_end = jnp.cumsum(tiles_per)
    tile_start = tile_end - tiles_per

    tile_ids = jnp.arange(n_tiles, dtype=jnp.int32)
    valid = (tile_ids < tile_end[-1]).astype(jnp.int32)
    tile_bucket = jnp.sum(tile_ids[:, None] >= tile_end[None, :], axis=1, dtype=jnp.int32)
    last_bucket = jnp.sum(tile_end[-1] - 1 >= tile_end, dtype=jnp.int32)
    tile_bucket = jnp.minimum(jnp.where(valid == 1, tile_bucket, last_bucket), N_BUCKETS - 1)
    onehot_tb = tile_bucket[:, None] == bids[None, :N_BUCKETS]
    pick = lambda table: jnp.sum(jnp.where(onehot_tb, table[None, :], 0), axis=1)
    pairs = jnp.asarray(_PAIR_TABLE, dtype=jnp.int32)
    tg = tile_bucket // N_PAIRS
    e0 = tg * EXPERTS_PER_GROUP + pairs[tile_bucket % N_PAIRS, 0]
    e1 = tg * EXPERTS_PER_GROUP + pairs[tile_bucket % N_PAIRS, 1]

    first_row = (tile_ids - pick(tile_start)) * tm
    within = first_row[:, None] + jnp.arange(tm, dtype=jnp.int32)[None, :]
    row_ok = (within < pick(counts)[:, None]) & (valid[:, None] == 1)
    idx = jnp.clip(pick(src_off)[:, None] + within, 0, n - 1)
    src = jnp.where(row_ok, order.at[idx.reshape(-1)].get(mode="promise_in_bounds").reshape(idx.shape), 0).reshape(-1)

    per_chunk = n_tiles // chunks
    ys = None
    for c in range(chunks):
        xs = h2x.at[src[c * per_chunk * tm:(c + 1) * per_chunk * tm]].get(mode="promise_in_bounds")
        ys = _moe_experts(layer, c * per_chunk, e0, e1, valid, xs, wg, wu, wd, ys)

    onehot_b = bucket[None, :] == bids[:N_BUCKETS, None]
    base = jnp.sum(jnp.where(onehot_b, (tile_start * tm - src_off)[:, None], 0), axis=0)
    return ys.at[base + rank].get(mode="promise_in_bounds")


def _fin_kernel(alpha, x_ref, y_ref, g2_ref, lng_ref, lnb_ref, o_ref):
    o_ref[...] = _ln(alpha * x_ref[...] + g2_ref[0] * y_ref[...]) * lng_ref[...] + lnb_ref[...]


def _finish(alpha, x1, y, g2, ln_g, ln_b, tokens_per_batch):
    n, d = x1.shape
    tm = min(ROW_TILE, tokens_per_batch)
    per_b = tokens_per_batch // tm
    row = lambda i: (i, 0)
    full = lambda i: (0, 0)
    return pl.pallas_call(
        functools.partial(_fin_kernel, alpha),
        grid=(n // tm,),
        in_specs=[pl.BlockSpec((tm, d), row), pl.BlockSpec((tm, d), row),
                  pl.BlockSpec((1, 1, d), lambda i: (i // per_b, 0, 0)),
                  pl.BlockSpec((1, d), full), pl.BlockSpec((1, d), full)],
        out_specs=pl.BlockSpec((tm, d), row),
        out_shape=jax.ShapeDtypeStruct((n, d), F32),
        compiler_params=_cparams("parallel"),
    )(x1, y, g2, ln_g, ln_b)


def _rope_tables(t):
    pos = jnp.arange(t)
    row = (pos // GRID_W).astype(F32)[:, None]
    col = (pos % GRID_W).astype(F32)[:, None]
    lane = jnp.arange(LANES)

    def build(d, half, active):
        axis_col = (d // half) % 2 == 1
        w = d % half
        first = w < half // 2
        f = (w % (half // 2)).astype(F32)
        freq = ROPE_THETA ** (-(2.0 * f) / half)
        ang = jnp.where(axis_col[None, :], col, row) * freq[None, :]
        cos = jnp.where(active[None, :], jnp.cos(ang), 1.0)
        sin = jnp.where(active[None, :], jnp.sin(ang), 0.0)
        return jnp.stack([cos, jnp.where(first[None, :], 0.0, sin), jnp.where(first[None, :], -sin, 0.0)])

    t64 = build(lane % HEAD_DIM, HEAD_DIM // 2, lane >= 0)
    in_rope = (lane >= MLA_NOPE) & (lane < MLA_QK)
    tq = build(jnp.where(in_rope, lane - MLA_NOPE, 0), MLA_ROPE // 2, in_rope)
    in_kr = lane < MLA_ROPE
    tk = build(jnp.where(in_kr, lane, 0), MLA_ROPE // 2, in_kr)
    return t64, tq, tk


def _nbr_bias_tables(na_bias, rows):
    heads = na_bias.shape[0]
    npair = rows // 2
    kr = min(NA_ROWS, rows)
    qcol = jnp.arange(GRID_W)[:, None]
    kcol = jnp.arange(GRID_W)[None, :]
    c0 = jnp.clip(qcol - NA_COLS // 2, 0, GRID_W - NA_COLS)
    col_ok = (kcol >= c0) & (kcol < c0 + NA_COLS)
    col_off = kcol - qcol + NA_COLS - 1
    onehot = (col_off[None] == jnp.arange(2 * NA_COLS - 1)[:, None, None]).astype(F32)
    colmat = jnp.einsum("hdc,cqk->hdqk", na_bias, onehot, precision=lax.Precision.HIGHEST) * LOG2E
    colmat = jnp.where(col_ok[None, None], colmat, NEG)
    outside = jnp.full((heads, GRID_W, GRID_W), NEG, F32)
    tabs = []
    for p in (0, 1, 2, npair - 2, npair - 1):
        ks = min(max(2 * p - NA_ROWS // 2, 0), rows - NBR_SPAN_ROWS)
        per_qrow = []
        for qr in range(2):
            r = 2 * p + qr
            r0 = min(max(r - kr // 2, 0), rows - kr)
            blocks = []
            for kri in range(NBR_SPAN_ROWS):
                krow = ks + kri
                blocks.append(colmat[:, krow - r + NA_ROWS - 1] if r0 <= krow < r0 + kr else outside)
            per_qrow.append(jnp.concatenate(blocks, axis=-1))
        tab = jnp.concatenate(per_qrow, axis=1)
        tabs.append(tab.reshape(heads // 2, 4 * GRID_W, NBR_SPAN_ROWS * GRID_W))
    return jnp.stack(tabs)


def _layer_weights(l, w_in, gqa_q_gain, gqa_k_gain, mla_q_gain, mla_w_qb, mla_kv_gain, mla_w_kvb,
                   w_out, ln1_g, ln1_b, router_group_w, router_group_b, router_expert_w, router_expert_b):
    d = w_in.shape[1]
    d_proj = w_in.shape[2]
    lw = {}
    lw["w_in"] = jnp.pad(w_in[l], ((0, 0), (0, 2304 - d_proj))).astype(BF16)
    lw["gq"] = jnp.tile(gqa_q_gain[l], 4)[None, :]
    lw["gk"] = jnp.tile(gqa_k_gain[l], 2)[None, :]
    lw["gmq"] = mla_q_gain[l][None, :]
    lw["gmkv"] = mla_kv_gain[l][None, :]
    wqb = mla_w_qb[l].reshape(-1, 4, MLA_QK)
    lw["w_qb"] = jnp.pad(wqb, ((0, 0), (0, 0), (0, LANES - MLA_QK))).reshape(-1, 4 * LANES).astype(BF16)
    wkvb = mla_w_kvb[l].reshape(-1, 4, 2 * HEAD_DIM)
    k_part = jnp.pad(wkvb[:, :, :MLA_NOPE], ((0, 0), (0, 0), (0, LANES - MLA_NOPE))).reshape(-1, 4 * LANES)
    v_part = jnp.pad(wkvb[:, :, MLA_NOPE:], ((0, 0), (0, 0), (0, LANES - HEAD_DIM))).reshape(-1, 4 * LANES)
    lw["w_kvb"] = jnp.concatenate([k_part, v_part], axis=1).astype(BF16)
    idx = jnp.arange(256) // HEAD_DIM
    lw["bd"] = (idx[:, None] == idx[None, :]).astype(BF16)
    lw["w_out"] = w_out[l].astype(BF16)
    lw["ln1_g"] = ln1_g[l][None, :]
    lw["ln1_b"] = ln1_b[l][None, :]
    wr = jnp.concatenate([router_group_w[l], router_expert_w[l]], axis=1)
    wr = jnp.pad(wr, ((0, 0), (0, LANES - wr.shape[1])))
    lw["wr_hi"] = wr.astype(BF16)
    lw["wr_lo"] = (wr - lw["wr_hi"].astype(F32)).astype(BF16)
    br = jnp.concatenate([router_group_b[l], router_expert_b[l]])
    lw["br"] = jnp.pad(br, (0, LANES - br.shape[0]))[None, :]
    return lw


def kernel(x, c, ctx, c_ctx, w_ada, b_ada, w_in, na_bias, gqa_q_gain, gqa_k_gain, win_sink, mla_q_gain, mla_w_qb, mla_kv_gain, mla_w_kvb, w_out, ln1_g, ln1_b, router_group_w, router_group_b, router_expert_w, router_expert_b, moe_w_gate, moe_w_up, moe_w_down, ln2_g, ln2_b):
    bsz, t, d = x.shape
    tc = ctx.shape[1]
    depth = w_ada.shape[0]
    rows = t // GRID_W
    assert t % (2 * GRID_W) == 0 and rows >= NBR_SPAN_ROWS and rows // 2 >= 5
    assert t >= BLOCK + 2 * WINDOW and t % ROW_TILE == 0 and tc % (8 * ROW_SPLIT) == 0 and t % tc == 0
    assert tc <= ROW_TILE and t % OUT_TILE == 0
    assert t % (LOCAL_NSUB * BLOCK) == 0 and t % GLOBAL_BQ == 0
    s_tot = t + tc
    alpha = (2.0 * depth) ** 0.25

    ada_rows = -(-(bsz + 1) // 8) * 8
    cs = jnp.zeros((ada_rows, d), F32).at[:bsz].set(c).at[bsz].set(c_ctx)
    mod = _ada(cs, w_ada, b_ada)
    tabs = _rope_tables(t)

    xl = x.reshape(bsz * t, d)
    xc = ctx.reshape(bsz * tc, d)
    fin_l = fin_c = None

    for l in range(depth):
        with_ctx = l < depth - 1
        lw = _layer_weights(l, w_in, gqa_q_gain, gqa_k_gain, mla_q_gain, mla_w_qb, mla_kv_gain,
                            mla_w_kvb, w_out, ln1_g, ln1_b, router_group_w, router_group_b,
                            router_expert_w, router_expert_b)
        ln2g, ln2b = ln2_g[l][None, :], ln2_b[l][None, :]

        def mods(lo, hi):
            parts = jnp.split(mod[l, lo:hi], 6, axis=-1)
            return [jnp.broadcast_to(p[:, None, :], (bsz, 1, d)) for p in parts]
        sh1, sc1, g1, sh2, sc2, g2 = mods(0, bsz)
        sh1c, sc1c, g1c, sh2c, sc2c, g2c = mods(bsz, bsz + 1)

        xl, pl_lat = _project(xl, fin_l, sc1, sh1, lw, tabs, t, s_tot, 0)
        xc, pl_ctx = _project(xc, fin_c, sc1c, sh1c, lw, None, tc, s_tot, t,
                              kv_prev=[pl_lat[i] for i in _KV_OUTS])
        qa, qb, qc, qd = [pl_lat[i].reshape(bsz, t, -1) for i in _Q_OUTS]
        qa_c, qb_c, qc_c, qd_c = [pl_ctx[i].reshape(bsz, tc, -1) for i in _Q_OUTS]
        ka, va, kb, vb, kc, vc, kd, vd = [pl_ctx[i] for i in _KV_OUTS]

        bias_tab = _nbr_bias_tables(na_bias[l], rows)
        everything = dict(parts=_key_ranges(s_tot), bq=GLOBAL_BQ, single_shot=False)
        out_a = _attention("mha", qa, ka, va, [("nbr", 0, t), ("full", t, tc)], bias=bias_tab, nsub=LOCAL_NSUB)
        out_b = _attention("gqa", qb, kb, vb, **everything)
        out_c = _attention("gqa", qc, kc, vc, [("win", 0, t), ("full", t, tc)], sink=win_sink[l],
                           nsub=LOCAL_NSUB)
        out_d = _attention("mla", qd, kd, vd, **everything)
        outs = [o.reshape(bsz * t, 256) for o in (out_a, out_b, out_c, out_d)]

        x1, h2x, ids = _out_proj(alpha, outs, xl, g1, sc2, sh2, lw, t)
        fin_l = (alpha, x1, _moe(l, h2x, ids, moe_w_gate, moe_w_up, moe_w_down, MOE_CHUNKS), g2, ln2g, ln2b)

        if with_ctx:
            ctx_only = dict(parts=[("full", 0, tc)], kv_block=(tc, t // tc))
            outs_c = [
                _attention("mha", qa_c, ka, va, **ctx_only),
                _attention("gqa", qb_c, kb, vb, **ctx_only),
                _attention("gqa", qc_c, kc, vc, sink=win_sink[l], **ctx_only),
                _attention("mla", qd_c, kd, vd, **ctx_only),
            ]
            outs_c = [o.reshape(bsz * tc, 256) for o in outs_c]
            x1c, h2xc, ids_c = _out_proj(alpha, outs_c, xc, g1c, sc2c, sh2c, lw, tc)
            fin_c = (alpha, x1c, _moe(l, h2xc, ids_c, moe_w_gate, moe_w_up, moe_w_down, 1), g2c, ln2g, ln2b)

    return _finish(*fin_l, t).reshape(bsz, t, d)
```

```python
import functools

import jax
import jax.numpy as jnp
from jax import lax
from jax.experimental import pallas as pl
from jax.experimental.pallas import tpu as pltpu

F32 = jnp.float32
BF16 = jnp.bfloat16

GRID_W = 64
HEAD_DIM = 64
BLOCK = 128
WINDOW = 128
ROPE_THETA = 10000.0
NEG = -1e30
EPS = 1e-6
LOG2E = 1.4426950408889634

NA_ROWS = 8
NA_COLS = 16
NBR_SPAN_ROWS = 10
MLA_NOPE = 64
MLA_ROPE = 32
MLA_QK = MLA_NOPE + MLA_ROPE
N_GROUPS = 4
EXPERTS_PER_GROUP = 4
N_EXPERTS = 16
N_PAIRS = 6
N_BUCKETS = N_GROUPS * N_PAIRS

LANES = 128
VMEM_LIMIT = 48 * 1024 * 1024

ROW_TILE = 512
ROW_SPLIT = 2
OUT_TILE = 1024
MOE_TILE = 256
MOE_CHUNKS = 4
KV_CHUNK = 2560
GLOBAL_BQ = 256
LOCAL_NSUB = 4
H2_EXT = 256


def _cparams(*sem):
    return pltpu.CompilerParams(dimension_semantics=sem, vmem_limit_bytes=VMEM_LIMIT)


def _dot(a, b):
    return jnp.dot(a, b, preferred_element_type=F32)


def _dot_nt(a, b):
    return lax.dot_general(a, b, (((1,), (1,)), ((), ())), preferred_element_type=F32)


def _split(a):
    hi = a.astype(BF16)
    lo = (a - hi.astype(F32)).astype(BF16)
    return hi, lo


def _dot3(a, b):
    ah, al = _split(a)
    bh, bl = _split(b)
    return _dot(al, bh) + _dot(ah, bl) + _dot(ah, bh)


def _ln(x):
    mu = jnp.mean(x, axis=-1, keepdims=True)
    xc = x - mu
    var = jnp.mean(xc * xc, axis=-1, keepdims=True)
    return xc * lax.rsqrt(var + EPS)


def _silu(g):
    return g / (1.0 + jnp.exp(-g))


def _ada_kernel(c_ref, w_ref, b_ref, o_ref):
    o_ref[0] = _dot3(_silu(c_ref[...]), w_ref[0]) + b_ref[0]


def _ada(cs, w_ada, b_ada):
    depth, d, n = w_ada.shape
    rows = cs.shape[0]
    tn = 1536
    return pl.pallas_call(
        _ada_kernel,
        grid=(depth, n // tn),
        in_specs=[pl.BlockSpec((rows, d), lambda l, j: (0, 0)),
                  pl.BlockSpec((1, d, tn), lambda l, j: (l, 0, j)),
                  pl.BlockSpec((1, 1, tn), lambda l, j: (l, 0, j))],
        out_specs=pl.BlockSpec((1, rows, tn), lambda l, j: (l, 0, j)),
        out_shape=jax.ShapeDtypeStruct((depth, rows, n), F32),
        compiler_params=_cparams("parallel", "parallel"),
    )(cs, w_ada, b_ada.reshape(depth, 1, n))


def _head_rms(x, bd, gain):
    hi, lo = _split(x * x)
    ss = _dot(lo, bd) + _dot(hi, bd)
    return x * lax.rsqrt(ss * (1.0 / HEAD_DIM) + EPS) * gain


def _rms(x, gain):
    return x * lax.rsqrt(jnp.mean(x * x, axis=-1, keepdims=True) + EPS) * gain


def _rope(x, tab_ref, shift):
    outs = []
    for j in range(x.shape[1] // LANES):
        xb = x[:, j * LANES:(j + 1) * LANES]
        outs.append(xb * tab_ref[0]
                    + pltpu.roll(xb, shift, 1) * tab_ref[1]
                    + pltpu.roll(xb, LANES - shift, 1) * tab_ref[2])
    return outs[0] if len(outs) == 1 else jnp.concatenate(outs, axis=1)


def _value_tiles(v):
    lane = lax.broadcasted_iota(jnp.int32, (v.shape[0], LANES), 1)
    fill = jnp.where(lane == HEAD_DIM, 1.0, 0.0)
    outs = []
    for j in range(v.shape[1] // LANES):
        blk = v[:, j * LANES:(j + 1) * LANES]
        outs.append(jnp.where(lane < HEAD_DIM, blk, fill))
        outs.append(jnp.where(lane < HEAD_DIM, pltpu.roll(blk, HEAD_DIM, 1), fill))
    return jnp.concatenate(outs, axis=1).astype(BF16)


def _proj_kernel(rope, fin_alpha, n_aliased, *refs):
    refs = list(refs)
    if fin_alpha is None:
        x_ref = refs.pop(0)
    else:
        x1_ref, y_ref, g2_ref, lng_ref, lnb_ref = refs[:5]
        refs = refs[5:]
    (sc_ref, sh_ref, w_ref, gq_ref, gk_ref, gmq_ref, gmkv_ref,
     wqb_ref, wkvb_ref, bd_ref) = refs[:10]
    refs = refs[10:]
    tabs = None
    if rope:
        tabs = refs[:3]
        refs = refs[3:]
    refs = refs[n_aliased:]
    if fin_alpha is not None:
        xo_ref = refs.pop(0)
    refs = [r.at[0] if len(r.shape) == 3 else r for r in refs]
    group = refs[0].shape[0] // ROW_SPLIT
    for u in range(ROW_SPLIT):
        rows = slice(u * group, (u + 1) * group)
        if fin_alpha is None:
            x = x_ref[rows, :]
        else:
            x = (_ln(fin_alpha * x1_ref[rows, :] + g2_ref[0] * y_ref[rows, :]) * lng_ref[...]
                 + lnb_ref[...])
            xo_ref[rows, :] = x
        _proj_rows(rope, x, rows, sc_ref, sh_ref, w_ref, gq_ref, gk_ref, gmq_ref, gmkv_ref, wqb_ref,
                   wkvb_ref, bd_ref, tabs, refs)


def _proj_rows(rope, x, rows, sc_ref, sh_ref, w_ref, gq_ref, gk_ref, gmq_ref, gmkv_ref, wqb_ref, wkvb_ref,
               bd_ref, tabs, outs):
    qa, ka, va, qb, kb, vb, qc, kc, vc, qd, kd, vd = outs
    if rope:
        t64_ref, tq_ref, tk_ref = [t.at[:, rows, :] for t in tabs]
    h = _ln(x) * (1.0 + sc_ref[0]) + sh_ref[0]
    hb = h.astype(BF16)
    sq = HEAD_DIM ** -0.5 * LOG2E

    def proj(a, b):
        return _dot(hb, w_ref[:, a:b])

    pa = proj(0, 768)
    qa[rows, :] = (pa[:, 0:256] * sq).astype(BF16)
    ka[rows, :] = pa[:, 256:512].astype(BF16)
    va[rows, :] = _value_tiles(pa[:, 512:768])

    pb = proj(768, 1280)
    q = _head_rms(pb[:, 0:256], bd_ref[...], gq_ref[...])
    k = _head_rms(pb[:, 256:384], bd_ref[0:128, 0:128], gk_ref[...])
    if rope:
        q = _rope(q, t64_ref, 16)
        k = _rope(k, t64_ref, 16)
    qb[rows, :] = (q * sq).astype(BF16)
    kb[rows, :] = k.astype(BF16)
    vb[rows, :] = _value_tiles(pb[:, 384:512])

    pc = proj(1280, 1792)
    q = pc[:, 0:256]
    k = pc[:, 256:384]
    if rope:
        q = _rope(q, t64_ref, 16)
        k = _rope(k, t64_ref, 16)
    qc[rows, :] = (q * sq).astype(BF16)
    kc[rows, :] = k.astype(BF16)
    vc[rows, :] = _value_tiles(pc[:, 384:512])

    pd = proj(1792, 2304)
    cq = _rms(pd[:, 0:256], gmq_ref[...])
    q = _dot(cq.astype(BF16), wqb_ref[...])
    if rope:
        q = _rope(q, tq_ref, 8)
    qd[rows, :] = (q * (MLA_QK ** -0.5 * LOG2E)).astype(BF16)
    ckv = _rms(pd[:, 256:384], gmkv_ref[...])
    kvu = _dot(ckv.astype(BF16), wkvb_ref[...])
    kr = pd[:, 384:512]
    if rope:
        kr = _rope(kr, tk_ref, 8)
    kr = pltpu.roll(kr, MLA_NOPE, 1)
    kd[rows, :] = jnp.concatenate(
        [kvu[:, j * LANES:(j + 1) * LANES] + kr for j in range(4)], axis=1).astype(BF16)
    lane = lax.broadcasted_iota(jnp.int32, (kr.shape[0], LANES), 1)
    fill = jnp.where(lane == HEAD_DIM, 1.0, 0.0)
    vd[rows, :] = jnp.concatenate(
        [kvu[:, (4 + j) * LANES:(5 + j) * LANES] + fill for j in range(4)], axis=1).astype(BF16)


_PROJ_WIDTHS = (256, 256, 512, 256, 128, 256, 256, 128, 256, 512, 512, 512)


_Q_OUTS = (0, 3, 6, 9)
_KV_OUTS = (1, 2, 4, 5, 7, 8, 10, 11)


def _project(x2d, fin, sc, sh, lw, tabs, tokens_per_batch, kv_rows, kv_row0, kv_prev=None):
    n, d = (x2d if fin is None else fin[1]).shape
    tm = min(ROW_TILE, tokens_per_batch)
    per_b = tokens_per_batch // tm
    bsz = n // tokens_per_batch
    rope = tabs is not None
    row = lambda i: (i, 0)
    full = lambda i: (0, 0)
    perb = lambda i: (i // per_b, 0, 0)
    kvrow = lambda i: (i // per_b, kv_row0 // tm + i % per_b, 0)
    if fin is None:
        in_specs = [pl.BlockSpec((tm, d), row)]
        args = [x2d]
    else:
        in_specs = [pl.BlockSpec((tm, d), row), pl.BlockSpec((tm, d), lambda i: (i + fin[6] // tm, 0)),
                    pl.BlockSpec((1, 1, d), perb), pl.BlockSpec((1, d), full), pl.BlockSpec((1, d), full)]
        args = list(fin[1:6])
    in_specs += [pl.BlockSpec((1, 1, d), perb), pl.BlockSpec((1, 1, d), perb),
                 pl.BlockSpec(lw["w_in"].shape, full),
                 pl.BlockSpec((1, 256), full), pl.BlockSpec((1, 128), full),
                 pl.BlockSpec((1, 256), full), pl.BlockSpec((1, 128), full),
                 pl.BlockSpec(lw["w_qb"].shape, full), pl.BlockSpec(lw["w_kvb"].shape, full),
                 pl.BlockSpec((256, 256), full)]
    args += [sc, sh, lw["w_in"], lw["gq"], lw["gk"], lw["gmq"], lw["gmkv"],
             lw["w_qb"], lw["w_kvb"], lw["bd"]]
    if rope:
        tab = lambda i: (0, i % per_b, 0)
        in_specs += [pl.BlockSpec((3, tm, LANES), tab)] * 3
        args += list(tabs)
    out_specs = [pl.BlockSpec((1, tm, w), kvrow) if i in _KV_OUTS else pl.BlockSpec((tm, w), row)
                 for i, w in enumerate(_PROJ_WIDTHS)]
    out_shape = [jax.ShapeDtypeStruct((bsz, kv_rows, w) if i in _KV_OUTS else (n, w), BF16)
                 for i, w in enumerate(_PROJ_WIDTHS)]
    lead = 0 if fin is None else 1
    if fin is not None:
        out_specs.insert(0, pl.BlockSpec((tm, d), row))
        out_shape.insert(0, jax.ShapeDtypeStruct((n, d), F32))
    aliases = {}
    if kv_prev is not None:
        for a, i in zip(kv_prev, _KV_OUTS):
            aliases[len(args)] = lead + i
            in_specs.append(pl.BlockSpec(memory_space=pl.ANY))
            args.append(a)
    outs = pl.pallas_call(
        functools.partial(_proj_kernel, rope, None if fin is None else fin[0], len(aliases)),
        grid=(n // tm,),
        in_specs=in_specs,
        out_specs=out_specs,
        out_shape=out_shape,
        input_output_aliases=aliases,
        compiler_params=_cparams("parallel"),
    )(*args)
    return (x2d, list(outs)) if fin is None else (outs[0], list(outs[1:]))


def _softmax_step(s, vts, bq, m, acc):
    m_new = jnp.maximum(m, jnp.max(s, axis=1, keepdims=True))
    alpha = jnp.exp2(m - m_new)
    p = jnp.exp2(s - m_new).astype(BF16)
    if len(vts) == 1:
        pv = _dot(p, vts[0])
    else:
        pv = jnp.concatenate([_dot(p[:bq], vts[0]), _dot(p[bq:], vts[1])], axis=0)
    return [m_new, alpha * acc + pv]


def _attn_kernel(mode, bq, nsub, srcs, single_shot, has_bias, has_sink, *refs):
    refs = list(refs)
    q_ref = refs.pop(0)
    k_ref, v_ref = refs.pop(0), refs.pop(0)
    bias_refs = [refs.pop(0) for _ in range(nsub)] if has_bias else None
    sink_ref = refs.pop(0) if has_sink else None
    o_ref = refs.pop(0)

    for u in range(nsub):
        qi = pl.program_id(1) * nsub + u
        rows = slice(u * bq, (u + 1) * bq)
        rows2 = 2 * bq
        lo = lax.broadcasted_iota(jnp.int32, (bq, LANES), 1) < HEAD_DIM

        qss, kcs, vcs = [], [], []
        for j in range(2):
            if mode == "mla":
                qblk = q_ref[0, rows, 2 * LANES * j:2 * LANES * (j + 1)].astype(F32)
                first = lax.broadcasted_iota(jnp.int32, (bq, 2 * LANES), 1) < LANES
                qs = jnp.concatenate([jnp.where(first, qblk, 0.0), jnp.where(first, 0.0, qblk)], axis=0)
                kcs.append((2 * LANES * j, 2 * LANES * (j + 1)))
                vcs.append((2 * LANES * j, 2 * LANES * j + LANES))
            else:
                qblk = q_ref[0, rows, LANES * j:LANES * (j + 1)].astype(F32)
                if mode == "mha":
                    qe, qo = jnp.where(lo, qblk, 0.0), jnp.where(lo, 0.0, qblk)
                    kcs.append((LANES * j, LANES * (j + 1)))
                    vcs.append((2 * LANES * j, 2 * LANES * j + LANES))
                else:
                    rolled = pltpu.roll(qblk, HEAD_DIM, 1)
                    if j == 0:
                        qe, qo = jnp.where(lo, qblk, 0.0), jnp.where(lo, rolled, 0.0)
                    else:
                        qe, qo = jnp.where(lo, 0.0, rolled), jnp.where(lo, 0.0, qblk)
                    kcs.append((0, LANES))
                    vcs.append((LANES * j,))
                qs = jnp.concatenate([qe, qo], axis=0)
            qss.append(qs.astype(BF16))

        def pv(p, vts):
            if len(vts) == 1:
                return _dot(p, vts[0])
            return jnp.concatenate([_dot(p[:bq], vts[0]), _dot(p[bq:], vts[1])], axis=0)

        def tiles(j, off, n):
            kt = k_ref[0, pl.ds(off, n), kcs[j][0]:kcs[j][1]]
            return kt, [v_ref[0, pl.ds(off, n), v0:v0 + LANES] for v0 in vcs[j]]

        def block_scores(j, kind, row0, nrows):
            if kind == "full":
                kt, vts = tiles(j, row0, nrows)
                return _dot_nt(qss[j], kt), vts
            if kind == "win":
                span = bq + 2 * WINDOW
                start = pl.multiple_of(jnp.clip((qi - 1) * bq, 0, nrows - span), bq)
                r = lax.broadcasted_iota(jnp.int32, (rows2, span), 0)
                qpos = qi * bq + jnp.where(r >= bq, r - bq, r)
                kpos = start + lax.broadcasted_iota(jnp.int32, (rows2, span), 1)
                kt, vts = tiles(j, row0 + start, span)
                return jnp.where(jnp.abs(qpos - kpos) <= WINDOW, _dot_nt(qss[j], kt), NEG), vts
            span = NBR_SPAN_ROWS * GRID_W
            ks = jnp.clip(2 * qi - NA_ROWS // 2, 0, nrows // GRID_W - NBR_SPAN_ROWS)
            kt, vts = tiles(j, row0 + pl.multiple_of(ks * GRID_W, GRID_W), span)
            return _dot_nt(qss[j], kt) + bias_refs[u][0, j], vts

        if single_shot:
            state = []
            for j in range(2):
                parts = [block_scores(j, kind, row0, nrows) for kind, row0, nrows in srcs]
                m = functools.reduce(jnp.maximum, [jnp.max(sc, axis=1, keepdims=True) for sc, _ in parts])
                acc = functools.reduce(lambda a, b: a + b,
                                       [pv(jnp.exp2(sc - m).astype(BF16), vts) for sc, vts in parts])
                state += [m, acc]
        else:
            state = [jnp.full((rows2, 1), NEG, F32), jnp.zeros((rows2, LANES), F32)] * 2
            for kind, row0, nrows in srcs:
                for j in range(2):
                    sc, vts = block_scores(j, kind, row0, nrows)
                    state[2 * j:2 * j + 2] = _softmax_step(sc, vts, bq, state[2 * j], state[2 * j + 1])

        for j in range(2):
            m, acc = state[2 * j], state[2 * j + 1]
            l = acc[:, HEAD_DIM:HEAD_DIM + 1]
            if has_sink:
                sk = jnp.concatenate([jnp.full((bq, 1), sink_ref[2 * j] * LOG2E, F32),
                                      jnp.full((bq, 1), sink_ref[2 * j + 1] * LOG2E, F32)], axis=0)
                mf = jnp.maximum(m, sk)
                a = jnp.exp2(m - mf)
                l = l * a + jnp.exp2(sk - mf)
                acc = acc * a
            o = acc / l
            o_ref[0, rows, LANES * j:LANES * (j + 1)] = jnp.where(
                lo, o[:bq], pltpu.roll(o[bq:], HEAD_DIM, 1)).astype(BF16)


def _key_ranges(nrows):
    n = -(-nrows // KV_CHUNK)
    size = max(nrows // n // 256, 1) * 256
    starts = [i * size for i in range(n)]
    return [("full", r0, (nrows if i == n - 1 else r0 + size) - r0) for i, r0 in enumerate(starts)]


def _attention(mode, q, k, v, parts, kv_block=None, bias=None, sink=None, bq=BLOCK, nsub=1, single_shot=True):
    b, tq, wq = q.shape
    srcs = tuple(parts)
    rows, blk = (k.shape[1], 0) if kv_block is None else kv_block
    in_specs = [pl.BlockSpec((1, nsub * bq, wq), lambda bi, qi: (bi, qi, 0)),
                pl.BlockSpec((1, rows, k.shape[2]), lambda bi, qi: (bi, blk, 0)),
                pl.BlockSpec((1, rows, v.shape[2]), lambda bi, qi: (bi, blk, 0))]
    args = [q, k, v]
    if bias is not None:
        npair = tq // bq

        for u in range(nsub):
            def bias_idx(bi, qi, u=u):
                p = qi * nsub + u
                return (jnp.where(p < 2, p, jnp.where(p >= npair - 2, p - (npair - 2) + 3, 2)), 0, 0, 0)
            in_specs.append(pl.BlockSpec((1,) + bias.shape[1:], bias_idx))
            args.append(bias)
    if sink is not None:
        in_specs.append(pl.BlockSpec(memory_space=pltpu.SMEM))
        args.append(sink)
    return pl.pallas_call(
        functools.partial(_attn_kernel, mode, bq, nsub, srcs, single_shot, bias is not None, sink is not None),
        grid=(b, tq // (nsub * bq)),
        in_specs=in_specs,
        out_specs=pl.BlockSpec((1, nsub * bq, 256), lambda bi, qi: (bi, qi, 0)),
        out_shape=jax.ShapeDtypeStruct((b, tq, 256), BF16),
        compiler_params=_cparams("parallel", "arbitrary"),
    )(*args)


def _route(logits):
    lane = lax.broadcasted_iota(jnp.int32, logits.shape, 1).astype(F32)
    big = float(LANES)
    is_g = lane < N_GROUPS
    lg = jnp.where(is_g, logits, NEG)
    gmax = jnp.max(lg, axis=1, keepdims=True)
    grp = jnp.min(jnp.where(lg == gmax, lane, big), axis=1, keepdims=True)
    den = jnp.sum(jnp.where(is_g, jnp.exp(lg - gmax), 0.0), axis=1, keepdims=True)
    p_grp = 1.0 / den
    e0 = N_GROUPS + EXPERTS_PER_GROUP * grp
    in_grp = (lane >= e0) & (lane < e0 + EXPERTS_PER_GROUP)
    le = jnp.where(in_grp, logits, NEG)
    v1 = jnp.max(le, axis=1, keepdims=True)
    i1 = jnp.min(jnp.where(in_grp & (le == v1), lane, big), axis=1, keepdims=True)
    rest = in_grp & (lane != i1)
    le2 = jnp.where(rest, logits, NEG)
    v2 = jnp.max(le2, axis=1, keepdims=True)
    i2 = jnp.min(jnp.where(rest & (le2 == v2), lane, big), axis=1, keepdims=True)
    t = jnp.exp(v2 - v1)
    w1 = p_grp / (1.0 + t)
    w2 = p_grp * t / (1.0 + t)
    first_lower = i1 < i2
    e_lo = jnp.where(first_lower, i1, i2) - N_GROUPS
    e_hi = jnp.where(first_lower, i2, i1) - N_GROUPS
    w_lo = jnp.where(first_lower, w1, w2)
    w_hi = jnp.where(first_lower, w2, w1)
    a = e_lo - EXPERTS_PER_GROUP * grp
    b = e_hi - EXPERTS_PER_GROUP * grp
    pair = a * (2 * EXPERTS_PER_GROUP - 1 - a) * 0.5 + (b - a - 1.0)
    pair = jnp.where(pair == 3.0, 4.0, jnp.where(pair == 4.0, 3.0, pair))
    bucket = grp * N_PAIRS + pair
    return jnp.where(lane == 0, e_lo, jnp.where(lane == 1, e_hi, jnp.where(lane == 2, w_lo,
                     jnp.where(lane == 3, w_hi, jnp.where(lane == 4, bucket, 0.0)))))


def _out_kernel(alpha, oa, ob, oc, od, x_ref, g1_ref, sc2_ref, sh2_ref, w_ref, lng_ref, lnb_ref,
                wrh_ref, wrl_ref, br_ref, *rest):
    x1_ref, h2_ref, ids_ref = rest[-3:]
    mix = (_dot(oa[...], w_ref[0:256, :]) + _dot(ob[...], w_ref[256:512, :])
           + _dot(oc[...], w_ref[512:768, :]) + _dot(od[...], w_ref[768:1024, :]))
    x1 = _ln(alpha * x_ref[...] + g1_ref[0] * mix) * lng_ref[...] + lnb_ref[...]
    x1_ref[...] = x1
    h2 = _ln(x1) * (1.0 + sc2_ref[0]) + sh2_ref[0]
    hh, hl = _split(h2)
    logits = _dot(hl, wrh_ref[...]) + _dot(hh, wrl_ref[...]) + _dot(hh, wrh_ref[...]) + br_ref[...]
    rh, rl = _split(_route(logits))
    d = h2.shape[1]
    h2_ref[:, 0:d] = hh
    h2_ref[:, d:d + LANES] = rh
    h2_ref[:, d + LANES:d + 2 * LANES] = rl
    pick = ((lax.broadcasted_iota(jnp.int32, (8, LANES), 0) == 0)
            & (lax.broadcasted_iota(jnp.int32, (8, LANES), 1) == 4)).astype(BF16)
    ids_ref[...] = _dot_nt(pick, rh)


def _out_proj(alpha, outs, x2d, g1, sc2, sh2, lw, tokens_per_batch, n_rows, row0, prev=None):
    n, d = x2d.shape
    tm = min(OUT_TILE, tokens_per_batch)
    per_b = tokens_per_batch // tm
    row = lambda i: (i, 0)
    full = lambda i: (0, 0)
    perb = lambda i: (i // per_b, 0, 0)
    in_specs = [pl.BlockSpec((tm, 256), row)] * 4 + [
        pl.BlockSpec((tm, d), row),
        pl.BlockSpec((1, 1, d), perb), pl.BlockSpec((1, 1, d), perb), pl.BlockSpec((1, 1, d), perb),
        pl.BlockSpec((d, d), full), pl.BlockSpec((1, d), full), pl.BlockSpec((1, d), full),
        pl.BlockSpec((d, LANES), full), pl.BlockSpec((d, LANES), full), pl.BlockSpec((1, LANES), full)]
    args = [*outs, x2d, g1, sc2, sh2, lw["w_out"], lw["ln1_g"], lw["ln1_b"], lw["wr_hi"], lw["wr_lo"], lw["br"]]
    aliases = {}
    if prev is not None:
        for k, a in enumerate(prev):
            aliases[len(args)] = 1 + k
            in_specs.append(pl.BlockSpec(memory_space=pl.ANY))
            args.append(a)
    return pl.pallas_call(
        functools.partial(_out_kernel, alpha),
        grid=(n // tm,),
        in_specs=in_specs,
        out_specs=[pl.BlockSpec((tm, d), row), pl.BlockSpec((tm, d + H2_EXT), lambda i: (i + row0 // tm, 0)),
                   pl.BlockSpec((8, tm), lambda i: (0, i + row0 // tm))],
        out_shape=[jax.ShapeDtypeStruct((n, d), F32), jax.ShapeDtypeStruct((n_rows, d + H2_EXT), BF16),
                   jax.ShapeDtypeStruct((8, n_rows), F32)],
        input_output_aliases=aliases,
        compiler_params=_cparams("parallel"),
    )(*args)


def _moe_kernel(tile0, e0_ref, e1_ref, valid_ref, xs_ref, wg0, wg1, wu0, wu1, wd0, wd1, *rest):
    o_ref, wb = rest[-7], rest[-6:]
    t = pl.program_id(0) + tile0
    d = o_ref.shape[1]
    prev = jnp.maximum(t - 1, 0)
    first = pl.program_id(0) == 0

    @pl.when(first | (e0_ref[t] != e0_ref[prev]))
    def _():
        for src, dst in zip((wg0, wu0, wd0), wb[0::2]):
            dst[...] = src[0, 0].astype(BF16)

    @pl.when(first | (e1_ref[t] != e1_ref[prev]))
    def _():
        for src, dst in zip((wg1, wu1, wd1), wb[1::2]):
            dst[...] = src[0, 0].astype(BF16)

    @pl.when(valid_ref[t] == 1)
    def _():
        x = xs_ref[:, 0:d]
        rt = xs_ref[:, d:d + LANES].astype(F32) + xs_ref[:, d + LANES:d + 2 * LANES].astype(F32)
        lower_first = e0_ref[t] < e1_ref[t]
        w0 = jnp.where(lower_first, rt[:, 2:3], rt[:, 3:4])
        w1 = jnp.where(lower_first, rt[:, 3:4], rt[:, 2:3])

        def expert(wg, wu, wd):
            a = _silu(_dot(x, wg[...])) * _dot(x, wu[...])
            return _dot(a.astype(BF16), wd[...])

        y = w0 * expert(wb[0], wb[2], wb[4]) + w1 * expert(wb[1], wb[3], wb[5])
        o_ref[...] = y.astype(o_ref.dtype)

    @pl.when(valid_ref[t] == 0)
    def _():
        o_ref[...] = jnp.zeros_like(o_ref)


def _moe_experts(layer, tile0, e0, e1, valid, xs, wg, wu, wd, ys_prev):
    p, dx = xs.shape
    d = dx - H2_EXT
    tm = MOE_TILE
    de = wg.shape[3]
    w0 = lambda t, e0, e1, v: (layer, e0[t + tile0], 0, 0)
    w1 = lambda t, e0, e1, v: (layer, e1[t + tile0], 0, 0)
    in_specs = [pl.BlockSpec((tm, dx), lambda t, e0, e1, v: (t, 0)),
                pl.BlockSpec((1, 1, d, de), w0), pl.BlockSpec((1, 1, d, de), w1),
                pl.BlockSpec((1, 1, d, de), w0), pl.BlockSpec((1, 1, d, de), w1),
                pl.BlockSpec((1, 1, de, d), w0), pl.BlockSpec((1, 1, de, d), w1)]
    args = [e0, e1, valid, xs, wg, wg, wu, wu, wd, wd]
    aliases = {}
    if ys_prev is not None:
        aliases[len(args)] = 0
        in_specs.append(pl.BlockSpec(memory_space=pl.ANY))
        args.append(ys_prev)
    return pl.pallas_call(
        functools.partial(_moe_kernel, tile0),
        grid_spec=pltpu.PrefetchScalarGridSpec(
            num_scalar_prefetch=3,
            grid=(p // tm,),
            in_specs=in_specs,
            out_specs=pl.BlockSpec((tm, d), lambda t, e0, e1, v: (t + tile0, 0)),
            scratch_shapes=[pltpu.VMEM((d, de), BF16)] * 4 + [pltpu.VMEM((de, d), BF16)] * 2),
        out_shape=jax.ShapeDtypeStruct((e0.shape[0] * tm, d), BF16),
        input_output_aliases=aliases,
        compiler_params=_cparams("arbitrary"),
    )(*args)


_PAIR_TABLE = ((0, 1), (0, 2), (0, 3), (1, 3), (1, 2), (3, 2))


def _moe(layer, h2x, ids, wg, wu, wd, chunks):
    n = h2x.shape[0]
    tm = MOE_TILE
    n_tiles = -(-(n // tm + N_BUCKETS) // chunks) * chunks
    bucket = ids[0].astype(jnp.int32)
    tok = jnp.arange(n, dtype=jnp.int32)
    sorted_bucket, order = lax.sort((bucket, tok), num_keys=1)
    _, rank = lax.sort((order, tok), num_keys=1)

    bids = jnp.arange(N_BUCKETS + 1, dtype=jnp.int32)
    edges = jnp.sum(sorted_bucket[None, :] < bids[:, None], axis=1, dtype=jnp.int32)
    counts = edges[1:] - edges[:-1]
    src_off = edges[:-1]
    tiles_per = (counts + tm - 1) // tm
    tile_end = jnp.cumsum(tiles_per)
    tile_start = tile_end - tiles_per

    tile_ids = jnp.arange(n_tiles, dtype=jnp.int32)
    valid = (tile_ids < tile_end[-1]).astype(jnp.int32)
    tile_bucket = jnp.sum(tile_ids[:, None] >= tile_end[None, :], axis=1, dtype=jnp.int32)
    last_bucket = jnp.sum(tile_end[-1] - 1 >= tile_end, dtype=jnp.int32)
    tile_bucket = jnp.minimum(jnp.where(valid == 1, tile_bucket, last_bucket), N_BUCKETS - 1)
    onehot_tb = tile_bucket[:, None] == bids[None, :N_BUCKETS]
    pick = lambda table: jnp.sum(jnp.where(onehot_tb, table[None, :], 0), axis=1)
    pairs = jnp.asarray(_PAIR_TABLE, dtype=jnp.int32)
    tg = tile_bucket // N_PAIRS
    e0 = tg * EXPERTS_PER_GROUP + pairs[tile_bucket % N_PAIRS, 0]
    e1 = tg * EXPERTS_PER_GROUP + pairs[tile_bucket % N_PAIRS, 1]

    first_row = (tile_ids - pick(tile_start)) * tm
    within = first_row[:, None] + jnp.arange(tm, dtype=jnp.int32)[None, :]
    row_ok = (within < pick(counts)[:, None]) & (valid[:, None] == 1)
    idx = jnp.clip(pick(src_off)[:, None] + within, 0, n - 1)
    src = jnp.where(row_ok, order.at[idx.reshape(-1)].get(mode="promise_in_bounds").reshape(idx.shape), 0).reshape(-1)

    per_chunk = n_tiles // chunks
    ys = None
    for c in range(chunks):
        xs = h2x.at[src[c * per_chunk * tm:(c + 1) * per_chunk * tm]].get(mode="promise_in_bounds")
        ys = _moe_experts(layer, c * per_chunk, e0, e1, valid, xs, wg, wu, wd, ys)

    onehot_b = bucket[None, :] == bids[:N_BUCKETS, None]
    base = jnp.sum(jnp.where(onehot_b, (tile_start * tm - src_off)[:, None], 0), axis=0)
    return ys.at[base + rank].get(mode="promise_in_bounds")


def _fin_kernel(alpha, x_ref, y_ref, g2_ref, lng_ref, lnb_ref, o_ref):
    o_ref[...] = _ln(alpha * x_ref[...] + g2_ref[0] * y_ref[...]) * lng_ref[...] + lnb_ref[...]


def _finish(alpha, x1, y, g2, ln_g, ln_b, tokens_per_batch):
    n, d = x1.shape
    tm = min(ROW_TILE, tokens_per_batch)
    per_b = tokens_per_batch // tm
    row = lambda i: (i, 0)
    full = lambda i: (0, 0)
    return pl.pallas_call(
        functools.partial(_fin_kernel, alpha),
        grid=(n // tm,),
        in_specs=[pl.BlockSpec((tm, d), row), pl.BlockSpec((tm, d), row),
                  pl.BlockSpec((1, 1, d), lambda i: (i // per_b, 0, 0)),
                  pl.BlockSpec((1, d), full), pl.BlockSpec((1, d), full)],
        out_specs=pl.BlockSpec((tm, d), row),
        out_shape=jax.ShapeDtypeStruct((n, d), F32),
        compiler_params=_cparams("parallel"),
    )(x1, y, g2, ln_g, ln_b)


def _rope_tables(t):
    pos = jnp.arange(t)
    row = (pos // GRID_W).astype(F32)[:, None]
    col = (pos % GRID_W).astype(F32)[:, None]
    lane = jnp.arange(LANES)

    def build(d, half, active):
        axis_col = (d // half) % 2 == 1
        w = d % half
        first = w < half // 2
        f = (w % (half // 2)).astype(F32)
        freq = ROPE_THETA ** (-(2.0 * f) / half)
        ang = jnp.where(axis_col[None, :], col, row) * freq[None, :]
        cos = jnp.where(active[None, :], jnp.cos(ang), 1.0)
        sin = jnp.where(active[None, :], jnp.sin(ang), 0.0)
        return jnp.stack([cos, jnp.where(first[None, :], 0.0, sin), jnp.where(first[None, :], -sin, 0.0)])

    t64 = build(lane % HEAD_DIM, HEAD_DIM // 2, lane >= 0)
    in_rope = (lane >= MLA_NOPE) & (lane < MLA_QK)
    tq = build(jnp.where(in_rope, lane - MLA_NOPE, 0), MLA_ROPE // 2, in_rope)
    in_kr = lane < MLA_ROPE
    tk = build(jnp.where(in_kr, lane, 0), MLA_ROPE // 2, in_kr)
    return t64, tq, tk


def _nbr_bias_tables(na_bias, rows):
    heads = na_bias.shape[0]
    npair = rows // 2
    kr = min(NA_ROWS, rows)
    qcol = jnp.arange(GRID_W)[:, None]
    kcol = jnp.arange(GRID_W)[None, :]
    c0 = jnp.clip(qcol - NA_COLS // 2, 0, GRID_W - NA_COLS)
    col_ok = (kcol >= c0) & (kcol < c0 + NA_COLS)
    col_off = kcol - qcol + NA_COLS - 1
    onehot = (col_off[None] == jnp.arange(2 * NA_COLS - 1)[:, None, None]).astype(F32)
    colmat = jnp.einsum("hdc,cqk->hdqk", na_bias, onehot, precision=lax.Precision.HIGHEST) * LOG2E
    colmat = jnp.where(col_ok[None, None], colmat, NEG)
    outside = jnp.full((heads, GRID_W, GRID_W), NEG, F32)
    tabs = []
    for p in (0, 1, 2, npair - 2, npair - 1):
        ks = min(max(2 * p - NA_ROWS // 2, 0), rows - NBR_SPAN_ROWS)
        per_qrow = []
        for qr in range(2):
            r = 2 * p + qr
            r0 = min(max(r - kr // 2, 0), rows - kr)
            blocks = []
            for kri in range(NBR_SPAN_ROWS):
                krow = ks + kri
                blocks.append(colmat[:, krow - r + NA_ROWS - 1] if r0 <= krow < r0 + kr else outside)
            per_qrow.append(jnp.concatenate(blocks, axis=-1))
        tab = jnp.concatenate(per_qrow, axis=1)
        tabs.append(tab.reshape(heads // 2, 4 * GRID_W, NBR_SPAN_ROWS * GRID_W))
    return jnp.stack(tabs)


def _layer_weights(l, w_in, gqa_q_gain, gqa_k_gain, mla_q_gain, mla_w_qb, mla_kv_gain, mla_w_kvb,
                   w_out, ln1_g, ln1_b, router_group_w, router_group_b, router_expert_w, router_expert_b):
    d = w_in.shape[1]
    d_proj = w_in.shape[2]
    lw = {}
    lw["w_in"] = jnp.pad(w_in[l], ((0, 0), (0, 2304 - d_proj))).astype(BF16)
    lw["gq"] = jnp.tile(gqa_q_gain[l], 4)[None, :]
    lw["gk"] = jnp.tile(gqa_k_gain[l], 2)[None, :]
    lw["gmq"] = mla_q_gain[l][None, :]
    lw["gmkv"] = mla_kv_gain[l][None, :]
    wqb = mla_w_qb[l].reshape(-1, 4, MLA_QK)
    lw["w_qb"] = jnp.pad(wqb, ((0, 0), (0, 0), (0, LANES - MLA_QK))).reshape(-1, 4 * LANES).astype(BF16)
    wkvb = mla_w_kvb[l].reshape(-1, 4, 2 * HEAD_DIM)
    k_part = jnp.pad(wkvb[:, :, :MLA_NOPE], ((0, 0), (0, 0), (0, LANES - MLA_NOPE))).reshape(-1, 4 * LANES)
    v_part = jnp.pad(wkvb[:, :, MLA_NOPE:], ((0, 0), (0, 0), (0, LANES - HEAD_DIM))).reshape(-1, 4 * LANES)
    lw["w_kvb"] = jnp.concatenate([k_part, v_part], axis=1).astype(BF16)
    idx = jnp.arange(256) // HEAD_DIM
    lw["bd"] = (idx[:, None] == idx[None, :]).astype(BF16)
    lw["w_out"] = w_out[l].astype(BF16)
    lw["ln1_g"] = ln1_g[l][None, :]
    lw["ln1_b"] = ln1_b[l][None, :]
    wr = jnp.concatenate([router_group_w[l], router_expert_w[l]], axis=1)
    wr = jnp.pad(wr, ((0, 0), (0, LANES - wr.shape[1])))
    lw["wr_hi"] = wr.astype(BF16)
    lw["wr_lo"] = (wr - lw["wr_hi"].astype(F32)).astype(BF16)
    br = jnp.concatenate([router_group_b[l], router_expert_b[l]])
    lw["br"] = jnp.pad(br, (0, LANES - br.shape[0]))[None, :]
    return lw


def kernel(x, c, ctx, c_ctx, w_ada, b_ada, w_in, na_bias, gqa_q_gain, gqa_k_gain, win_sink, mla_q_gain, mla_w_qb, mla_kv_gain, mla_w_kvb, w_out, ln1_g, ln1_b, router_group_w, router_group_b, router_expert_w, router_expert_b, moe_w_gate, moe_w_up, moe_w_down, ln2_g, ln2_b):
    bsz, t, d = x.shape
    tc = ctx.shape[1]
    depth = w_ada.shape[0]
    rows = t // GRID_W
    assert t % (2 * GRID_W) == 0 and rows >= NBR_SPAN_ROWS and rows // 2 >= 5
    assert t >= BLOCK + 2 * WINDOW and t % ROW_TILE == 0 and tc % (8 * ROW_SPLIT) == 0 and t % tc == 0
    assert tc <= ROW_TILE and t % OUT_TILE == 0
    assert t % (LOCAL_NSUB * BLOCK) == 0 and t % GLOBAL_BQ == 0
    s_tot = t + tc
    alpha = (2.0 * depth) ** 0.25

    ada_rows = -(-(bsz + 1) // 8) * 8
    cs = jnp.zeros((ada_rows, d), F32).at[:bsz].set(c).at[bsz].set(c_ctx)
    mod = _ada(cs, w_ada, b_ada)
    tabs = _rope_tables(t)

    xl = x.reshape(bsz * t, d)
    xc = ctx.reshape(bsz * tc, d)
    fin_l = fin_c = None

    for l in range(depth):
        with_ctx = l < depth - 1
        lw = _layer_weights(l, w_in, gqa_q_gain, gqa_k_gain, mla_q_gain, mla_w_qb, mla_kv_gain,
                            mla_w_kvb, w_out, ln1_g, ln1_b, router_group_w, router_group_b,
                            router_expert_w, router_expert_b)
        ln2g, ln2b = ln2_g[l][None, :], ln2_b[l][None, :]

        def mods(lo, hi):
            parts = jnp.split(mod[l, lo:hi], 6, axis=-1)
            return [jnp.broadcast_to(p[:, None, :], (bsz, 1, d)) for p in parts]
        sh1, sc1, g1, sh2, sc2, g2 = mods(0, bsz)
        sh1c, sc1c, g1c, sh2c, sc2c, g2c = mods(bsz, bsz + 1)

        xl, pl_lat = _project(xl, fin_l, sc1, sh1, lw, tabs, t, s_tot, 0)
        xc, pl_ctx = _project(xc, fin_c, sc1c, sh1c, lw, None, tc, s_tot, t,
                              kv_prev=[pl_lat[i] for i in _KV_OUTS])
        qa, qb, qc, qd = [pl_lat[i].reshape(bsz, t, -1) for i in _Q_OUTS]
        qa_c, qb_c, qc_c, qd_c = [pl_ctx[i].reshape(bsz, tc, -1) for i in _Q_OUTS]
        ka, va, kb, vb, kc, vc, kd, vd = [pl_ctx[i] for i in _KV_OUTS]

        bias_tab = _nbr_bias_tables(na_bias[l], rows)
        everything = dict(parts=_key_ranges(s_tot), bq=GLOBAL_BQ, single_shot=False)
        out_a = _attention("mha", qa, ka, va, [("nbr", 0, t), ("full", t, tc)], bias=bias_tab, nsub=LOCAL_NSUB)
        out_b = _attention("gqa", qb, kb, vb, **everything)
        out_c = _attention("gqa", qc, kc, vc, [("win", 0, t), ("full", t, tc)], sink=win_sink[l],
                           nsub=LOCAL_NSUB)
        out_d = _attention("mla", qd, kd, vd, **everything)
        outs = [o.reshape(bsz * t, 256) for o in (out_a, out_b, out_c, out_d)]

        n_lat, n_ctx = bsz * t, bsz * tc
        n_moe = n_lat + n_ctx if with_ctx else n_lat
        x1, h2x, ids = _out_proj(alpha, outs, xl, g1, sc2, sh2, lw, t, n_moe, 0)
        if with_ctx:
            ctx_only = dict(parts=[("full", 0, tc)], kv_block=(tc, t // tc))
            outs_c = [
                _attention("mha", qa_c, ka, va, **ctx_only),
                _attention("gqa", qb_c, kb, vb, **ctx_only),
                _attention("gqa", qc_c, kc, vc, sink=win_sink[l], **ctx_only),
                _attention("mla", qd_c, kd, vd, **ctx_only),
            ]
            outs_c = [o.reshape(n_ctx, 256) for o in outs_c]
            x1c, h2x, ids = _out_proj(alpha, outs_c, xc, g1c, sc2c, sh2c, lw, tc, n_moe, n_lat, prev=(h2x, ids))
        y = _moe(l, h2x, ids, moe_w_gate, moe_w_up, moe_w_down, MOE_CHUNKS)
        fin_l = (alpha, x1, y, g2, ln2g, ln2b, 0)
        if with_ctx:
            fin_c = (alpha, x1c, y, g2c, ln2g, ln2b, n_lat)

    return _finish(*fin_l[:6], t).reshape(bsz, t, d)
```

```python
import functools

import jax
import jax.numpy as jnp
from jax import lax
from jax.experimental import pallas as pl
from jax.experimental.pallas import tpu as pltpu

F32 = jnp.float32
BF16 = jnp.bfloat16

GRID_W = 64
HEAD_DIM = 64
BLOCK = 128
WINDOW = 128
ROPE_THETA = 10000.0
NEG = -1e30
EPS = 1e-6
LOG2E = 1.4426950408889634

NA_ROWS = 8
NA_COLS = 16
NBR_SPAN_ROWS = 10
MLA_NOPE = 64
MLA_ROPE = 32
MLA_QK = MLA_NOPE + MLA_ROPE
N_GROUPS = 4
EXPERTS_PER_GROUP = 4
N_EXPERTS = 16
N_PAIRS = 6
N_BUCKETS = N_GROUPS * N_PAIRS

LANES = 128
VMEM_LIMIT = 48 * 1024 * 1024

ROW_TILE = 512
ROW_SPLIT = 2
OUT_TILE = 1024
MOE_TILE = 256
MOE_CHUNKS = 4
KV_CHUNK = 2560
GLOBAL_BQ = 256
LOCAL_NSUB = 8
H2_EXT = 256


def _cparams(*sem):
    return pltpu.CompilerParams(dimension_semantics=sem, vmem_limit_bytes=VMEM_LIMIT)


def _dot(a, b):
    return jnp.dot(a, b, preferred_element_type=F32)


def _dot_nt(a, b):
    return lax.dot_general(a, b, (((1,), (1,)), ((), ())), preferred_element_type=F32)


def _split(a):
    hi = a.astype(BF16)
    lo = (a - hi.astype(F32)).astype(BF16)
    return hi, lo


def _dot3(a, b):
    ah, al = _split(a)
    bh, bl = _split(b)
    return _dot(al, bh) + _dot(ah, bl) + _dot(ah, bh)


def _ln(x):
    mu = jnp.mean(x, axis=-1, keepdims=True)
    xc = x - mu
    var = jnp.mean(xc * xc, axis=-1, keepdims=True)
    return xc * lax.rsqrt(var + EPS)


def _silu(g):
    return g / (1.0 + jnp.exp(-g))


def _ada_kernel(c_ref, w_ref, b_ref, o_ref):
    o_ref[0] = _dot3(_silu(c_ref[...]), w_ref[0]) + b_ref[0]


def _ada(cs, w_ada, b_ada):
    depth, d, n = w_ada.shape
    rows = cs.shape[0]
    tn = 1536
    return pl.pallas_call(
        _ada_kernel,
        grid=(depth, n // tn),
        in_specs=[pl.BlockSpec((rows, d), lambda l, j: (0, 0)),
                  pl.BlockSpec((1, d, tn), lambda l, j: (l, 0, j)),
                  pl.BlockSpec((1, 1, tn), lambda l, j: (l, 0, j))],
        out_specs=pl.BlockSpec((1, rows, tn), lambda l, j: (l, 0, j)),
        out_shape=jax.ShapeDtypeStruct((depth, rows, n), F32),
        compiler_params=_cparams("parallel", "parallel"),
    )(cs, w_ada, b_ada.reshape(depth, 1, n))


def _head_rms(x, bd, gain):
    hi, lo = _split(x * x)
    ss = _dot(lo, bd) + _dot(hi, bd)
    return x * lax.rsqrt(ss * (1.0 / HEAD_DIM) + EPS) * gain


def _rms(x, gain):
    return x * lax.rsqrt(jnp.mean(x * x, axis=-1, keepdims=True) + EPS) * gain


def _rope(x, tab_ref, shift):
    outs = []
    for j in range(x.shape[1] // LANES):
        xb = x[:, j * LANES:(j + 1) * LANES]
        outs.append(xb * tab_ref[0]
                    + pltpu.roll(xb, shift, 1) * tab_ref[1]
                    + pltpu.roll(xb, LANES - shift, 1) * tab_ref[2])
    return outs[0] if len(outs) == 1 else jnp.concatenate(outs, axis=1)


def _value_tiles(v):
    lane = lax.broadcasted_iota(jnp.int32, (v.shape[0], LANES), 1)
    fill = jnp.where(lane == HEAD_DIM, 1.0, 0.0)
    outs = []
    for j in range(v.shape[1] // LANES):
        blk = v[:, j * LANES:(j + 1) * LANES]
        outs.append(jnp.where(lane < HEAD_DIM, blk, fill))
        outs.append(jnp.where(lane < HEAD_DIM, pltpu.roll(blk, HEAD_DIM, 1), fill))
    return jnp.concatenate(outs, axis=1).astype(BF16)


def _proj_kernel(rope, fin_alpha, n_aliased, *refs):
    refs = list(refs)
    if fin_alpha is None:
        x_ref = refs.pop(0)
    else:
        x1_ref, y_ref, g2_ref, lng_ref, lnb_ref = refs[:5]
        refs = refs[5:]
    (sc_ref, sh_ref, w_ref, gq_ref, gk_ref, gmq_ref, gmkv_ref,
     wqb_ref, wkvb_ref, bd_ref) = refs[:10]
    refs = refs[10:]
    tabs = None
    if rope:
        tabs = refs[:3]
        refs = refs[3:]
    refs = refs[n_aliased:]
    if fin_alpha is not None:
        xo_ref = refs.pop(0)
    refs = [r.at[0] if len(r.shape) == 3 else r for r in refs]
    group = refs[0].shape[0] // ROW_SPLIT
    for u in range(ROW_SPLIT):
        rows = slice(u * group, (u + 1) * group)
        if fin_alpha is None:
            x = x_ref[rows, :]
        else:
            x = (_ln(fin_alpha * x1_ref[rows, :] + g2_ref[0] * y_ref[rows, :]) * lng_ref[...]
                 + lnb_ref[...])
            xo_ref[rows, :] = x
        _proj_rows(rope, x, rows, sc_ref, sh_ref, w_ref, gq_ref, gk_ref, gmq_ref, gmkv_ref, wqb_ref,
                   wkvb_ref, bd_ref, tabs, refs)


def _proj_rows(rope, x, rows, sc_ref, sh_ref, w_ref, gq_ref, gk_ref, gmq_ref, gmkv_ref, wqb_ref, wkvb_ref,
               bd_ref, tabs, outs):
    qa, ka, va, qb, kb, vb, qc, kc, vc, qd, kd, vd = outs
    if rope:
        t64_ref, tq_ref, tk_ref = [t.at[:, rows, :] for t in tabs]
    h = _ln(x) * (1.0 + sc_ref[0]) + sh_ref[0]
    hb = h.astype(BF16)
    sq = HEAD_DIM ** -0.5 * LOG2E

    def proj(a, b):
        return _dot(hb, w_ref[:, a:b])

    pa = proj(0, 768)
    qa[rows, :] = (pa[:, 0:256] * sq).astype(BF16)
    ka[rows, :] = pa[:, 256:512].astype(BF16)
    va[rows, :] = _value_tiles(pa[:, 512:768])

    pb = proj(768, 1280)
    q = _head_rms(pb[:, 0:256], bd_ref[...], gq_ref[...])
    k = _head_rms(pb[:, 256:384], bd_ref[0:128, 0:128], gk_ref[...])
    if rope:
        q = _rope(q, t64_ref, 16)
        k = _rope(k, t64_ref, 16)
    qb[rows, :] = (q * sq).astype(BF16)
    kb[rows, :] = k.astype(BF16)
    vb[rows, :] = _value_tiles(pb[:, 384:512])

    pc = proj(1280, 1792)
    q = pc[:, 0:256]
    k = pc[:, 256:384]
    if rope:
        q = _rope(q, t64_ref, 16)
        k = _rope(k, t64_ref, 16)
    qc[rows, :] = (q * sq).astype(BF16)
    kc[rows, :] = k.astype(BF16)
    vc[rows, :] = _value_tiles(pc[:, 384:512])

    pd = proj(1792, 2304)
    cq = _rms(pd[:, 0:256], gmq_ref[...])
    q = _dot(cq.astype(BF16), wqb_ref[...])
    if rope:
        q = _rope(q, tq_ref, 8)
    qd[rows, :] = (q * (MLA_QK ** -0.5 * LOG2E)).astype(BF16)
    ckv = _rms(pd[:, 256:384], gmkv_ref[...])
    kvu = _dot(ckv.astype(BF16), wkvb_ref[...])
    kr = pd[:, 384:512]
    if rope:
        kr = _rope(kr, tk_ref, 8)
    kr = pltpu.roll(kr, MLA_NOPE, 1)
    kd[rows, :] = jnp.concatenate(
        [kvu[:, j * LANES:(j + 1) * LANES] + kr for j in range(4)], axis=1).astype(BF16)
    lane = lax.broadcasted_iota(jnp.int32, (kr.shape[0], LANES), 1)
    fill = jnp.where(lane == HEAD_DIM, 1.0, 0.0)
    vd[rows, :] = jnp.concatenate(
        [kvu[:, (4 + j) * LANES:(5 + j) * LANES] + fill for j in range(4)], axis=1).astype(BF16)


_PROJ_WIDTHS = (256, 256, 512, 256, 128, 256, 256, 128, 256, 512, 512, 512)


_Q_OUTS = (0, 3, 6, 9)
_KV_OUTS = (1, 2, 4, 5, 7, 8, 10, 11)


def _project(x2d, fin, sc, sh, lw, tabs, tokens_per_batch, kv_rows, kv_row0, kv_prev=None):
    n, d = (x2d if fin is None else fin[1]).shape
    tm = min(ROW_TILE, tokens_per_batch)
    per_b = tokens_per_batch // tm
    bsz = n // tokens_per_batch
    rope = tabs is not None
    row = lambda i: (i, 0)
    full = lambda i: (0, 0)
    perb = lambda i: (i // per_b, 0, 0)
    kvrow = lambda i: (i // per_b, kv_row0 // tm + i % per_b, 0)
    if fin is None:
        in_specs = [pl.BlockSpec((tm, d), row)]
        args = [x2d]
    else:
        in_specs = [pl.BlockSpec((tm, d), row), pl.BlockSpec((tm, d), lambda i: (i + fin[6] // tm, 0)),
                    pl.BlockSpec((1, 1, d), perb), pl.BlockSpec((1, d), full), pl.BlockSpec((1, d), full)]
        args = list(fin[1:6])
    in_specs += [pl.BlockSpec((1, 1, d), perb), pl.BlockSpec((1, 1, d), perb),
                 pl.BlockSpec(lw["w_in"].shape, full),
                 pl.BlockSpec((1, 256), full), pl.BlockSpec((1, 128), full),
                 pl.BlockSpec((1, 256), full), pl.BlockSpec((1, 128), full),
                 pl.BlockSpec(lw["w_qb"].shape, full), pl.BlockSpec(lw["w_kvb"].shape, full),
                 pl.BlockSpec((256, 256), full)]
    args += [sc, sh, lw["w_in"], lw["gq"], lw["gk"], lw["gmq"], lw["gmkv"],
             lw["w_qb"], lw["w_kvb"], lw["bd"]]
    if rope:
        tab = lambda i: (0, i % per_b, 0)
        in_specs += [pl.BlockSpec((3, tm, LANES), tab)] * 3
        args += list(tabs)
    out_specs = [pl.BlockSpec((1, tm, w), kvrow) if i in _KV_OUTS else pl.BlockSpec((tm, w), row)
                 for i, w in enumerate(_PROJ_WIDTHS)]
    out_shape = [jax.ShapeDtypeStruct((bsz, kv_rows, w) if i in _KV_OUTS else (n, w), BF16)
                 for i, w in enumerate(_PROJ_WIDTHS)]
    lead = 0 if fin is None else 1
    if fin is not None:
        out_specs.insert(0, pl.BlockSpec((tm, d), row))
        out_shape.insert(0, jax.ShapeDtypeStruct((n, d), F32))
    aliases = {}
    if kv_prev is not None:
        for a, i in zip(kv_prev, _KV_OUTS):
            aliases[len(args)] = lead + i
            in_specs.append(pl.BlockSpec(memory_space=pl.ANY))
            args.append(a)
    outs = pl.pallas_call(
        functools.partial(_proj_kernel, rope, None if fin is None else fin[0], len(aliases)),
        grid=(n // tm,),
        in_specs=in_specs,
        out_specs=out_specs,
        out_shape=out_shape,
        input_output_aliases=aliases,
        compiler_params=_cparams("parallel"),
    )(*args)
    return (x2d, list(outs)) if fin is None else (outs[0], list(outs[1:]))


def _softmax_step(s, vts, bq, m, acc):
    m_new = jnp.maximum(m, jnp.max(s, axis=1, keepdims=True))
    alpha = jnp.exp2(m - m_new)
    p = jnp.exp2(s - m_new).astype(BF16)
    if len(vts) == 1:
        pv = _dot(p, vts[0])
    else:
        pv = jnp.concatenate([_dot(p[:bq], vts[0]), _dot(p[bq:], vts[1])], axis=0)
    return [m_new, alpha * acc + pv]


def _attn_kernel(mode, bq, nsub, srcs, single_shot, has_bias, has_sink, *refs):
    refs = list(refs)
    q_ref = refs.pop(0)
    k_ref, v_ref = refs.pop(0), refs.pop(0)
    bias_refs = [refs.pop(0) for _ in range(nsub)] if has_bias else None
    sink_ref = refs.pop(0) if has_sink else None
    o_ref = refs.pop(0)

    for u in range(nsub):
        qi = pl.program_id(1) * nsub + u
        rows = slice(u * bq, (u + 1) * bq)
        rows2 = 2 * bq
        lo = lax.broadcasted_iota(jnp.int32, (bq, LANES), 1) < HEAD_DIM

        qss, kcs, vcs = [], [], []
        for j in range(2):
            if mode == "mla":
                qblk = q_ref[0, rows, 2 * LANES * j:2 * LANES * (j + 1)].astype(F32)
                first = lax.broadcasted_iota(jnp.int32, (bq, 2 * LANES), 1) < LANES
                qs = jnp.concatenate([jnp.where(first, qblk, 0.0), jnp.where(first, 0.0, qblk)], axis=0)
                kcs.append((2 * LANES * j, 2 * LANES * (j + 1)))
                vcs.append((2 * LANES * j, 2 * LANES * j + LANES))
            else:
                qblk = q_ref[0, rows, LANES * j:LANES * (j + 1)].astype(F32)
                if mode == "mha":
                    qe, qo = jnp.where(lo, qblk, 0.0), jnp.where(lo, 0.0, qblk)
                    kcs.append((LANES * j, LANES * (j + 1)))
                    vcs.append((2 * LANES * j, 2 * LANES * j + LANES))
                else:
                    rolled = pltpu.roll(qblk, HEAD_DIM, 1)
                    if j == 0:
                        qe, qo = jnp.where(lo, qblk, 0.0), jnp.where(lo, rolled, 0.0)
                    else:
                        qe, qo = jnp.where(lo, 0.0, rolled), jnp.where(lo, 0.0, qblk)
                    kcs.append((0, LANES))
                    vcs.append((LANES * j,))
                qs = jnp.concatenate([qe, qo], axis=0)
            qss.append(qs.astype(BF16))

        def pv(p, vts):
            if len(vts) == 1:
                return _dot(p, vts[0])
            return jnp.concatenate([_dot(p[:bq], vts[0]), _dot(p[bq:], vts[1])], axis=0)

        def tiles(j, off, n):
            kt = k_ref[0, pl.ds(off, n), kcs[j][0]:kcs[j][1]]
            return kt, [v_ref[0, pl.ds(off, n), v0:v0 + LANES] for v0 in vcs[j]]

        def block_scores(j, kind, row0, nrows):
            if kind == "full":
                kt, vts = tiles(j, row0, nrows)
                return _dot_nt(qss[j], kt), vts
            if kind == "win":
                span = bq + 2 * WINDOW
                start = pl.multiple_of(jnp.clip((qi - 1) * bq, 0, nrows - span), bq)
                r = lax.broadcasted_iota(jnp.int32, (rows2, span), 0)
                qpos = qi * bq + jnp.where(r >= bq, r - bq, r)
                kpos = start + lax.broadcasted_iota(jnp.int32, (rows2, span), 1)
                kt, vts = tiles(j, row0 + start, span)
                return jnp.where(jnp.abs(qpos - kpos) <= WINDOW, _dot_nt(qss[j], kt), NEG), vts
            span = NBR_SPAN_ROWS * GRID_W
            ks = jnp.clip(2 * qi - NA_ROWS // 2, 0, nrows // GRID_W - NBR_SPAN_ROWS)
            kt, vts = tiles(j, row0 + pl.multiple_of(ks * GRID_W, GRID_W), span)
            return _dot_nt(qss[j], kt) + bias_refs[u][0, j], vts

        if single_shot:
            state = []
            for j in range(2):
                parts = [block_scores(j, kind, row0, nrows) for kind, row0, nrows in srcs]
                m = functools.reduce(jnp.maximum, [jnp.max(sc, axis=1, keepdims=True) for sc, _ in parts])
                acc = functools.reduce(lambda a, b: a + b,
                                       [pv(jnp.exp2(sc - m).astype(BF16), vts) for sc, vts in parts])
                state += [m, acc]
        else:
            state = [jnp.full((rows2, 1), NEG, F32), jnp.zeros((rows2, LANES), F32)] * 2
            for kind, row0, nrows in srcs:
                for j in range(2):
                    sc, vts = block_scores(j, kind, row0, nrows)
                    state[2 * j:2 * j + 2] = _softmax_step(sc, vts, bq, state[2 * j], state[2 * j + 1])

        for j in range(2):
            m, acc = state[2 * j], state[2 * j + 1]
            l = acc[:, HEAD_DIM:HEAD_DIM + 1]
            if has_sink:
                sk = jnp.concatenate([jnp.full((bq, 1), sink_ref[2 * j] * LOG2E, F32),
                                      jnp.full((bq, 1), sink_ref[2 * j + 1] * LOG2E, F32)], axis=0)
                mf = jnp.maximum(m, sk)
                a = jnp.exp2(m - mf)
                l = l * a + jnp.exp2(sk - mf)
                acc = acc * a
            o = acc / l
            o_ref[0, rows, LANES * j:LANES * (j + 1)] = jnp.where(
                lo, o[:bq], pltpu.roll(o[bq:], HEAD_DIM, 1)).astype(BF16)


def _key_ranges(nrows):
    n = -(-nrows // KV_CHUNK)
    size = max(nrows // n // 256, 1) * 256
    starts = [i * size for i in range(n)]
    return [("full", r0, (nrows if i == n - 1 else r0 + size) - r0) for i, r0 in enumerate(starts)]


def _attention(mode, q, k, v, parts, kv_block=None, bias=None, sink=None, bq=BLOCK, nsub=1, single_shot=True):
    b, tq, wq = q.shape
    srcs = tuple(parts)
    rows, blk = (k.shape[1], 0) if kv_block is None else kv_block
    in_specs = [pl.BlockSpec((1, nsub * bq, wq), lambda bi, qi: (bi, qi, 0)),
                pl.BlockSpec((1, rows, k.shape[2]), lambda bi, qi: (bi, blk, 0)),
                pl.BlockSpec((1, rows, v.shape[2]), lambda bi, qi: (bi, blk, 0))]
    args = [q, k, v]
    if bias is not None:
        npair = tq // bq

        for u in range(nsub):
            def bias_idx(bi, qi, u=u):
                p = qi * nsub + u
                return (jnp.where(p < 2, p, jnp.where(p >= npair - 2, p - (npair - 2) + 3, 2)), 0, 0, 0)
            in_specs.append(pl.BlockSpec((1,) + bias.shape[1:], bias_idx))
            args.append(bias)
    if sink is not None:
        in_specs.append(pl.BlockSpec(memory_space=pltpu.SMEM))
        args.append(sink)
    return pl.pallas_call(
        functools.partial(_attn_kernel, mode, bq, nsub, srcs, single_shot, bias is not None, sink is not None),
        grid=(b, tq // (nsub * bq)),
        in_specs=in_specs,
        out_specs=pl.BlockSpec((1, nsub * bq, 256), lambda bi, qi: (bi, qi, 0)),
        out_shape=jax.ShapeDtypeStruct((b, tq, 256), BF16),
        compiler_params=_cparams("parallel", "arbitrary"),
    )(*args)


def _route(logits):
    lane = lax.broadcasted_iota(jnp.int32, logits.shape, 1).astype(F32)
    big = float(LANES)
    is_g = lane < N_GROUPS
    lg = jnp.where(is_g, logits, NEG)
    gmax = jnp.max(lg, axis=1, keepdims=True)
    grp = jnp.min(jnp.where(lg == gmax, lane, big), axis=1, keepdims=True)
    den = jnp.sum(jnp.where(is_g, jnp.exp(lg - gmax), 0.0), axis=1, keepdims=True)
    p_grp = 1.0 / den
    e0 = N_GROUPS + EXPERTS_PER_GROUP * grp
    in_grp = (lane >= e0) & (lane < e0 + EXPERTS_PER_GROUP)
    le = jnp.where(in_grp, logits, NEG)
    v1 = jnp.max(le, axis=1, keepdims=True)
    i1 = jnp.min(jnp.where(in_grp & (le == v1), lane, big), axis=1, keepdims=True)
    rest = in_grp & (lane != i1)
    le2 = jnp.where(rest, logits, NEG)
    v2 = jnp.max(le2, axis=1, keepdims=True)
    i2 = jnp.min(jnp.where(rest & (le2 == v2), lane, big), axis=1, keepdims=True)
    t = jnp.exp(v2 - v1)
    w1 = p_grp / (1.0 + t)
    w2 = p_grp * t / (1.0 + t)
    first_lower = i1 < i2
    e_lo = jnp.where(first_lower, i1, i2) - N_GROUPS
    e_hi = jnp.where(first_lower, i2, i1) - N_GROUPS
    w_lo = jnp.where(first_lower, w1, w2)
    w_hi = jnp.where(first_lower, w2, w1)
    a = e_lo - EXPERTS_PER_GROUP * grp
    b = e_hi - EXPERTS_PER_GROUP * grp
    pair = a * (2 * EXPERTS_PER_GROUP - 1 - a) * 0.5 + (b - a - 1.0)
    pair = jnp.where(pair == 3.0, 4.0, jnp.where(pair == 4.0, 3.0, pair))
    bucket = grp * N_PAIRS + pair
    return jnp.where(lane == 0, e_lo, jnp.where(lane == 1, e_hi, jnp.where(lane == 2, w_lo,
                     jnp.where(lane == 3, w_hi, jnp.where(lane == 4, bucket, 0.0)))))


def _out_kernel(alpha, oa, ob, oc, od, x_ref, g1_ref, sc2_ref, sh2_ref, w_ref, lng_ref, lnb_ref,
                wrh_ref, wrl_ref, br_ref, *rest):
    x1_ref, h2_ref, ids_ref = rest[-3:]
    mix = (_dot(oa[...], w_ref[0:256, :]) + _dot(ob[...], w_ref[256:512, :])
           + _dot(oc[...], w_ref[512:768, :]) + _dot(od[...], w_ref[768:1024, :]))
    x1 = _ln(alpha * x_ref[...] + g1_ref[0] * mix) * lng_ref[...] + lnb_ref[...]
    x1_ref[...] = x1
    h2 = _ln(x1) * (1.0 + sc2_ref[0]) + sh2_ref[0]
    hh, hl = _split(h2)
    logits = _dot(hl, wrh_ref[...]) + _dot(hh, wrl_ref[...]) + _dot(hh, wrh_ref[...]) + br_ref[...]
    rh, rl = _split(_route(logits))
    d = h2.shape[1]
    h2_ref[:, 0:d] = hh
    h2_ref[:, d:d + LANES] = rh
    h2_ref[:, d + LANES:d + 2 * LANES] = rl
    pick = ((lax.broadcasted_iota(jnp.int32, (8, LANES), 0) == 0)
            & (lax.broadcasted_iota(jnp.int32, (8, LANES), 1) == 4)).astype(BF16)
    ids_ref[...] = _dot_nt(pick, rh)


def _out_proj(alpha, outs, x2d, g1, sc2, sh2, lw, tokens_per_batch, n_rows, row0, prev=None):
    n, d = x2d.shape
    tm = min(OUT_TILE, tokens_per_batch)
    per_b = tokens_per_batch // tm
    row = lambda i: (i, 0)
    full = lambda i: (0, 0)
    perb = lambda i: (i // per_b, 0, 0)
    in_specs = [pl.BlockSpec((tm, 256), row)] * 4 + [
        pl.BlockSpec((tm, d), row),
        pl.BlockSpec((1, 1, d), perb), pl.BlockSpec((1, 1, d), perb), pl.BlockSpec((1, 1, d), perb),
        pl.BlockSpec((d, d), full), pl.BlockSpec((1, d), full), pl.BlockSpec((1, d), full),
        pl.BlockSpec((d, LANES), full), pl.BlockSpec((d, LANES), full), pl.BlockSpec((1, LANES), full)]
    args = [*outs, x2d, g1, sc2, sh2, lw["w_out"], lw["ln1_g"], lw["ln1_b"], lw["wr_hi"], lw["wr_lo"], lw["br"]]
    aliases = {}
    if prev is not None:
        for k, a in enumerate(prev):
            aliases[len(args)] = 1 + k
            in_specs.append(pl.BlockSpec(memory_space=pl.ANY))
            args.append(a)
    return pl.pallas_call(
        functools.partial(_out_kernel, alpha),
        grid=(n // tm,),
        in_specs=in_specs,
        out_specs=[pl.BlockSpec((tm, d), row), pl.BlockSpec((tm, d + H2_EXT), lambda i: (i + row0 // tm, 0)),
                   pl.BlockSpec((8, tm), lambda i: (0, i + row0 // tm))],
        out_shape=[jax.ShapeDtypeStruct((n, d), F32), jax.ShapeDtypeStruct((n_rows, d + H2_EXT), BF16),
                   jax.ShapeDtypeStruct((8, n_rows), F32)],
        input_output_aliases=aliases,
        compiler_params=_cparams("parallel"),
    )(*args)


def _moe_kernel(tile0, e0_ref, e1_ref, valid_ref, xs_ref, wg0, wg1, wu0, wu1, wd0, wd1, *rest):
    o_ref, wb = rest[-7], rest[-6:]
    t = pl.program_id(0) + tile0
    d = o_ref.shape[1]
    prev = jnp.maximum(t - 1, 0)
    first = pl.program_id(0) == 0

    @pl.when(first | (e0_ref[t] != e0_ref[prev]))
    def _():
        for src, dst in zip((wg0, wu0, wd0), wb[0::2]):
            dst[...] = src[0, 0].astype(BF16)

    @pl.when(first | (e1_ref[t] != e1_ref[prev]))
    def _():
        for src, dst in zip((wg1, wu1, wd1), wb[1::2]):
            dst[...] = src[0, 0].astype(BF16)

    @pl.when(valid_ref[t] == 1)
    def _():
        x = xs_ref[:, 0:d]
        rt = xs_ref[:, d:d + LANES].astype(F32) + xs_ref[:, d + LANES:d + 2 * LANES].astype(F32)
        lower_first = e0_ref[t] < e1_ref[t]
        w0 = jnp.where(lower_first, rt[:, 2:3], rt[:, 3:4])
        w1 = jnp.where(lower_first, rt[:, 3:4], rt[:, 2:3])

        def expert(wg, wu, wd):
            a = _silu(_dot(x, wg[...])) * _dot(x, wu[...])
            return _dot(a.astype(BF16), wd[...])

        y = w0 * expert(wb[0], wb[2], wb[4]) + w1 * expert(wb[1], wb[3], wb[5])
        o_ref[...] = y.astype(o_ref.dtype)

    @pl.when(valid_ref[t] == 0)
    def _():
        o_ref[...] = jnp.zeros_like(o_ref)


def _moe_experts(layer, tile0, e0, e1, valid, xs, wg, wu, wd, ys_prev):
    p, dx = xs.shape
    d = dx - H2_EXT
    tm = MOE_TILE
    de = wg.shape[3]
    w0 = lambda t, e0, e1, v: (layer, e0[t + tile0], 0, 0)
    w1 = lambda t, e0, e1, v: (layer, e1[t + tile0], 0, 0)
    in_specs = [pl.BlockSpec((tm, dx), lambda t, e0, e1, v: (t, 0)),
                pl.BlockSpec((1, 1, d, de), w0), pl.BlockSpec((1, 1, d, de), w1),
                pl.BlockSpec((1, 1, d, de), w0), pl.BlockSpec((1, 1, d, de), w1),
                pl.BlockSpec((1, 1, de, d), w0), pl.BlockSpec((1, 1, de, d), w1)]
    args = [e0, e1, valid, xs, wg, wg, wu, wu, wd, wd]
    aliases = {}
    if ys_prev is not None:
        aliases[len(args)] = 0
        in_specs.append(pl.BlockSpec(memory_space=pl.ANY))
        args.append(ys_prev)
    return pl.pallas_call(
        functools.partial(_moe_kernel, tile0),
        grid_spec=pltpu.PrefetchScalarGridSpec(
            num_scalar_prefetch=3,
            grid=(p // tm,),
            in_specs=in_specs,
            out_specs=pl.BlockSpec((tm, d), lambda t, e0, e1, v: (t + tile0, 0)),
            scratch_shapes=[pltpu.VMEM((d, de), BF16)] * 4 + [pltpu.VMEM((de, d), BF16)] * 2),
        out_shape=jax.ShapeDtypeStruct((e0.shape[0] * tm, d), BF16),
        input_output_aliases=aliases,
        compiler_params=_cparams("arbitrary"),
    )(*args)


_PAIR_TABLE = ((0, 1), (0, 2), (0, 3), (1, 3), (1, 2), (3, 2))


def _moe(layer, h2x, ids, wg, wu, wd, chunks, segments):
    n = h2x.shape[0]
    tm = MOE_TILE
    n_tiles = -(-(n // tm + N_BUCKETS) // chunks) * chunks
    bucket = ids[0].astype(jnp.int32)
    bids = jnp.arange(N_BUCKETS + 1, dtype=jnp.int32)
    onehot_b = bucket[None, :] == bids[:N_BUCKETS, None]
    per_token = lambda table: jnp.sum(jnp.where(onehot_b, table[:, None], 0), axis=0)

    seg = []
    within_tok = []
    start = 0
    for size in segments:
        tok = jnp.arange(size, dtype=jnp.int32)
        sorted_bucket, order = lax.sort((bucket[start:start + size], tok), num_keys=1)
        _, rank = lax.sort((order, tok), num_keys=1)
        edges = jnp.sum(sorted_bucket[None, :] < bids[:, None], axis=1, dtype=jnp.int32)
        seg.append((start, order, edges[:-1], edges[1:] - edges[:-1]))
        within_tok.append(rank)
        start += size
    counts = sum(c for _, _, _, c in seg)
    tiles_per = (counts + tm - 1) // tm
    tile_end = jnp.cumsum(tiles_per)
    tile_start = tile_end - tiles_per

    tile_ids = jnp.arange(n_tiles, dtype=jnp.int32)
    valid = (tile_ids < tile_end[-1]).astype(jnp.int32)
    tile_bucket = jnp.sum(tile_ids[:, None] >= tile_end[None, :], axis=1, dtype=jnp.int32)
    last_bucket = jnp.sum(tile_end[-1] - 1 >= tile_end, dtype=jnp.int32)
    tile_bucket = jnp.minimum(jnp.where(valid == 1, tile_bucket, last_bucket), N_BUCKETS - 1)
    onehot_tb = tile_bucket[:, None] == bids[None, :N_BUCKETS]
    pick = lambda table: jnp.sum(jnp.where(onehot_tb, table[None, :], 0), axis=1)
    pairs = jnp.asarray(_PAIR_TABLE, dtype=jnp.int32)
    tg = tile_bucket // N_PAIRS
    e0 = tg * EXPERTS_PER_GROUP + pairs[tile_bucket % N_PAIRS, 0]
    e1 = tg * EXPERTS_PER_GROUP + pairs[tile_bucket % N_PAIRS, 1]

    first_row = (tile_ids - pick(tile_start)) * tm
    within = first_row[:, None] + jnp.arange(tm, dtype=jnp.int32)[None, :]
    row_ok = (within < pick(counts)[:, None]) & (valid[:, None] == 1)
    src = jnp.zeros(within.shape, jnp.int32)
    before = jnp.zeros((N_BUCKETS,), jnp.int32)
    base = []
    for (start, order, off, cnt), rank in zip(seg, within_tok):
        j = within - pick(before)[:, None]
        mine = (j >= 0) & (j < pick(cnt)[:, None])
        idx = jnp.clip(pick(off)[:, None] + j, 0, order.shape[0] - 1)
        picked = order.at[idx.reshape(-1)].get(mode="promise_in_bounds").reshape(idx.shape)
        src = jnp.where(mine, start + picked, src)
        base.append((tile_start * tm + before - off, start, rank))
        before = before + cnt
    src = jnp.where(row_ok, src, 0).reshape(-1)

    per_chunk = n_tiles // chunks
    ys = None
    for c in range(chunks):
        xs = h2x.at[src[c * per_chunk * tm:(c + 1) * per_chunk * tm]].get(mode="promise_in_bounds")
        ys = _moe_experts(layer, c * per_chunk, e0, e1, valid, xs, wg, wu, wd, ys)

    pos = jnp.concatenate([per_token(table)[start:start + rank.shape[0]] + rank for table, start, rank in base])
    return ys.at[pos].get(mode="promise_in_bounds")


def _fin_kernel(alpha, x_ref, y_ref, g2_ref, lng_ref, lnb_ref, o_ref):
    o_ref[...] = _ln(alpha * x_ref[...] + g2_ref[0] * y_ref[...]) * lng_ref[...] + lnb_ref[...]


def _finish(alpha, x1, y, g2, ln_g, ln_b, tokens_per_batch):
    n, d = x1.shape
    tm = min(ROW_TILE, tokens_per_batch)
    per_b = tokens_per_batch // tm
    row = lambda i: (i, 0)
    full = lambda i: (0, 0)
    return pl.pallas_call(
        functools.partial(_fin_kernel, alpha),
        grid=(n // tm,),
        in_specs=[pl.BlockSpec((tm, d), row), pl.BlockSpec((tm, d), row),
                  pl.BlockSpec((1, 1, d), lambda i: (i // per_b, 0, 0)),
                  pl.BlockSpec((1, d), full), pl.BlockSpec((1, d), full)],
        out_specs=pl.BlockSpec((tm, d), row),
        out_shape=jax.ShapeDtypeStruct((n, d), F32),
        compiler_params=_cparams("parallel"),
    )(x1, y, g2, ln_g, ln_b)


def _rope_tables(t):
    pos = jnp.arange(t)
    row = (pos // GRID_W).astype(F32)[:, None]
    col = (pos % GRID_W).astype(F32)[:, None]
    lane = jnp.arange(LANES)

    def build(d, half, active):
        axis_col = (d // half) % 2 == 1
        w = d % half
        first = w < half // 2
        f = (w % (half // 2)).astype(F32)
        freq = ROPE_THETA ** (-(2.0 * f) / half)
        ang = jnp.where(axis_col[None, :], col, row) * freq[None, :]
        cos = jnp.where(active[None, :], jnp.cos(ang), 1.0)
        sin = jnp.where(active[None, :], jnp.sin(ang), 0.0)
        return jnp.stack([cos, jnp.where(first[None, :], 0.0, sin), jnp.where(first[None, :], -sin, 0.0)])

    t64 = build(lane % HEAD_DIM, HEAD_DIM // 2, lane >= 0)
    in_rope = (lane >= MLA_NOPE) & (lane < MLA_QK)
    tq = build(jnp.where(in_rope, lane - MLA_NOPE, 0), MLA_ROPE // 2, in_rope)
    in_kr = lane < MLA_ROPE
    tk = build(jnp.where(in_kr, lane, 0), MLA_ROPE // 2, in_kr)
    return t64, tq, tk


def _nbr_bias_tables(na_bias, rows):
    heads = na_bias.shape[0]
    npair = rows // 2
    kr = min(NA_ROWS, rows)
    qcol = jnp.arange(GRID_W)[:, None]
    kcol = jnp.arange(GRID_W)[None, :]
    c0 = jnp.clip(qcol - NA_COLS // 2, 0, GRID_W - NA_COLS)
    col_ok = (kcol >= c0) & (kcol < c0 + NA_COLS)
    col_off = kcol - qcol + NA_COLS - 1
    onehot = (col_off[None] == jnp.arange(2 * NA_COLS - 1)[:, None, None]).astype(F32)
    colmat = jnp.einsum("hdc,cqk->hdqk", na_bias, onehot, precision=lax.Precision.HIGHEST) * LOG2E
    colmat = jnp.where(col_ok[None, None], colmat, NEG)
    outside = jnp.full((heads, GRID_W, GRID_W), NEG, F32)
    tabs = []
    for p in (0, 1, 2, npair - 2, npair - 1):
        ks = min(max(2 * p - NA_ROWS // 2, 0), rows - NBR_SPAN_ROWS)
        per_qrow = []
        for qr in range(2):
            r = 2 * p + qr
            r0 = min(max(r - kr // 2, 0), rows - kr)
            blocks = []
            for kri in range(NBR_SPAN_ROWS):
                krow = ks + kri
                blocks.append(colmat[:, krow - r + NA_ROWS - 1] if r0 <= krow < r0 + kr else outside)
            per_qrow.append(jnp.concatenate(blocks, axis=-1))
        tab = jnp.concatenate(per_qrow, axis=1)
        tabs.append(tab.reshape(heads // 2, 4 * GRID_W, NBR_SPAN_ROWS * GRID_W))
    return jnp.stack(tabs)


def _layer_weights(l, w_in, gqa_q_gain, gqa_k_gain, mla_q_gain, mla_w_qb, mla_kv_gain, mla_w_kvb,
                   w_out, ln1_g, ln1_b, router_group_w, router_group_b, router_expert_w, router_expert_b):
    d = w_in.shape[1]
    d_proj = w_in.shape[2]
    lw = {}
    lw["w_in"] = jnp.pad(w_in[l], ((0, 0), (0, 2304 - d_proj))).astype(BF16)
    lw["gq"] = jnp.tile(gqa_q_gain[l], 4)[None, :]
    lw["gk"] = jnp.tile(gqa_k_gain[l], 2)[None, :]
    lw["gmq"] = mla_q_gain[l][None, :]
    lw["gmkv"] = mla_kv_gain[l][None, :]
    wqb = mla_w_qb[l].reshape(-1, 4, MLA_QK)
    lw["w_qb"] = jnp.pad(wqb, ((0, 0), (0, 0), (0, LANES - MLA_QK))).reshape(-1, 4 * LANES).astype(BF16)
    wkvb = mla_w_kvb[l].reshape(-1, 4, 2 * HEAD_DIM)
    k_part = jnp.pad(wkvb[:, :, :MLA_NOPE], ((0, 0), (0, 0), (0, LANES - MLA_NOPE))).reshape(-1, 4 * LANES)
    v_part = jnp.pad(wkvb[:, :, MLA_NOPE:], ((0, 0), (0, 0), (0, LANES - HEAD_DIM))).reshape(-1, 4 * LANES)
    lw["w_kvb"] = jnp.concatenate([k_part, v_part], axis=1).astype(BF16)
    idx = jnp.arange(256) // HEAD_DIM
    lw["bd"] = (idx[:, None] == idx[None, :]).astype(BF16)
    lw["w_out"] = w_out[l].astype(BF16)
    lw["ln1_g"] = ln1_g[l][None, :]
    lw["ln1_b"] = ln1_b[l][None, :]
    wr = jnp.concatenate([router_group_w[l], router_expert_w[l]], axis=1)
    wr = jnp.pad(wr, ((0, 0), (0, LANES - wr.shape[1])))
    lw["wr_hi"] = wr.astype(BF16)
    lw["wr_lo"] = (wr - lw["wr_hi"].astype(F32)).astype(BF16)
    br = jnp.concatenate([router_group_b[l], router_expert_b[l]])
    lw["br"] = jnp.pad(br, (0, LANES - br.shape[0]))[None, :]
    return lw


def kernel(x, c, ctx, c_ctx, w_ada, b_ada, w_in, na_bias, gqa_q_gain, gqa_k_gain, win_sink, mla_q_gain, mla_w_qb, mla_kv_gain, mla_w_kvb, w_out, ln1_g, ln1_b, router_group_w, router_group_b, router_expert_w, router_expert_b, moe_w_gate, moe_w_up, moe_w_down, ln2_g, ln2_b):
    bsz, t, d = x.shape
    tc = ctx.shape[1]
    depth = w_ada.shape[0]
    rows = t // GRID_W
    assert t % (2 * GRID_W) == 0 and rows >= NBR_SPAN_ROWS and rows // 2 >= 5
    assert t >= BLOCK + 2 * WINDOW and t % ROW_TILE == 0 and tc % (8 * ROW_SPLIT) == 0 and t % tc == 0
    assert tc <= ROW_TILE and t % OUT_TILE == 0
    assert t % (LOCAL_NSUB * BLOCK) == 0 and t % GLOBAL_BQ == 0
    s_tot = t + tc
    alpha = (2.0 * depth) ** 0.25

    ada_rows = -(-(bsz + 1) // 8) * 8
    cs = jnp.zeros((ada_rows, d), F32).at[:bsz].set(c).at[bsz].set(c_ctx)
    mod = _ada(cs, w_ada, b_ada)
    tabs = _rope_tables(t)

    xl = x.reshape(bsz * t, d)
    xc = ctx.reshape(bsz * tc, d)
    fin_l = fin_c = None

    for l in range(depth):
        with_ctx = l < depth - 1
        lw = _layer_weights(l, w_in, gqa_q_gain, gqa_k_gain, mla_q_gain, mla_w_qb, mla_kv_gain,
                            mla_w_kvb, w_out, ln1_g, ln1_b, router_group_w, router_group_b,
                            router_expert_w, router_expert_b)
        ln2g, ln2b = ln2_g[l][None, :], ln2_b[l][None, :]

        def mods(lo, hi):
            parts = jnp.split(mod[l, lo:hi], 6, axis=-1)
            return [jnp.broadcast_to(p[:, None, :], (bsz, 1, d)) for p in parts]
        sh1, sc1, g1, sh2, sc2, g2 = mods(0, bsz)
        sh1c, sc1c, g1c, sh2c, sc2c, g2c = mods(bsz, bsz + 1)

        xl, pl_lat = _project(xl, fin_l, sc1, sh1, lw, tabs, t, s_tot, 0)
        xc, pl_ctx = _project(xc, fin_c, sc1c, sh1c, lw, None, tc, s_tot, t,
                              kv_prev=[pl_lat[i] for i in _KV_OUTS])
        qa, qb, qc, qd = [pl_lat[i].reshape(bsz, t, -1) for i in _Q_OUTS]
        qa_c, qb_c, qc_c, qd_c = [pl_ctx[i].reshape(bsz, tc, -1) for i in _Q_OUTS]
        ka, va, kb, vb, kc, vc, kd, vd = [pl_ctx[i] for i in _KV_OUTS]

        bias_tab = _nbr_bias_tables(na_bias[l], rows)
        everything = dict(parts=_key_ranges(s_tot), bq=GLOBAL_BQ, single_shot=False)
        out_a = _attention("mha", qa, ka, va, [("nbr", 0, t), ("full", t, tc)], bias=bias_tab, nsub=LOCAL_NSUB)
        out_b = _attention("gqa", qb, kb, vb, **everything)
        out_c = _attention("gqa", qc, kc, vc, [("win", 0, t), ("full", t, tc)], sink=win_sink[l],
                           nsub=LOCAL_NSUB)
        out_d = _attention("mla", qd, kd, vd, **everything)
        outs = [o.reshape(bsz * t, 256) for o in (out_a, out_b, out_c, out_d)]

        n_lat, n_ctx = bsz * t, bsz * tc
        n_moe = n_lat + n_ctx if with_ctx else n_lat
        x1, h2x, ids = _out_proj(alpha, outs, xl, g1, sc2, sh2, lw, t, n_moe, 0)
        if with_ctx:
            ctx_only = dict(parts=[("full", 0, tc)], kv_block=(tc, t // tc))
            outs_c = [
                _attention("mha", qa_c, ka, va, **ctx_only),
                _attention("gqa", qb_c, kb, vb, **ctx_only),
                _attention("gqa", qc_c, kc, vc, sink=win_sink[l], **ctx_only),
                _attention("mla", qd_c, kd, vd, **ctx_only),
            ]
            outs_c = [o.reshape(n_ctx, 256) for o in outs_c]
            x1c, h2x, ids = _out_proj(alpha, outs_c, xc, g1c, sc2c, sh2c, lw, tc, n_moe, n_lat, prev=(h2x, ids))
        y = _moe(l, h2x, ids, moe_w_gate, moe_w_up, moe_w_down, MOE_CHUNKS,
                 (n_lat, n_ctx) if with_ctx else (n_lat,))
        fin_l = (alpha, x1, y, g2, ln2g, ln2b, 0)
        if with_ctx:
            fin_c = (alpha, x1c, y, g2c, ln2g, ln2b, n_lat)

    return _finish(*fin_l[:6], t).reshape(bsz, t, d)
```

```python
import functools

import jax
import jax.numpy as jnp
from jax import lax
from jax.experimental import pallas as pl
from jax.experimental.pallas import tpu as pltpu

F32 = jnp.float32
BF16 = jnp.bfloat16

GRID_W = 64
HEAD_DIM = 64
BLOCK = 128
WINDOW = 128
ROPE_THETA = 10000.0
NEG = -1e30
EPS = 1e-6
LOG2E = 1.4426950408889634

NA_ROWS = 8
NA_COLS = 16
NBR_SPAN_ROWS = 10
MLA_NOPE = 64
MLA_ROPE = 32
MLA_QK = MLA_NOPE + MLA_ROPE
N_GROUPS = 4
EXPERTS_PER_GROUP = 4
N_EXPERTS = 16
N_PAIRS = 6
N_BUCKETS = N_GROUPS * N_PAIRS

LANES = 128
VMEM_LIMIT = 48 * 1024 * 1024

ROW_TILE = 512
ROW_SPLIT = 2
OUT_TILE = 1024
MOE_TILE = 256
MOE_CHUNKS = 4
KV_CHUNK = 2560
GLOBAL_BQ = 256
LOCAL_NSUB = 8
H2_EXT = 256


def _cparams(*sem):
    return pltpu.CompilerParams(dimension_semantics=sem, vmem_limit_bytes=VMEM_LIMIT)


def _dot(a, b):
    return jnp.dot(a, b, preferred_element_type=F32)


def _dot_nt(a, b):
    return lax.dot_general(a, b, (((1,), (1,)), ((), ())), preferred_element_type=F32)


def _split(a):
    hi = a.astype(BF16)
    lo = (a - hi.astype(F32)).astype(BF16)
    return hi, lo


def _dot3(a, b):
    ah, al = _split(a)
    bh, bl = _split(b)
    return _dot(al, bh) + _dot(ah, bl) + _dot(ah, bh)


def _ln(x):
    mu = jnp.mean(x, axis=-1, keepdims=True)
    xc = x - mu
    var = jnp.mean(xc * xc, axis=-1, keepdims=True)
    return xc * lax.rsqrt(var + EPS)


def _silu(g):
    return g / (1.0 + jnp.exp(-g))


def _ada_kernel(c_ref, w_ref, b_ref, o_ref):
    o_ref[0] = _dot3(_silu(c_ref[...]), w_ref[0]) + b_ref[0]


def _ada(cs, w_ada, b_ada):
    depth, d, n = w_ada.shape
    rows = cs.shape[0]
    tn = 1536
    return pl.pallas_call(
        _ada_kernel,
        grid=(depth, n // tn),
        in_specs=[pl.BlockSpec((rows, d), lambda l, j: (0, 0)),
                  pl.BlockSpec((1, d, tn), lambda l, j: (l, 0, j)),
                  pl.BlockSpec((1, 1, tn), lambda l, j: (l, 0, j))],
        out_specs=pl.BlockSpec((1, rows, tn), lambda l, j: (l, 0, j)),
        out_shape=jax.ShapeDtypeStruct((depth, rows, n), F32),
        compiler_params=_cparams("parallel", "parallel"),
    )(cs, w_ada, b_ada.reshape(depth, 1, n))


def _head_rms(x, bd, gain):
    hi, lo = _split(x * x)
    ss = _dot(lo, bd) + _dot(hi, bd)
    return x * lax.rsqrt(ss * (1.0 / HEAD_DIM) + EPS) * gain


def _rms(x, gain):
    return x * lax.rsqrt(jnp.mean(x * x, axis=-1, keepdims=True) + EPS) * gain


def _rope(x, tab_ref, shift):
    outs = []
    for j in range(x.shape[1] // LANES):
        xb = x[:, j * LANES:(j + 1) * LANES]
        outs.append(xb * tab_ref[0]
                    + pltpu.roll(xb, shift, 1) * tab_ref[1]
                    + pltpu.roll(xb, LANES - shift, 1) * tab_ref[2])
    return outs[0] if len(outs) == 1 else jnp.concatenate(outs, axis=1)


def _value_tiles(v):
    lane = lax.broadcasted_iota(jnp.int32, (v.shape[0], LANES), 1)
    fill = jnp.where(lane == HEAD_DIM, 1.0, 0.0)
    outs = []
    for j in range(v.shape[1] // LANES):
        blk = v[:, j * LANES:(j + 1) * LANES]
        outs.append(jnp.where(lane < HEAD_DIM, blk, fill))
        outs.append(jnp.where(lane < HEAD_DIM, pltpu.roll(blk, HEAD_DIM, 1), fill))
    return jnp.concatenate(outs, axis=1).astype(BF16)


def _proj_kernel(rope, fin_alpha, n_aliased, *refs):
    refs = list(refs)
    if fin_alpha is None:
        x_ref = refs.pop(0)
    else:
        x1_ref, y_ref, g2_ref, lng_ref, lnb_ref = refs[:5]
        refs = refs[5:]
    (sc_ref, sh_ref, w_ref, gq_ref, gk_ref, gmq_ref, gmkv_ref,
     wqb_ref, wkvb_ref, bd_ref) = refs[:10]
    refs = refs[10:]
    tabs = None
    if rope:
        tabs = refs[:3]
        refs = refs[3:]
    refs = refs[n_aliased:]
    if fin_alpha is not None:
        xo_ref = refs.pop(0)
    refs = [r.at[0] if len(r.shape) == 3 else r for r in refs]
    group = refs[0].shape[0] // ROW_SPLIT
    for u in range(ROW_SPLIT):
        rows = slice(u * group, (u + 1) * group)
        if fin_alpha is None:
            x = x_ref[rows, :]
        else:
            x = (_ln(fin_alpha * x1_ref[rows, :] + g2_ref[0] * y_ref[rows, :]) * lng_ref[...]
                 + lnb_ref[...])
            xo_ref[rows, :] = x
        _proj_rows(rope, x, rows, sc_ref, sh_ref, w_ref, gq_ref, gk_ref, gmq_ref, gmkv_ref, wqb_ref,
                   wkvb_ref, bd_ref, tabs, refs)


def _proj_rows(rope, x, rows, sc_ref, sh_ref, w_ref, gq_ref, gk_ref, gmq_ref, gmkv_ref, wqb_ref, wkvb_ref,
               bd_ref, tabs, outs):
    qa, ka, va, qb, kb, vb, qc, kc, vc, qd, kd, vd = outs
    if rope:
        t64_ref, tq_ref, tk_ref = [t.at[:, rows, :] for t in tabs]
    h = _ln(x) * (1.0 + sc_ref[0]) + sh_ref[0]
    hb = h.astype(BF16)
    sq = HEAD_DIM ** -0.5 * LOG2E

    def proj(a, b):
        return _dot(hb, w_ref[:, a:b])

    pa = proj(0, 768)
    qa[rows, :] = (pa[:, 0:256] * sq).astype(BF16)
    ka[rows, :] = pa[:, 256:512].astype(BF16)
    va[rows, :] = _value_tiles(pa[:, 512:768])

    pb = proj(768, 1280)
    q = _head_rms(pb[:, 0:256], bd_ref[...], gq_ref[...])
    k = _head_rms(pb[:, 256:384], bd_ref[0:128, 0:128], gk_ref[...])
    if rope:
        q = _rope(q, t64_ref, 16)
        k = _rope(k, t64_ref, 16)
    qb[rows, :] = (q * sq).astype(BF16)
    kb[rows, :] = k.astype(BF16)
    vb[rows, :] = _value_tiles(pb[:, 384:512])

    pc = proj(1280, 1792)
    q = pc[:, 0:256]
    k = pc[:, 256:384]
    if rope:
        q = _rope(q, t64_ref, 16)
        k = _rope(k, t64_ref, 16)
    qc[rows, :] = (q * sq).astype(BF16)
    kc[rows, :] = k.astype(BF16)
    vc[rows, :] = _value_tiles(pc[:, 384:512])

    pd = proj(1792, 2304)
    cq = _rms(pd[:, 0:256], gmq_ref[...])
    q = _dot(cq.astype(BF16), wqb_ref[...])
    if rope:
        q = _rope(q, tq_ref, 8)
    qd[rows, :] = (q * (MLA_QK ** -0.5 * LOG2E)).astype(BF16)
    ckv = _rms(pd[:, 256:384], gmkv_ref[...])
    kvu = _dot(ckv.astype(BF16), wkvb_ref[...])
    kr = pd[:, 384:512]
    if rope:
        kr = _rope(kr, tk_ref, 8)
    kr = pltpu.roll(kr, MLA_NOPE, 1)
    kd[rows, :] = jnp.concatenate(
        [kvu[:, j * LANES:(j + 1) * LANES] + kr for j in range(4)], axis=1).astype(BF16)
    lane = lax.broadcasted_iota(jnp.int32, (kr.shape[0], LANES), 1)
    fill = jnp.where(lane == HEAD_DIM, 1.0, 0.0)
    vd[rows, :] = jnp.concatenate(
        [kvu[:, (4 + j) * LANES:(5 + j) * LANES] + fill for j in range(4)], axis=1).astype(BF16)


_PROJ_WIDTHS = (256, 256, 512, 256, 128, 256, 256, 128, 256, 512, 512, 512)


_Q_OUTS = (0, 3, 6, 9)
_KV_OUTS = (1, 2, 4, 5, 7, 8, 10, 11)


def _project(x2d, fin, sc, sh, lw, tabs, tokens_per_batch, kv_rows, kv_row0, kv_prev=None):
    n, d = (x2d if fin is None else fin[1]).shape
    tm = min(ROW_TILE, tokens_per_batch)
    per_b = tokens_per_batch // tm
    bsz = n // tokens_per_batch
    rope = tabs is not None
    row = lambda i: (i, 0)
    full = lambda i: (0, 0)
    perb = lambda i: (i // per_b, 0, 0)
    kvrow = lambda i: (i // per_b, kv_row0 // tm + i % per_b, 0)
    if fin is None:
        in_specs = [pl.BlockSpec((tm, d), row)]
        args = [x2d]
    else:
        in_specs = [pl.BlockSpec((tm, d), row), pl.BlockSpec((tm, d), lambda i: (i + fin[6] // tm, 0)),
                    pl.BlockSpec((1, 1, d), perb), pl.BlockSpec((1, d), full), pl.BlockSpec((1, d), full)]
        args = list(fin[1:6])
    in_specs += [pl.BlockSpec((1, 1, d), perb), pl.BlockSpec((1, 1, d), perb),
                 pl.BlockSpec(lw["w_in"].shape, full),
                 pl.BlockSpec((1, 256), full), pl.BlockSpec((1, 128), full),
                 pl.BlockSpec((1, 256), full), pl.BlockSpec((1, 128), full),
                 pl.BlockSpec(lw["w_qb"].shape, full), pl.BlockSpec(lw["w_kvb"].shape, full),
                 pl.BlockSpec((256, 256), full)]
    args += [sc, sh, lw["w_in"], lw["gq"], lw["gk"], lw["gmq"], lw["gmkv"],
             lw["w_qb"], lw["w_kvb"], lw["bd"]]
    if rope:
        tab = lambda i: (0, i % per_b, 0)
        in_specs += [pl.BlockSpec((3, tm, LANES), tab)] * 3
        args += list(tabs)
    out_specs = [pl.BlockSpec((1, tm, w), kvrow) if i in _KV_OUTS else pl.BlockSpec((tm, w), row)
                 for i, w in enumerate(_PROJ_WIDTHS)]
    out_shape = [jax.ShapeDtypeStruct((bsz, kv_rows, w) if i in _KV_OUTS else (n, w), BF16)
                 for i, w in enumerate(_PROJ_WIDTHS)]
    lead = 0 if fin is None else 1
    if fin is not None:
        out_specs.insert(0, pl.BlockSpec((tm, d), row))
        out_shape.insert(0, jax.ShapeDtypeStruct((n, d), F32))
    aliases = {}
    if kv_prev is not None:
        for a, i in zip(kv_prev, _KV_OUTS):
            aliases[len(args)] = lead + i
            in_specs.append(pl.BlockSpec(memory_space=pl.ANY))
            args.append(a)
    outs = pl.pallas_call(
        functools.partial(_proj_kernel, rope, None if fin is None else fin[0], len(aliases)),
        grid=(n // tm,),
        in_specs=in_specs,
        out_specs=out_specs,
        out_shape=out_shape,
        input_output_aliases=aliases,
        compiler_params=_cparams("parallel"),
    )(*args)
    return (x2d, list(outs)) if fin is None else (outs[0], list(outs[1:]))


def _softmax_step(s, vts, bq, m, acc):
    m_new = jnp.maximum(m, jnp.max(s, axis=1, keepdims=True))
    alpha = jnp.exp2(m - m_new)
    p = jnp.exp2(s - m_new).astype(BF16)
    if len(vts) == 1:
        pv = _dot(p, vts[0])
    else:
        pv = jnp.concatenate([_dot(p[:bq], vts[0]), _dot(p[bq:], vts[1])], axis=0)
    return [m_new, alpha * acc + pv]


def _attn_kernel(mode, bq, nsub, srcs, single_shot, has_bias, has_sink, *refs):
    refs = list(refs)
    q_ref = refs.pop(0)
    k_ref, v_ref = refs.pop(0), refs.pop(0)
    bias_refs = [refs.pop(0) for _ in range(nsub)] if has_bias else None
    sink_ref = refs.pop(0) if has_sink else None
    o_ref = refs.pop(0)

    for u in range(nsub):
        qi = pl.program_id(1) * nsub + u
        rows = slice(u * bq, (u + 1) * bq)
        rows2 = 2 * bq
        lo = lax.broadcasted_iota(jnp.int32, (bq, LANES), 1) < HEAD_DIM

        qss, kcs, vcs = [], [], []
        for j in range(2):
            if mode == "mla":
                qblk = q_ref[0, rows, 2 * LANES * j:2 * LANES * (j + 1)].astype(F32)
                first = lax.broadcasted_iota(jnp.int32, (bq, 2 * LANES), 1) < LANES
                qs = jnp.concatenate([jnp.where(first, qblk, 0.0), jnp.where(first, 0.0, qblk)], axis=0)
                kcs.append((2 * LANES * j, 2 * LANES * (j + 1)))
                vcs.append((2 * LANES * j, 2 * LANES * j + LANES))
            else:
                qblk = q_ref[0, rows, LANES * j:LANES * (j + 1)].astype(F32)
                if mode == "mha":
                    qe, qo = jnp.where(lo, qblk, 0.0), jnp.where(lo, 0.0, qblk)
                    kcs.append((LANES * j, LANES * (j + 1)))
                    vcs.append((2 * LANES * j, 2 * LANES * j + LANES))
                else:
                    rolled = pltpu.roll(qblk, HEAD_DIM, 1)
                    if j == 0:
                        qe, qo = jnp.where(lo, qblk, 0.0), jnp.where(lo, rolled, 0.0)
                    else:
                        qe, qo = jnp.where(lo, 0.0, rolled), jnp.where(lo, 0.0, qblk)
                    kcs.append((0, LANES))
                    vcs.append((LANES * j,))
                qs = jnp.concatenate([qe, qo], axis=0)
            qss.append(qs.astype(BF16))

        def pv(p, vts):
            if len(vts) == 1:
                return _dot(p, vts[0])
            return jnp.concatenate([_dot(p[:bq], vts[0]), _dot(p[bq:], vts[1])], axis=0)

        def tiles(j, off, n):
            kt = k_ref[0, pl.ds(off, n), kcs[j][0]:kcs[j][1]]
            return kt, [v_ref[0, pl.ds(off, n), v0:v0 + LANES] for v0 in vcs[j]]

        def block_scores(j, kind, row0, nrows):
            if kind == "full":
                kt, vts = tiles(j, row0, nrows)
                return _dot_nt(qss[j], kt), vts
            if kind == "win":
                span = bq + 2 * WINDOW
                start = pl.multiple_of(jnp.clip((qi - 1) * bq, 0, nrows - span), bq)
                r = lax.broadcasted_iota(jnp.int32, (rows2, span), 0)
                qpos = qi * bq + jnp.where(r >= bq, r - bq, r)
                kpos = start + lax.broadcasted_iota(jnp.int32, (rows2, span), 1)
                kt, vts = tiles(j, row0 + start, span)
                return jnp.where(jnp.abs(qpos - kpos) <= WINDOW, _dot_nt(qss[j], kt), NEG), vts
            span = NBR_SPAN_ROWS * GRID_W
            ks = jnp.clip(2 * qi - NA_ROWS // 2, 0, nrows // GRID_W - NBR_SPAN_ROWS)
            kt, vts = tiles(j, row0 + pl.multiple_of(ks * GRID_W, GRID_W), span)
            return _dot_nt(qss[j], kt) + bias_refs[u][0, j], vts

        if single_shot:
            state = []
            for j in range(2):
                parts = [block_scores(j, kind, row0, nrows) for kind, row0, nrows in srcs]
                m = functools.reduce(jnp.maximum, [jnp.max(sc, axis=1, keepdims=True) for sc, _ in parts])
                acc = functools.reduce(lambda a, b: a + b,
                                       [pv(jnp.exp2(sc - m).astype(BF16), vts) for sc, vts in parts])
                state += [m, acc]
        else:
            state = [jnp.full((rows2, 1), NEG, F32), jnp.zeros((rows2, LANES), F32)] * 2
            for kind, row0, nrows in srcs:
                for j in range(2):
                    sc, vts = block_scores(j, kind, row0, nrows)
                    state[2 * j:2 * j + 2] = _softmax_step(sc, vts, bq, state[2 * j], state[2 * j + 1])

        for j in range(2):
            m, acc = state[2 * j], state[2 * j + 1]
            l = acc[:, HEAD_DIM:HEAD_DIM + 1]
            if has_sink:
                sk = jnp.concatenate([jnp.full((bq, 1), sink_ref[2 * j] * LOG2E, F32),
                                      jnp.full((bq, 1), sink_ref[2 * j + 1] * LOG2E, F32)], axis=0)
                mf = jnp.maximum(m, sk)
                a = jnp.exp2(m - mf)
                l = l * a + jnp.exp2(sk - mf)
                acc = acc * a
            o = acc / l
            o_ref[0, rows, LANES * j:LANES * (j + 1)] = jnp.where(
                lo, o[:bq], pltpu.roll(o[bq:], HEAD_DIM, 1)).astype(BF16)


def _key_ranges(nrows):
    n = -(-nrows // KV_CHUNK)
    size = max(nrows // n // 256, 1) * 256
    starts = [i * size for i in range(n)]
    return [("full", r0, (nrows if i == n - 1 else r0 + size) - r0) for i, r0 in enumerate(starts)]


def _attention(mode, q, k, v, parts, kv_block=None, bias=None, sink=None, bq=BLOCK, nsub=1, single_shot=True):
    b, tq, wq = q.shape
    srcs = tuple(parts)
    rows, blk = (k.shape[1], 0) if kv_block is None else kv_block
    in_specs = [pl.BlockSpec((1, nsub * bq, wq), lambda bi, qi: (bi, qi, 0)),
                pl.BlockSpec((1, rows, k.shape[2]), lambda bi, qi: (bi, blk, 0)),
                pl.BlockSpec((1, rows, v.shape[2]), lambda bi, qi: (bi, blk, 0))]
    args = [q, k, v]
    if bias is not None:
        npair = tq // bq

        for u in range(nsub):
            def bias_idx(bi, qi, u=u):
                p = qi * nsub + u
                return (jnp.where(p < 2, p, jnp.where(p >= npair - 2, p - (npair - 2) + 3, 2)), 0, 0, 0)
            in_specs.append(pl.BlockSpec((1,) + bias.shape[1:], bias_idx))
            args.append(bias)
    if sink is not None:
        in_specs.append(pl.BlockSpec(memory_space=pltpu.SMEM))
        args.append(sink)
    return pl.pallas_call(
        functools.partial(_attn_kernel, mode, bq, nsub, srcs, single_shot, bias is not None, sink is not None),
        grid=(b, tq // (nsub * bq)),
        in_specs=in_specs,
        out_specs=pl.BlockSpec((1, nsub * bq, 256), lambda bi, qi: (bi, qi, 0)),
        out_shape=jax.ShapeDtypeStruct((b, tq, 256), BF16),
        compiler_params=_cparams("parallel", "arbitrary"),
    )(*args)


def _route(logits):
    lane = lax.broadcasted_iota(jnp.int32, logits.shape, 1).astype(F32)
    big = float(LANES)
    is_g = lane < N_GROUPS
    lg = jnp.where(is_g, logits, NEG)
    gmax = jnp.max(lg, axis=1, keepdims=True)
    grp = jnp.min(jnp.where(lg == gmax, lane, big), axis=1, keepdims=True)
    den = jnp.sum(jnp.where(is_g, jnp.exp(lg - gmax), 0.0), axis=1, keepdims=True)
    p_grp = 1.0 / den
    e0 = N_GROUPS + EXPERTS_PER_GROUP * grp
    in_grp = (lane >= e0) & (lane < e0 + EXPERTS_PER_GROUP)
    le = jnp.where(in_grp, logits, NEG)
    v1 = jnp.max(le, axis=1, keepdims=True)
    i1 = jnp.min(jnp.where(in_grp & (le == v1), lane, big), axis=1, keepdims=True)
    rest = in_grp & (lane != i1)
    le2 = jnp.where(rest, logits, NEG)
    v2 = jnp.max(le2, axis=1, keepdims=True)
    i2 = jnp.min(jnp.where(rest & (le2 == v2), lane, big), axis=1, keepdims=True)
    t = jnp.exp(v2 - v1)
    w1 = p_grp / (1.0 + t)
    w2 = p_grp * t / (1.0 + t)
    first_lower = i1 < i2
    e_lo = jnp.where(first_lower, i1, i2) - N_GROUPS
    e_hi = jnp.where(first_lower, i2, i1) - N_GROUPS
    w_lo = jnp.where(first_lower, w1, w2)
    w_hi = jnp.where(first_lower, w2, w1)
    a = e_lo - EXPERTS_PER_GROUP * grp
    b = e_hi - EXPERTS_PER_GROUP * grp
    pair = a * (2 * EXPERTS_PER_GROUP - 1 - a) * 0.5 + (b - a - 1.0)
    pair = jnp.where(pair == 3.0, 4.0, jnp.where(pair == 4.0, 3.0, pair))
    bucket = grp * N_PAIRS + pair
    return jnp.where(lane == 0, e_lo, jnp.where(lane == 1, e_hi, jnp.where(lane == 2, w_lo,
                     jnp.where(lane == 3, w_hi, jnp.where(lane == 4, bucket, 0.0)))))


def _out_kernel(alpha, oa, ob, oc, od, x_ref, g1_ref, sc2_ref, sh2_ref, w_ref, lng_ref, lnb_ref,
                wrh_ref, wrl_ref, br_ref, *rest):
    x1_ref, h2_ref, ids_ref = rest[-3:]
    mix = (_dot(oa[...], w_ref[0:256, :]) + _dot(ob[...], w_ref[256:512, :])
           + _dot(oc[...], w_ref[512:768, :]) + _dot(od[...], w_ref[768:1024, :]))
    x1 = _ln(alpha * x_ref[...] + g1_ref[0] * mix) * lng_ref[...] + lnb_ref[...]
    x1_ref[...] = x1
    h2 = _ln(x1) * (1.0 + sc2_ref[0]) + sh2_ref[0]
    hh, hl = _split(h2)
    logits = _dot(hl, wrh_ref[...]) + _dot(hh, wrl_ref[...]) + _dot(hh, wrh_ref[...]) + br_ref[...]
    rh, rl = _split(_route(logits))
    d = h2.shape[1]
    h2_ref[:, 0:d] = hh
    h2_ref[:, d:d + LANES] = rh
    h2_ref[:, d + LANES:d + 2 * LANES] = rl
    pick = ((lax.broadcasted_iota(jnp.int32, (8, LANES), 0) == 0)
            & (lax.broadcasted_iota(jnp.int32, (8, LANES), 1) == 4)).astype(BF16)
    ids_ref[...] = _dot_nt(pick, rh)


def _out_proj(alpha, outs, x2d, g1, sc2, sh2, lw, tokens_per_batch, n_rows, row0, prev=None):
    n, d = x2d.shape
    tm = min(OUT_TILE, tokens_per_batch)
    per_b = tokens_per_batch // tm
    row = lambda i: (i, 0)
    full = lambda i: (0, 0)
    perb = lambda i: (i // per_b, 0, 0)
    in_specs = [pl.BlockSpec((tm, 256), row)] * 4 + [
        pl.BlockSpec((tm, d), row),
        pl.BlockSpec((1, 1, d), perb), pl.BlockSpec((1, 1, d), perb), pl.BlockSpec((1, 1, d), perb),
        pl.BlockSpec((d, d), full), pl.BlockSpec((1, d), full), pl.BlockSpec((1, d), full),
        pl.BlockSpec((d, LANES), full), pl.BlockSpec((d, LANES), full), pl.BlockSpec((1, LANES), full)]
    args = [*outs, x2d, g1, sc2, sh2, lw["w_out"], lw["ln1_g"], lw["ln1_b"], lw["wr_hi"], lw["wr_lo"], lw["br"]]
    aliases = {}
    if prev is not None:
        for k, a in enumerate(prev):
            aliases[len(args)] = 1 + k
            in_specs.append(pl.BlockSpec(memory_space=pl.ANY))
            args.append(a)
    return pl.pallas_call(
        functools.partial(_out_kernel, alpha),
        grid=(n // tm,),
        in_specs=in_specs,
        out_specs=[pl.BlockSpec((tm, d), row), pl.BlockSpec((tm, d + H2_EXT), lambda i: (i + row0 // tm, 0)),
                   pl.BlockSpec((8, tm), lambda i: (0, i + row0 // tm))],
        out_shape=[jax.ShapeDtypeStruct((n, d), F32), jax.ShapeDtypeStruct((n_rows, d + H2_EXT), BF16),
                   jax.ShapeDtypeStruct((8, n_rows), F32)],
        input_output_aliases=aliases,
        compiler_params=_cparams("parallel"),
    )(*args)


def _moe_kernel(tile0, e0_ref, e1_ref, valid_ref, xs_ref, wg0, wg1, wu0, wu1, wd0, wd1, *rest):
    o_ref, wb = rest[-7], rest[-6:]
    t = pl.program_id(0) + tile0
    d = o_ref.shape[1]
    prev = jnp.maximum(t - 1, 0)
    first = pl.program_id(0) == 0

    @pl.when(first | (e0_ref[t] != e0_ref[prev]))
    def _():
        for src, dst in zip((wg0, wu0, wd0), wb[0::2]):
            dst[...] = src[0, 0].astype(BF16)

    @pl.when(first | (e1_ref[t] != e1_ref[prev]))
    def _():
        for src, dst in zip((wg1, wu1, wd1), wb[1::2]):
            dst[...] = src[0, 0].astype(BF16)

    @pl.when(valid_ref[t] == 1)
    def _():
        x = xs_ref[:, 0:d]
        rt = xs_ref[:, d:d + LANES].astype(F32) + xs_ref[:, d + LANES:d + 2 * LANES].astype(F32)
        lower_first = e0_ref[t] < e1_ref[t]
        w0 = jnp.where(lower_first, rt[:, 2:3], rt[:, 3:4])
        w1 = jnp.where(lower_first, rt[:, 3:4], rt[:, 2:3])

        def expert(wg, wu, wd):
            a = _silu(_dot(x, wg[...])) * _dot(x, wu[...])
            return _dot(a.astype(BF16), wd[...])

        y = w0 * expert(wb[0], wb[2], wb[4]) + w1 * expert(wb[1], wb[3], wb[5])
        o_ref[...] = y.astype(o_ref.dtype)

    @pl.when(valid_ref[t] == 0)
    def _():
        o_ref[...] = jnp.zeros_like(o_ref)


def _moe_experts(layer, tile0, e0, e1, valid, xs, wg, wu, wd, ys_prev):
    p, dx = xs.shape
    d = dx - H2_EXT
    tm = MOE_TILE
    de = wg.shape[3]
    w0 = lambda t, e0, e1, v: (layer, e0[t + tile0], 0, 0)
    w1 = lambda t, e0, e1, v: (layer, e1[t + tile0], 0, 0)
    in_specs = [pl.BlockSpec((tm, dx), lambda t, e0, e1, v: (t, 0)),
                pl.BlockSpec((1, 1, d, de), w0), pl.BlockSpec((1, 1, d, de), w1),
                pl.BlockSpec((1, 1, d, de), w0), pl.BlockSpec((1, 1, d, de), w1),
                pl.BlockSpec((1, 1, de, d), w0), pl.BlockSpec((1, 1, de, d), w1)]
    args = [e0, e1, valid, xs, wg, wg, wu, wu, wd, wd]
    aliases = {}
    if ys_prev is not None:
        aliases[len(args)] = 0
        in_specs.append(pl.BlockSpec(memory_space=pl.ANY))
        args.append(ys_prev)
    return pl.pallas_call(
        functools.partial(_moe_kernel, tile0),
        grid_spec=pltpu.PrefetchScalarGridSpec(
            num_scalar_prefetch=3,
            grid=(p // tm,),
            in_specs=in_specs,
            out_specs=pl.BlockSpec((tm, d), lambda t, e0, e1, v: (t + tile0, 0)),
            scratch_shapes=[pltpu.VMEM((d, de), BF16)] * 4 + [pltpu.VMEM((de, d), BF16)] * 2),
        out_shape=jax.ShapeDtypeStruct((e0.shape[0] * tm, d), BF16),
        input_output_aliases=aliases,
        compiler_params=_cparams("arbitrary"),
    )(*args)


_PAIR_TABLE = ((0, 1), (0, 2), (0, 3), (1, 3), (1, 2), (3, 2))


def _moe(layer, h2x, ids, wg, wu, wd, chunks, segments):
    n = h2x.shape[0]
    tm = MOE_TILE
    n_tiles = -(-(n // tm + N_BUCKETS) // chunks) * chunks
    bucket = ids[0].astype(jnp.int32)
    bids = jnp.arange(N_BUCKETS + 1, dtype=jnp.int32)
    onehot_b = bucket[None, :] == bids[:N_BUCKETS, None]
    per_token = lambda table: jnp.sum(jnp.where(onehot_b, table[:, None], 0), axis=0)

    seg = []
    within_tok = []
    start = 0
    for size in segments:
        tok = jnp.arange(size, dtype=jnp.int32)
        sorted_bucket, order = lax.sort((bucket[start:start + size], tok), num_keys=1)
        _, rank = lax.sort((order, tok), num_keys=1)
        edges = jnp.sum(sorted_bucket[None, :] < bids[:, None], axis=1, dtype=jnp.int32)
        seg.append((start, order, edges[:-1], edges[1:] - edges[:-1]))
        within_tok.append(rank)
        start += size
    counts = sum(c for _, _, _, c in seg)
    tiles_per = (counts + tm - 1) // tm
    tile_end = jnp.cumsum(tiles_per)
    tile_start = tile_end - tiles_per

    tile_ids = jnp.arange(n_tiles, dtype=jnp.int32)
    valid = (tile_ids < tile_end[-1]).astype(jnp.int32)
    tile_bucket = jnp.sum(tile_ids[:, None] >= tile_end[None, :], axis=1, dtype=jnp.int32)
    last_bucket = jnp.sum(tile_end[-1] - 1 >= tile_end, dtype=jnp.int32)
    tile_bucket = jnp.minimum(jnp.where(valid == 1, tile_bucket, last_bucket), N_BUCKETS - 1)
    onehot_tb = tile_bucket[:, None] == bids[None, :N_BUCKETS]
    pick = lambda table: jnp.sum(jnp.where(onehot_tb, table[None, :], 0), axis=1)
    pairs = jnp.asarray(_PAIR_TABLE, dtype=jnp.int32)
    tg = tile_bucket // N_PAIRS
    e0 = tg * EXPERTS_PER_GROUP + pairs[tile_bucket % N_PAIRS, 0]
    e1 = tg * EXPERTS_PER_GROUP + pairs[tile_bucket % N_PAIRS, 1]

    first_row = (tile_ids - pick(tile_start)) * tm
    within = first_row[:, None] + jnp.arange(tm, dtype=jnp.int32)[None, :]
    row_ok = (within < pick(counts)[:, None]) & (valid[:, None] == 1)
    order_all = jnp.concatenate([start + order for start, order, _, _ in seg])
    idx = jnp.zeros(within.shape, jnp.int32)
    before = jnp.zeros((N_BUCKETS,), jnp.int32)
    base = []
    for (start, order, off, cnt), rank in zip(seg, within_tok):
        j = within - pick(before)[:, None]
        mine = (j >= 0) & (j < pick(cnt)[:, None])
        idx = jnp.where(mine, start + pick(off)[:, None] + j, idx)
        base.append((tile_start * tm + before - off, start, rank))
        before = before + cnt
    idx = jnp.clip(idx, 0, n - 1).reshape(-1)
    src = jnp.where(row_ok.reshape(-1), order_all.at[idx].get(mode="promise_in_bounds"), 0)

    per_chunk = n_tiles // chunks
    ys = None
    for c in range(chunks):
        xs = h2x.at[src[c * per_chunk * tm:(c + 1) * per_chunk * tm]].get(mode="promise_in_bounds")
        ys = _moe_experts(layer, c * per_chunk, e0, e1, valid, xs, wg, wu, wd, ys)

    pos = jnp.concatenate([per_token(table)[start:start + rank.shape[0]] + rank for table, start, rank in base])
    return ys.at[pos].get(mode="promise_in_bounds")


def _fin_kernel(alpha, x_ref, y_ref, g2_ref, lng_ref, lnb_ref, o_ref):
    o_ref[...] = _ln(alpha * x_ref[...] + g2_ref[0] * y_ref[...]) * lng_ref[...] + lnb_ref[...]


def _finish(alpha, x1, y, g2, ln_g, ln_b, tokens_per_batch):
    n, d = x1.shape
    tm = min(ROW_TILE, tokens_per_batch)
    per_b = tokens_per_batch // tm
    row = lambda i: (i, 0)
    full = lambda i: (0, 0)
    return pl.pallas_call(
        functools.partial(_fin_kernel, alpha),
        grid=(n // tm,),
        in_specs=[pl.BlockSpec((tm, d), row), pl.BlockSpec((tm, d), row),
                  pl.BlockSpec((1, 1, d), lambda i: (i // per_b, 0, 0)),
                  pl.BlockSpec((1, d), full), pl.BlockSpec((1, d), full)],
        out_specs=pl.BlockSpec((tm, d), row),
        out_shape=jax.ShapeDtypeStruct((n, d), F32),
        compiler_params=_cparams("parallel"),
    )(x1, y, g2, ln_g, ln_b)


def _rope_tables(t):
    pos = jnp.arange(t)
    row = (pos // GRID_W).astype(F32)[:, None]
    col = (pos % GRID_W).astype(F32)[:, None]
    lane = jnp.arange(LANES)

    def build(d, half, active):
        axis_col = (d // half) % 2 == 1
        w = d % half
        first = w < half // 2
        f = (w % (half // 2)).astype(F32)
        freq = ROPE_THETA ** (-(2.0 * f) / half)
        ang = jnp.where(axis_col[None, :], col, row) * freq[None, :]
        cos = jnp.where(active[None, :], jnp.cos(ang), 1.0)
        sin = jnp.where(active[None, :], jnp.sin(ang), 0.0)
        return jnp.stack([cos, jnp.where(first[None, :], 0.0, sin), jnp.where(first[None, :], -sin, 0.0)])

    t64 = build(lane % HEAD_DIM, HEAD_DIM // 2, lane >= 0)
    in_rope = (lane >= MLA_NOPE) & (lane < MLA_QK)
    tq = build(jnp.where(in_rope, lane - MLA_NOPE, 0), MLA_ROPE // 2, in_rope)
    in_kr = lane < MLA_ROPE
    tk = build(jnp.where(in_kr, lane, 0), MLA_ROPE // 2, in_kr)
    return t64, tq, tk


def _nbr_bias_tables(na_bias, rows):
    heads = na_bias.shape[0]
    npair = rows // 2
    kr = min(NA_ROWS, rows)
    qcol = jnp.arange(GRID_W)[:, None]
    kcol = jnp.arange(GRID_W)[None, :]
    c0 = jnp.clip(qcol - NA_COLS // 2, 0, GRID_W - NA_COLS)
    col_ok = (kcol >= c0) & (kcol < c0 + NA_COLS)
    col_off = kcol - qcol + NA_COLS - 1
    onehot = (col_off[None] == jnp.arange(2 * NA_COLS - 1)[:, None, None]).astype(F32)
    colmat = jnp.einsum("hdc,cqk->hdqk", na_bias, onehot, precision=lax.Precision.HIGHEST) * LOG2E
    colmat = jnp.where(col_ok[None, None], colmat, NEG)
    outside = jnp.full((heads, GRID_W, GRID_W), NEG, F32)
    tabs = []
    for p in (0, 1, 2, npair - 2, npair - 1):
        ks = min(max(2 * p - NA_ROWS // 2, 0), rows - NBR_SPAN_ROWS)
        per_qrow = []
        for qr in range(2):
            r = 2 * p + qr
            r0 = min(max(r - kr // 2, 0), rows - kr)
            blocks = []
            for kri in range(NBR_SPAN_ROWS):
                krow = ks + kri
                blocks.append(colmat[:, krow - r + NA_ROWS - 1] if r0 <= krow < r0 + kr else outside)
            per_qrow.append(jnp.concatenate(blocks, axis=-1))
        tab = jnp.concatenate(per_qrow, axis=1)
        tabs.append(tab.reshape(heads // 2, 4 * GRID_W, NBR_SPAN_ROWS * GRID_W))
    return jnp.stack(tabs)


def _layer_weights(l, w_in, gqa_q_gain, gqa_k_gain, mla_q_gain, mla_w_qb, mla_kv_gain, mla_w_kvb,
                   w_out, ln1_g, ln1_b, router_group_w, router_group_b, router_expert_w, router_expert_b):
    d = w_in.shape[1]
    d_proj = w_in.shape[2]
    lw = {}
    lw["w_in"] = jnp.pad(w_in[l], ((0, 0), (0, 2304 - d_proj))).astype(BF16)
    lw["gq"] = jnp.tile(gqa_q_gain[l], 4)[None, :]
    lw["gk"] = jnp.tile(gqa_k_gain[l], 2)[None, :]
    lw["gmq"] = mla_q_gain[l][None, :]
    lw["gmkv"] = mla_kv_gain[l][None, :]
    wqb = mla_w_qb[l].reshape(-1, 4, MLA_QK)
    lw["w_qb"] = jnp.pad(wqb, ((0, 0), (0, 0), (0, LANES - MLA_QK))).reshape(-1, 4 * LANES).astype(BF16)
    wkvb = mla_w_kvb[l].reshape(-1, 4, 2 * HEAD_DIM)
    k_part = jnp.pad(wkvb[:, :, :MLA_NOPE], ((0, 0), (0, 0), (0, LANES - MLA_NOPE))).reshape(-1, 4 * LANES)
    v_part = jnp.pad(wkvb[:, :, MLA_NOPE:], ((0, 0), (0, 0), (0, LANES - HEAD_DIM))).reshape(-1, 4 * LANES)
    lw["w_kvb"] = jnp.concatenate([k_part, v_part], axis=1).astype(BF16)
    idx = jnp.arange(256) // HEAD_DIM
    lw["bd"] = (idx[:, None] == idx[None, :]).astype(BF16)
    lw["w_out"] = w_out[l].astype(BF16)
    lw["ln1_g"] = ln1_g[l][None, :]
    lw["ln1_b"] = ln1_b[l][None, :]
    wr = jnp.concatenate([router_group_w[l], router_expert_w[l]], axis=1)
    wr = jnp.pad(wr, ((0, 0), (0, LANES - wr.shape[1])))
    lw["wr_hi"] = wr.astype(BF16)
    lw["wr_lo"] = (wr - lw["wr_hi"].astype(F32)).astype(BF16)
    br = jnp.concatenate([router_group_b[l], router_expert_b[l]])
    lw["br"] = jnp.pad(br, (0, LANES - br.shape[0]))[None, :]
    return lw


def kernel(x, c, ctx, c_ctx, w_ada, b_ada, w_in, na_bias, gqa_q_gain, gqa_k_gain, win_sink, mla_q_gain, mla_w_qb, mla_kv_gain, mla_w_kvb, w_out, ln1_g, ln1_b, router_group_w, router_group_b, router_expert_w, router_expert_b, moe_w_gate, moe_w_up, moe_w_down, ln2_g, ln2_b):
    bsz, t, d = x.shape
    tc = ctx.shape[1]
    depth = w_ada.shape[0]
    rows = t // GRID_W
    assert t % (2 * GRID_W) == 0 and rows >= NBR_SPAN_ROWS and rows // 2 >= 5
    assert t >= BLOCK + 2 * WINDOW and t % ROW_TILE == 0 and tc % (8 * ROW_SPLIT) == 0 and t % tc == 0
    assert tc <= ROW_TILE and t % OUT_TILE == 0
    assert t % (LOCAL_NSUB * BLOCK) == 0 and t % GLOBAL_BQ == 0
    s_tot = t + tc
    alpha = (2.0 * depth) ** 0.25

    ada_rows = -(-(bsz + 1) // 8) * 8
    cs = jnp.zeros((ada_rows, d), F32).at[:bsz].set(c).at[bsz].set(c_ctx)
    mod = _ada(cs, w_ada, b_ada)
    tabs = _rope_tables(t)

    xl = x.reshape(bsz * t, d)
    xc = ctx.reshape(bsz * tc, d)
    fin_l = fin_c = None

    for l in range(depth):
        with_ctx = l < depth - 1
        lw = _layer_weights(l, w_in, gqa_q_gain, gqa_k_gain, mla_q_gain, mla_w_qb, mla_kv_gain,
                            mla_w_kvb, w_out, ln1_g, ln1_b, router_group_w, router_group_b,
                            router_expert_w, router_expert_b)
        ln2g, ln2b = ln2_g[l][None, :], ln2_b[l][None, :]

        def mods(lo, hi):
            parts = jnp.split(mod[l, lo:hi], 6, axis=-1)
            return [jnp.broadcast_to(p[:, None, :], (bsz, 1, d)) for p in parts]
        sh1, sc1, g1, sh2, sc2, g2 = mods(0, bsz)
        sh1c, sc1c, g1c, sh2c, sc2c, g2c = mods(bsz, bsz + 1)

        xl, pl_lat = _project(xl, fin_l, sc1, sh1, lw, tabs, t, s_tot, 0)
        xc, pl_ctx = _project(xc, fin_c, sc1c, sh1c, lw, None, tc, s_tot, t,
                              kv_prev=[pl_lat[i] for i in _KV_OUTS])
        qa, qb, qc, qd = [pl_lat[i].reshape(bsz, t, -1) for i in _Q_OUTS]
        qa_c, qb_c, qc_c, qd_c = [pl_ctx[i].reshape(bsz, tc, -1) for i in _Q_OUTS]
        ka, va, kb, vb, kc, vc, kd, vd = [pl_ctx[i] for i in _KV_OUTS]

        bias_tab = _nbr_bias_tables(na_bias[l], rows)
        everything = dict(parts=_key_ranges(s_tot), bq=GLOBAL_BQ, single_shot=False)
        out_a = _attention("mha", qa, ka, va, [("nbr", 0, t), ("full", t, tc)], bias=bias_tab, nsub=LOCAL_NSUB)
        out_b = _attention("gqa", qb, kb, vb, **everything)
        out_c = _attention("gqa", qc, kc, vc, [("win", 0, t), ("full", t, tc)], sink=win_sink[l],
                           nsub=LOCAL_NSUB)
        out_d = _attention("mla", qd, kd, vd, **everything)
        outs = [o.reshape(bsz * t, 256) for o in (out_a, out_b, out_c, out_d)]

        n_lat, n_ctx = bsz * t, bsz * tc
        n_moe = n_lat + n_ctx if with_ctx else n_lat
        x1, h2x, ids = _out_proj(alpha, outs, xl, g1, sc2, sh2, lw, t, n_moe, 0)
        if with_ctx:
            ctx_only = dict(parts=[("full", 0, tc)], kv_block=(tc, t // tc))
            outs_c = [
                _attention("mha", qa_c, ka, va, **ctx_only),
                _attention("gqa", qb_c, kb, vb, **ctx_only),
                _attention("gqa", qc_c, kc, vc, sink=win_sink[l], **ctx_only),
                _attention("mla", qd_c, kd, vd, **ctx_only),
            ]
            outs_c = [o.reshape(n_ctx, 256) for o in outs_c]
            x1c, h2x, ids = _out_proj(alpha, outs_c, xc, g1c, sc2c, sh2c, lw, tc, n_moe, n_lat, prev=(h2x, ids))
        y = _moe(l, h2x, ids, moe_w_gate, moe_w_up, moe_w_down, MOE_CHUNKS,
                 (n_lat, n_ctx) if with_ctx else (n_lat,))
        fin_l = (alpha, x1, y, g2, ln2g, ln2b, 0)
        if with_ctx:
            fin_c = (alpha, x1c, y, g2c, ln2g, ln2b, n_lat)

    return _finish(*fin_l[:6], t).reshape(bsz, t, d)
```

```python
import functools

import jax
import jax.numpy as jnp
from jax import lax
from jax.experimental import pallas as pl
from jax.experimental.pallas import tpu as pltpu

F32 = jnp.float32
BF16 = jnp.bfloat16

GRID_W = 64
HEAD_DIM = 64
BLOCK = 128
WINDOW = 128
ROPE_THETA = 10000.0
NEG = -1e30
EPS = 1e-6
LOG2E = 1.4426950408889634

NA_ROWS = 8
NA_COLS = 16
NBR_SPAN_ROWS = 10
MLA_NOPE = 64
MLA_ROPE = 32
MLA_QK = MLA_NOPE + MLA_ROPE
N_GROUPS = 4
EXPERTS_PER_GROUP = 4
N_EXPERTS = 16
N_PAIRS = 6
N_BUCKETS = N_GROUPS * N_PAIRS
ROUTE_LANES = 8

LANES = 128
VMEM_LIMIT = 48 * 1024 * 1024

ROW_TILE = 512
ROW_SPLIT = 2
OUT_TILE = 1024
MOE_TILE = 256
MOE_CHUNKS = 4
KV_CHUNK = 2560
GLOBAL_BQ = 256
LOCAL_NSUB = 8
H2_EXT = 128


def _cparams(*sem):
    return pltpu.CompilerParams(dimension_semantics=sem, vmem_limit_bytes=VMEM_LIMIT)


def _dot(a, b):
    return jnp.dot(a, b, preferred_element_type=F32)


def _dot_nt(a, b):
    return lax.dot_general(a, b, (((1,), (1,)), ((), ())), preferred_element_type=F32)


def _split(a):
    hi = a.astype(BF16)
    lo = (a - hi.astype(F32)).astype(BF16)
    return hi, lo


def _dot3(a, b):
    ah, al = _split(a)
    bh, bl = _split(b)
    return _dot(al, bh) + _dot(ah, bl) + _dot(ah, bh)


def _ln(x):
    mu = jnp.mean(x, axis=-1, keepdims=True)
    xc = x - mu
    var = jnp.mean(xc * xc, axis=-1, keepdims=True)
    return xc * lax.rsqrt(var + EPS)


def _silu(g):
    return g / (1.0 + jnp.exp(-g))


def _ada_kernel(c_ref, w_ref, b_ref, o_ref):
    o_ref[0] = _dot3(_silu(c_ref[...]), w_ref[0]) + b_ref[0]


def _ada(cs, w_ada, b_ada):
    depth, d, n = w_ada.shape
    rows = cs.shape[0]
    tn = 1536
    return pl.pallas_call(
        _ada_kernel,
        grid=(depth, n // tn),
        in_specs=[pl.BlockSpec((rows, d), lambda l, j: (0, 0)),
                  pl.BlockSpec((1, d, tn), lambda l, j: (l, 0, j)),
                  pl.BlockSpec((1, 1, tn), lambda l, j: (l, 0, j))],
        out_specs=pl.BlockSpec((1, rows, tn), lambda l, j: (l, 0, j)),
        out_shape=jax.ShapeDtypeStruct((depth, rows, n), F32),
        compiler_params=_cparams("parallel", "parallel"),
    )(cs, w_ada, b_ada.reshape(depth, 1, n))


def _head_rms(x, bd, gain):
    hi, lo = _split(x * x)
    ss = _dot(lo, bd) + _dot(hi, bd)
    return x * lax.rsqrt(ss * (1.0 / HEAD_DIM) + EPS) * gain


def _rms(x, gain):
    return x * lax.rsqrt(jnp.mean(x * x, axis=-1, keepdims=True) + EPS) * gain


def _rope(x, tab_ref, shift):
    outs = []
    for j in range(x.shape[1] // LANES):
        xb = x[:, j * LANES:(j + 1) * LANES]
        outs.append(xb * tab_ref[0]
                    + pltpu.roll(xb, shift, 1) * tab_ref[1]
                    + pltpu.roll(xb, LANES - shift, 1) * tab_ref[2])
    return outs[0] if len(outs) == 1 else jnp.concatenate(outs, axis=1)


def _value_tiles(v):
    lane = lax.broadcasted_iota(jnp.int32, (v.shape[0], LANES), 1)
    fill = jnp.where(lane == HEAD_DIM, 1.0, 0.0)
    outs = []
    for j in range(v.shape[1] // LANES):
        blk = v[:, j * LANES:(j + 1) * LANES]
        outs.append(jnp.where(lane < HEAD_DIM, blk, fill))
        outs.append(jnp.where(lane < HEAD_DIM, pltpu.roll(blk, HEAD_DIM, 1), fill))
    return jnp.concatenate(outs, axis=1).astype(BF16)


def _proj_kernel(rope, fin_alpha, n_aliased, *refs):
    refs = list(refs)
    if fin_alpha is None:
        x_ref = refs.pop(0)
    else:
        x1_ref, y_ref, g2_ref, lng_ref, lnb_ref = refs[:5]
        refs = refs[5:]
    (sc_ref, sh_ref, w_ref, gq_ref, gk_ref, gmq_ref, gmkv_ref,
     wqb_ref, wkvb_ref, bd_ref) = refs[:10]
    refs = refs[10:]
    tabs = None
    if rope:
        tabs = refs[:3]
        refs = refs[3:]
    refs = refs[n_aliased:]
    if fin_alpha is not None:
        xo_ref = refs.pop(0)
    refs = [r.at[0] if len(r.shape) == 3 else r for r in refs]
    group = refs[0].shape[0] // ROW_SPLIT
    for u in range(ROW_SPLIT):
        rows = slice(u * group, (u + 1) * group)
        if fin_alpha is None:
            x = x_ref[rows, :]
        else:
            x = (_ln(fin_alpha * x1_ref[rows, :] + g2_ref[0] * y_ref[rows, :]) * lng_ref[...]
                 + lnb_ref[...])
            xo_ref[rows, :] = x
        _proj_rows(rope, x, rows, sc_ref, sh_ref, w_ref, gq_ref, gk_ref, gmq_ref, gmkv_ref, wqb_ref,
                   wkvb_ref, bd_ref, tabs, refs)


def _proj_rows(rope, x, rows, sc_ref, sh_ref, w_ref, gq_ref, gk_ref, gmq_ref, gmkv_ref, wqb_ref, wkvb_ref,
               bd_ref, tabs, outs):
    qa, ka, va, qb, kb, vb, qc, kc, vc, qd, kd, vd = outs
    if rope:
        t64_ref, tq_ref, tk_ref = [t.at[:, rows, :] for t in tabs]
    h = _ln(x) * (1.0 + sc_ref[0]) + sh_ref[0]
    hb = h.astype(BF16)
    sq = HEAD_DIM ** -0.5 * LOG2E

    def proj(a, b):
        return _dot(hb, w_ref[:, a:b])

    pa = proj(0, 768)
    qa[rows, :] = (pa[:, 0:256] * sq).astype(BF16)
    ka[rows, :] = pa[:, 256:512].astype(BF16)
    va[rows, :] = _value_tiles(pa[:, 512:768])

    pb = proj(768, 1280)
    q = _head_rms(pb[:, 0:256], bd_ref[...], gq_ref[...])
    k = _head_rms(pb[:, 256:384], bd_ref[0:128, 0:128], gk_ref[...])
    if rope:
        q = _rope(q, t64_ref, 16)
        k = _rope(k, t64_ref, 16)
    qb[rows, :] = (q * sq).astype(BF16)
    kb[rows, :] = k.astype(BF16)
    vb[rows, :] = _value_tiles(pb[:, 384:512])

    pc = proj(1280, 1792)
    q = pc[:, 0:256]
    k = pc[:, 256:384]
    if rope:
        q = _rope(q, t64_ref, 16)
        k = _rope(k, t64_ref, 16)
    qc[rows, :] = (q * sq).astype(BF16)
    kc[rows, :] = k.astype(BF16)
    vc[rows, :] = _value_tiles(pc[:, 384:512])

    pd = proj(1792, 2304)
    cq = _rms(pd[:, 0:256], gmq_ref[...])
    q = _dot(cq.astype(BF16), wqb_ref[...])
    if rope:
        q = _rope(q, tq_ref, 8)
    qd[rows, :] = (q * (MLA_QK ** -0.5 * LOG2E)).astype(BF16)
    ckv = _rms(pd[:, 256:384], gmkv_ref[...])
    kvu = _dot(ckv.astype(BF16), wkvb_ref[...])
    kr = pd[:, 384:512]
    if rope:
        kr = _rope(kr, tk_ref, 8)
    kr = pltpu.roll(kr, MLA_NOPE, 1)
    kd[rows, :] = jnp.concatenate(
        [kvu[:, j * LANES:(j + 1) * LANES] + kr for j in range(4)], axis=1).astype(BF16)
    lane = lax.broadcasted_iota(jnp.int32, (kr.shape[0], LANES), 1)
    fill = jnp.where(lane == HEAD_DIM, 1.0, 0.0)
    vd[rows, :] = jnp.concatenate(
        [kvu[:, (4 + j) * LANES:(5 + j) * LANES] + fill for j in range(4)], axis=1).astype(BF16)


_PROJ_WIDTHS = (256, 256, 512, 256, 128, 256, 256, 128, 256, 512, 512, 512)


_Q_OUTS = (0, 3, 6, 9)
_KV_OUTS = (1, 2, 4, 5, 7, 8, 10, 11)


def _project(x2d, fin, sc, sh, lw, tabs, tokens_per_batch, kv_rows, kv_row0, kv_prev=None):
    n, d = (x2d if fin is None else fin[1]).shape
    tm = min(ROW_TILE, tokens_per_batch)
    per_b = tokens_per_batch // tm
    bsz = n // tokens_per_batch
    rope = tabs is not None
    row = lambda i: (i, 0)
    full = lambda i: (0, 0)
    perb = lambda i: (i // per_b, 0, 0)
    kvrow = lambda i: (i // per_b, kv_row0 // tm + i % per_b, 0)
    if fin is None:
        in_specs = [pl.BlockSpec((tm, d), row)]
        args = [x2d]
    else:
        in_specs = [pl.BlockSpec((tm, d), row), pl.BlockSpec((tm, d), lambda i: (i + fin[6] // tm, 0)),
                    pl.BlockSpec((1, 1, d), perb), pl.BlockSpec((1, d), full), pl.BlockSpec((1, d), full)]
        args = list(fin[1:6])
    in_specs += [pl.BlockSpec((1, 1, d), perb), pl.BlockSpec((1, 1, d), perb),
                 pl.BlockSpec(lw["w_in"].shape, full),
                 pl.BlockSpec((1, 256), full), pl.BlockSpec((1, 128), full),
                 pl.BlockSpec((1, 256), full), pl.BlockSpec((1, 128), full),
                 pl.BlockSpec(lw["w_qb"].shape, full), pl.BlockSpec(lw["w_kvb"].shape, full),
                 pl.BlockSpec((256, 256), full)]
    args += [sc, sh, lw["w_in"], lw["gq"], lw["gk"], lw["gmq"], lw["gmkv"],
             lw["w_qb"], lw["w_kvb"], lw["bd"]]
    if rope:
        tab = lambda i: (0, i % per_b, 0)
        in_specs += [pl.BlockSpec((3, tm, LANES), tab)] * 3
        args += list(tabs)
    out_specs = [pl.BlockSpec((1, tm, w), kvrow) if i in _KV_OUTS else pl.BlockSpec((tm, w), row)
                 for i, w in enumerate(_PROJ_WIDTHS)]
    out_shape = [jax.ShapeDtypeStruct((bsz, kv_rows, w) if i in _KV_OUTS else (n, w), BF16)
                 for i, w in enumerate(_PROJ_WIDTHS)]
    lead = 0 if fin is None else 1
    if fin is not None:
        out_specs.insert(0, pl.BlockSpec((tm, d), row))
        out_shape.insert(0, jax.ShapeDtypeStruct((n, d), F32))
    aliases = {}
    if kv_prev is not None:
        for a, i in zip(kv_prev, _KV_OUTS):
            aliases[len(args)] = lead + i
            in_specs.append(pl.BlockSpec(memory_space=pl.ANY))
            args.append(a)
    outs = pl.pallas_call(
        functools.partial(_proj_kernel, rope, None if fin is None else fin[0], len(aliases)),
        grid=(n // tm,),
        in_specs=in_specs,
        out_specs=out_specs,
        out_shape=out_shape,
        input_output_aliases=aliases,
        compiler_params=_cparams("parallel"),
    )(*args)
    return (x2d, list(outs)) if fin is None else (outs[0], list(outs[1:]))


def _softmax_step(s, vts, bq, m, acc):
    m_new = jnp.maximum(m, jnp.max(s, axis=1, keepdims=True))
    alpha = jnp.exp2(m - m_new)
    p = jnp.exp2(s - m_new).astype(BF16)
    if len(vts) == 1:
        pv = _dot(p, vts[0])
    else:
        pv = jnp.concatenate([_dot(p[:bq], vts[0]), _dot(p[bq:], vts[1])], axis=0)
    return [m_new, alpha * acc + pv]


def _attn_kernel(mode, bq, nsub, srcs, single_shot, has_bias, has_sink, *refs):
    refs = list(refs)
    q_ref = refs.pop(0)
    k_ref, v_ref = refs.pop(0), refs.pop(0)
    bias_refs = [refs.pop(0) for _ in range(nsub)] if has_bias else None
    sink_ref = refs.pop(0) if has_sink else None
    o_ref = refs.pop(0)

    for u in range(nsub):
        qi = pl.program_id(1) * nsub + u
        rows = slice(u * bq, (u + 1) * bq)
        rows2 = 2 * bq
        lo = lax.broadcasted_iota(jnp.int32, (bq, LANES), 1) < HEAD_DIM

        qss, kcs, vcs = [], [], []
        for j in range(2):
            if mode == "mla":
                qblk = q_ref[0, rows, 2 * LANES * j:2 * LANES * (j + 1)].astype(F32)
                first = lax.broadcasted_iota(jnp.int32, (bq, 2 * LANES), 1) < LANES
                qs = jnp.concatenate([jnp.where(first, qblk, 0.0), jnp.where(first, 0.0, qblk)], axis=0)
                kcs.append((2 * LANES * j, 2 * LANES * (j + 1)))
                vcs.append((2 * LANES * j, 2 * LANES * j + LANES))
            else:
                qblk = q_ref[0, rows, LANES * j:LANES * (j + 1)].astype(F32)
                if mode == "mha":
                    qe, qo = jnp.where(lo, qblk, 0.0), jnp.where(lo, 0.0, qblk)
                    kcs.append((LANES * j, LANES * (j + 1)))
                    vcs.append((2 * LANES * j, 2 * LANES * j + LANES))
                else:
                    rolled = pltpu.roll(qblk, HEAD_DIM, 1)
                    if j == 0:
                        qe, qo = jnp.where(lo, qblk, 0.0), jnp.where(lo, rolled, 0.0)
                    else:
                        qe, qo = jnp.where(lo, 0.0, rolled), jnp.where(lo, 0.0, qblk)
                    kcs.append((0, LANES))
                    vcs.append((LANES * j,))
                qs = jnp.concatenate([qe, qo], axis=0)
            qss.append(qs.astype(BF16))

        def pv(p, vts):
            if len(vts) == 1:
                return _dot(p, vts[0])
            return jnp.concatenate([_dot(p[:bq], vts[0]), _dot(p[bq:], vts[1])], axis=0)

        def tiles(j, off, n):
            kt = k_ref[0, pl.ds(off, n), kcs[j][0]:kcs[j][1]]
            return kt, [v_ref[0, pl.ds(off, n), v0:v0 + LANES] for v0 in vcs[j]]

        def block_scores(j, kind, row0, nrows):
            if kind == "full":
                kt, vts = tiles(j, row0, nrows)
                return _dot_nt(qss[j], kt), vts
            if kind == "win":
                span = bq + 2 * WINDOW
                start = pl.multiple_of(jnp.clip((qi - 1) * bq, 0, nrows - span), bq)
                r = lax.broadcasted_iota(jnp.int32, (rows2, span), 0)
                qpos = qi * bq + jnp.where(r >= bq, r - bq, r)
                kpos = start + lax.broadcasted_iota(jnp.int32, (rows2, span), 1)
                kt, vts = tiles(j, row0 + start, span)
                return jnp.where(jnp.abs(qpos - kpos) <= WINDOW, _dot_nt(qss[j], kt), NEG), vts
            span = NBR_SPAN_ROWS * GRID_W
            ks = jnp.clip(2 * qi - NA_ROWS // 2, 0, nrows // GRID_W - NBR_SPAN_ROWS)
            kt, vts = tiles(j, row0 + pl.multiple_of(ks * GRID_W, GRID_W), span)
            return _dot_nt(qss[j], kt) + bias_refs[u][0, j], vts

        if single_shot:
            state = []
            for j in range(2):
                parts = [block_scores(j, kind, row0, nrows) for kind, row0, nrows in srcs]
                m = functools.reduce(jnp.maximum, [jnp.max(sc, axis=1, keepdims=True) for sc, _ in parts])
                acc = functools.reduce(lambda a, b: a + b,
                                       [pv(jnp.exp2(sc - m).astype(BF16), vts) for sc, vts in parts])
                state += [m, acc]
        else:
            state = [jnp.full((rows2, 1), NEG, F32), jnp.zeros((rows2, LANES), F32)] * 2
            for kind, row0, nrows in srcs:
                for j in range(2):
                    sc, vts = block_scores(j, kind, row0, nrows)
                    state[2 * j:2 * j + 2] = _softmax_step(sc, vts, bq, state[2 * j], state[2 * j + 1])

        for j in range(2):
            m, acc = state[2 * j], state[2 * j + 1]
            l = acc[:, HEAD_DIM:HEAD_DIM + 1]
            if has_sink:
                sk = jnp.concatenate([jnp.full((bq, 1), sink_ref[2 * j] * LOG2E, F32),
                                      jnp.full((bq, 1), sink_ref[2 * j + 1] * LOG2E, F32)], axis=0)
                mf = jnp.maximum(m, sk)
                a = jnp.exp2(m - mf)
                l = l * a + jnp.exp2(sk - mf)
                acc = acc * a
            o = acc / l
            o_ref[0, rows, LANES * j:LANES * (j + 1)] = jnp.where(
                lo, o[:bq], pltpu.roll(o[bq:], HEAD_DIM, 1)).astype(BF16)


def _key_ranges(nrows):
    n = -(-nrows // KV_CHUNK)
    size = max(nrows // n // 256, 1) * 256
    starts = [i * size for i in range(n)]
    return [("full", r0, (nrows if i == n - 1 else r0 + size) - r0) for i, r0 in enumerate(starts)]


def _attention_plan(mode, q, k, v, parts, kv_block=None, bias=None, sink=None, bq=BLOCK, nsub=1,
                    single_shot=True):
    b, tq, wq = q.shape
    srcs = tuple(parts)
    rows, blk = (k.shape[1], 0) if kv_block is None else kv_block
    in_specs = [pl.BlockSpec((1, nsub * bq, wq), lambda bi, qi: (bi, qi, 0)),
                pl.BlockSpec((1, rows, k.shape[2]), lambda bi, qi: (bi, blk, 0)),
                pl.BlockSpec((1, rows, v.shape[2]), lambda bi, qi: (bi, blk, 0))]
    args = [q, k, v]
    if bias is not None:
        npair = tq // bq

        for u in range(nsub):
            def bias_idx(bi, qi, u=u):
                p = qi * nsub + u
                return (jnp.where(p < 2, p, jnp.where(p >= npair - 2, p - (npair - 2) + 3, 2)), 0, 0, 0)
            in_specs.append(pl.BlockSpec((1,) + bias.shape[1:], bias_idx))
            args.append(bias)
    if sink is not None:
        in_specs.append(pl.BlockSpec(memory_space=pltpu.SMEM))
        args.append(sink)
    body = functools.partial(_attn_kernel, mode, bq, nsub, srcs, single_shot, bias is not None, sink is not None)
    return (body, nsub * bq, in_specs, args, pl.BlockSpec((1, nsub * bq, 256), lambda bi, qi: (bi, qi, 0)),
            jax.ShapeDtypeStruct((b, tq, 256), BF16))


def _attention_run(plans, batch, tq):
    step = plans[0][1]
    assert all(p[1] == step for p in plans)
    counts = [len(p[2]) for p in plans]

    def body(*refs):
        ins, outs = refs[:sum(counts)], refs[sum(counts):]
        at = 0
        for (fn, _, _, _, _, _), n_in, o_ref in zip(plans, counts, outs):
            fn(*ins[at:at + n_in], o_ref)
            at += n_in

    return list(pl.pallas_call(
        body,
        grid=(batch, tq // step),
        in_specs=[sp for p in plans for sp in p[2]],
        out_specs=[p[4] for p in plans],
        out_shape=[p[5] for p in plans],
        compiler_params=_cparams("parallel", "arbitrary"),
    )(*[a for p in plans for a in p[3]]))


def _attention(mode, q, k, v, parts, **kw):
    return _attention_run([_attention_plan(mode, q, k, v, parts, **kw)], q.shape[0], q.shape[1])[0]


def _route(logits):
    lane = lax.broadcasted_iota(jnp.int32, logits.shape, 1).astype(F32)
    big = float(LANES)
    is_g = lane < N_GROUPS
    lg = jnp.where(is_g, logits, NEG)
    gmax = jnp.max(lg, axis=1, keepdims=True)
    grp = jnp.min(jnp.where(lg == gmax, lane, big), axis=1, keepdims=True)
    den = jnp.sum(jnp.where(is_g, jnp.exp(lg - gmax), 0.0), axis=1, keepdims=True)
    p_grp = 1.0 / den
    e0 = N_GROUPS + EXPERTS_PER_GROUP * grp
    in_grp = (lane >= e0) & (lane < e0 + EXPERTS_PER_GROUP)
    le = jnp.where(in_grp, logits, NEG)
    v1 = jnp.max(le, axis=1, keepdims=True)
    i1 = jnp.min(jnp.where(in_grp & (le == v1), lane, big), axis=1, keepdims=True)
    rest = in_grp & (lane != i1)
    le2 = jnp.where(rest, logits, NEG)
    v2 = jnp.max(le2, axis=1, keepdims=True)
    i2 = jnp.min(jnp.where(rest & (le2 == v2), lane, big), axis=1, keepdims=True)
    t = jnp.exp(v2 - v1)
    w1 = p_grp / (1.0 + t)
    w2 = p_grp * t / (1.0 + t)
    first_lower = i1 < i2
    e_lo = jnp.where(first_lower, i1, i2) - N_GROUPS
    e_hi = jnp.where(first_lower, i2, i1) - N_GROUPS
    w_lo = jnp.where(first_lower, w1, w2)
    w_hi = jnp.where(first_lower, w2, w1)
    a = e_lo - EXPERTS_PER_GROUP * grp
    b = e_hi - EXPERTS_PER_GROUP * grp
    pair = a * (2 * EXPERTS_PER_GROUP - 1 - a) * 0.5 + (b - a - 1.0)
    pair = jnp.where(pair == 3.0, 4.0, jnp.where(pair == 4.0, 3.0, pair))
    bucket = grp * N_PAIRS + pair
    return jnp.where(lane == 0, e_lo, jnp.where(lane == 1, e_hi, jnp.where(lane == 2, w_lo,
                     jnp.where(lane == 3, w_hi, jnp.where(lane == 4, bucket, 0.0)))))


def _out_kernel(alpha, oa, ob, oc, od, x_ref, g1_ref, sc2_ref, sh2_ref, w_ref, lng_ref, lnb_ref,
                wrh_ref, wrl_ref, br_ref, *rest):
    x1_ref, h2_ref, ids_ref = rest[-3:]
    mix = (_dot(oa[...], w_ref[0:256, :]) + _dot(ob[...], w_ref[256:512, :])
           + _dot(oc[...], w_ref[512:768, :]) + _dot(od[...], w_ref[768:1024, :]))
    x1 = _ln(alpha * x_ref[...] + g1_ref[0] * mix) * lng_ref[...] + lnb_ref[...]
    x1_ref[...] = x1
    h2 = _ln(x1) * (1.0 + sc2_ref[0]) + sh2_ref[0]
    hh, hl = _split(h2)
    logits = _dot(hl, wrh_ref[...]) + _dot(hh, wrl_ref[...]) + _dot(hh, wrh_ref[...]) + br_ref[...]
    rh, rl = _split(_route(logits))
    d = h2.shape[1]
    h2_ref[:, 0:d] = hh
    lane = lax.broadcasted_iota(jnp.int32, rh.shape, 1)
    h2_ref[:, d:d + LANES] = jnp.where(lane < ROUTE_LANES, rh.astype(F32),
                                       pltpu.roll(rl.astype(F32), ROUTE_LANES, 1)).astype(BF16)
    pick = ((lax.broadcasted_iota(jnp.int32, (8, LANES), 0) == 0)
            & (lax.broadcasted_iota(jnp.int32, (8, LANES), 1) == 4)).astype(BF16)
    ids_ref[...] = _dot_nt(pick, rh)


def _out_proj(alpha, outs, x2d, g1, sc2, sh2, lw, tokens_per_batch, n_rows, row0, prev=None):
    n, d = x2d.shape
    tm = min(OUT_TILE, tokens_per_batch)
    per_b = tokens_per_batch // tm
    row = lambda i: (i, 0)
    full = lambda i: (0, 0)
    perb = lambda i: (i // per_b, 0, 0)
    in_specs = [pl.BlockSpec((tm, 256), row)] * 4 + [
        pl.BlockSpec((tm, d), row),
        pl.BlockSpec((1, 1, d), perb), pl.BlockSpec((1, 1, d), perb), pl.BlockSpec((1, 1, d), perb),
        pl.BlockSpec((d, d), full), pl.BlockSpec((1, d), full), pl.BlockSpec((1, d), full),
        pl.BlockSpec((d, LANES), full), pl.BlockSpec((d, LANES), full), pl.BlockSpec((1, LANES), full)]
    args = [*outs, x2d, g1, sc2, sh2, lw["w_out"], lw["ln1_g"], lw["ln1_b"], lw["wr_hi"], lw["wr_lo"], lw["br"]]
    aliases = {}
    if prev is not None:
        for k, a in enumerate(prev):
            aliases[len(args)] = 1 + k
            in_specs.append(pl.BlockSpec(memory_space=pl.ANY))
            args.append(a)
    return pl.pallas_call(
        functools.partial(_out_kernel, alpha),
        grid=(n // tm,),
        in_specs=in_specs,
        out_specs=[pl.BlockSpec((tm, d), row), pl.BlockSpec((tm, d + H2_EXT), lambda i: (i + row0 // tm, 0)),
                   pl.BlockSpec((8, tm), lambda i: (0, i + row0 // tm))],
        out_shape=[jax.ShapeDtypeStruct((n, d), F32), jax.ShapeDtypeStruct((n_rows, d + H2_EXT), BF16),
                   jax.ShapeDtypeStruct((8, n_rows), F32)],
        input_output_aliases=aliases,
        compiler_params=_cparams("parallel"),
    )(*args)


def _moe_kernel(tile0, e0_ref, e1_ref, valid_ref, xs_ref, wg0, wg1, wu0, wu1, wd0, wd1, *rest):
    o_ref, wb = rest[-7], rest[-6:]
    t = pl.program_id(0) + tile0
    d = o_ref.shape[1]
    prev = jnp.maximum(t - 1, 0)
    first = pl.program_id(0) == 0

    @pl.when(first | (e0_ref[t] != e0_ref[prev]))
    def _():
        for src, dst in zip((wg0, wu0, wd0), wb[0::2]):
            dst[...] = src[0, 0].astype(BF16)

    @pl.when(first | (e1_ref[t] != e1_ref[prev]))
    def _():
        for src, dst in zip((wg1, wu1, wd1), wb[1::2]):
            dst[...] = src[0, 0].astype(BF16)

    @pl.when(valid_ref[t] == 1)
    def _():
        x = xs_ref[:, 0:d]
        ext = xs_ref[:, d:d + LANES].astype(F32)
        rt = ext + pltpu.roll(ext, LANES - ROUTE_LANES, 1)
        lower_first = e0_ref[t] < e1_ref[t]
        w0 = jnp.where(lower_first, rt[:, 2:3], rt[:, 3:4])
        w1 = jnp.where(lower_first, rt[:, 3:4], rt[:, 2:3])

        def expert(wg, wu, wd):
            a = _silu(_dot(x, wg[...])) * _dot(x, wu[...])
            return _dot(a.astype(BF16), wd[...])

        y = w0 * expert(wb[0], wb[2], wb[4]) + w1 * expert(wb[1], wb[3], wb[5])
        o_ref[...] = y.astype(o_ref.dtype)

    @pl.when(valid_ref[t] == 0)
    def _():
        o_ref[...] = jnp.zeros_like(o_ref)


def _moe_experts(layer, tile0, e0, e1, valid, xs, wg, wu, wd, ys_prev):
    p, dx = xs.shape
    d = dx - H2_EXT
    tm = MOE_TILE
    de = wg.shape[3]
    w0 = lambda t, e0, e1, v: (layer, e0[t + tile0], 0, 0)
    w1 = lambda t, e0, e1, v: (layer, e1[t + tile0], 0, 0)
    in_specs = [pl.BlockSpec((tm, dx), lambda t, e0, e1, v: (t, 0)),
                pl.BlockSpec((1, 1, d, de), w0), pl.BlockSpec((1, 1, d, de), w1),
                pl.BlockSpec((1, 1, d, de), w0), pl.BlockSpec((1, 1, d, de), w1),
                pl.BlockSpec((1, 1, de, d), w0), pl.BlockSpec((1, 1, de, d), w1)]
    args = [e0, e1, valid, xs, wg, wg, wu, wu, wd, wd]
    aliases = {}
    if ys_prev is not None:
        aliases[len(args)] = 0
        in_specs.append(pl.BlockSpec(memory_space=pl.ANY))
        args.append(ys_prev)
    return pl.pallas_call(
        functools.partial(_moe_kernel, tile0),
        grid_spec=pltpu.PrefetchScalarGridSpec(
            num_scalar_prefetch=3,
            grid=(p // tm,),
            in_specs=in_specs,
            out_specs=pl.BlockSpec((tm, d), lambda t, e0, e1, v: (t + tile0, 0)),
            scratch_shapes=[pltpu.VMEM((d, de), BF16)] * 4 + [pltpu.VMEM((de, d), BF16)] * 2),
        out_shape=jax.ShapeDtypeStruct((e0.shape[0] * tm, d), BF16),
        input_output_aliases=aliases,
        compiler_params=_cparams("arbitrary"),
    )(*args)


_PAIR_TABLE = ((0, 1), (0, 2), (0, 3), (1, 3), (1, 2), (3, 2))


def _moe(layer, h2x, ids, wg, wu, wd, chunks, segments):
    n = h2x.shape[0]
    tm = MOE_TILE
    n_tiles = -(-(n // tm + N_BUCKETS) // chunks) * chunks
    bucket = ids[0].astype(jnp.int32)
    bids = jnp.arange(N_BUCKETS + 1, dtype=jnp.int32)
    onehot_b = bucket[None, :] == bids[:N_BUCKETS, None]
    per_token = lambda table: jnp.sum(jnp.where(onehot_b, table[:, None], 0), axis=0)

    seg = []
    within_tok = []
    start = 0
    for size in segments:
        tok = jnp.arange(size, dtype=jnp.int32)
        sorted_bucket, order = lax.sort((bucket[start:start + size], tok), num_keys=1)
        _, rank = lax.sort((order, tok), num_keys=1)
        edges = jnp.sum(sorted_bucket[None, :] < bids[:, None], axis=1, dtype=jnp.int32)
        seg.append((start, order, edges[:-1], edges[1:] - edges[:-1]))
        within_tok.append(rank)
        start += size
    counts = sum(c for _, _, _, c in seg)
    tiles_per = (counts + tm - 1) // tm
    tile_end = jnp.cumsum(tiles_per)
    tile_start = tile_end - tiles_per

    tile_ids = jnp.arange(n_tiles, dtype=jnp.int32)
    valid = (tile_ids < tile_end[-1]).astype(jnp.int32)
    tile_bucket = jnp.sum(tile_ids[:, None] >= tile_end[None, :], axis=1, dtype=jnp.int32)
    last_bucket = jnp.sum(tile_end[-1] - 1 >= tile_end, dtype=jnp.int32)
    tile_bucket = jnp.minimum(jnp.where(valid == 1, tile_bucket, last_bucket), N_BUCKETS - 1)
    onehot_tb = tile_bucket[:, None] == bids[None, :N_BUCKETS]
    pick = lambda table: jnp.sum(jnp.where(onehot_tb, table[None, :], 0), axis=1)
    pairs = jnp.asarray(_PAIR_TABLE, dtype=jnp.int32)
    tg = tile_bucket // N_PAIRS
    e0 = tg * EXPERTS_PER_GROUP + pairs[tile_bucket % N_PAIRS, 0]
    e1 = tg * EXPERTS_PER_GROUP + pairs[tile_bucket % N_PAIRS, 1]

    first_row = (tile_ids - pick(tile_start)) * tm
    within = first_row[:, None] + jnp.arange(tm, dtype=jnp.int32)[None, :]
    row_ok = (within < pick(counts)[:, None]) & (valid[:, None] == 1)
    order_all = jnp.concatenate([start + order for start, order, _, _ in seg])
    idx = jnp.zeros(within.shape, jnp.int32)
    before = jnp.zeros((N_BUCKETS,), jnp.int32)
    base = []
    for (start, order, off, cnt), rank in zip(seg, within_tok):
        j = within - pick(before)[:, None]
        mine = (j >= 0) & (j < pick(cnt)[:, None])
        idx = jnp.where(mine, start + pick(off)[:, None] + j, idx)
        base.append((tile_start * tm + before - off, start, rank))
        before = before + cnt
    idx = jnp.clip(idx, 0, n - 1).reshape(-1)
    src = jnp.where(row_ok.reshape(-1), order_all.at[idx].get(mode="promise_in_bounds"), 0)

    per_chunk = n_tiles // chunks
    ys = None
    for c in range(chunks):
        xs = h2x.at[src[c * per_chunk * tm:(c + 1) * per_chunk * tm]].get(mode="promise_in_bounds")
        ys = _moe_experts(layer, c * per_chunk, e0, e1, valid, xs, wg, wu, wd, ys)

    pos = jnp.concatenate([per_token(table)[start:start + rank.shape[0]] + rank for table, start, rank in base])
    return ys.at[pos].get(mode="promise_in_bounds")


def _fin_kernel(alpha, x_ref, y_ref, g2_ref, lng_ref, lnb_ref, o_ref):
    o_ref[...] = _ln(alpha * x_ref[...] + g2_ref[0] * y_ref[...]) * lng_ref[...] + lnb_ref[...]


def _finish(alpha, x1, y, g2, ln_g, ln_b, tokens_per_batch):
    n, d = x1.shape
    tm = min(ROW_TILE, tokens_per_batch)
    per_b = tokens_per_batch // tm
    row = lambda i: (i, 0)
    full = lambda i: (0, 0)
    return pl.pallas_call(
        functools.partial(_fin_kernel, alpha),
        grid=(n // tm,),
        in_specs=[pl.BlockSpec((tm, d), row), pl.BlockSpec((tm, d), row),
                  pl.BlockSpec((1, 1, d), lambda i: (i // per_b, 0, 0)),
                  pl.BlockSpec((1, d), full), pl.BlockSpec((1, d), full)],
        out_specs=pl.BlockSpec((tm, d), row),
        out_shape=jax.ShapeDtypeStruct((n, d), F32),
        compiler_params=_cparams("parallel"),
    )(x1, y, g2, ln_g, ln_b)


def _rope_tables(t):
    pos = jnp.arange(t)
    row = (pos // GRID_W).astype(F32)[:, None]
    col = (pos % GRID_W).astype(F32)[:, None]
    lane = jnp.arange(LANES)

    def build(d, half, active):
        axis_col = (d // half) % 2 == 1
        w = d % half
        first = w < half // 2
        f = (w % (half // 2)).astype(F32)
        freq = ROPE_THETA ** (-(2.0 * f) / half)
        ang = jnp.where(axis_col[None, :], col, row) * freq[None, :]
        cos = jnp.where(active[None, :], jnp.cos(ang), 1.0)
        sin = jnp.where(active[None, :], jnp.sin(ang), 0.0)
        return jnp.stack([cos, jnp.where(first[None, :], 0.0, sin), jnp.where(first[None, :], -sin, 0.0)])

    t64 = build(lane % HEAD_DIM, HEAD_DIM // 2, lane >= 0)
    in_rope = (lane >= MLA_NOPE) & (lane < MLA_QK)
    tq = build(jnp.where(in_rope, lane - MLA_NOPE, 0), MLA_ROPE // 2, in_rope)
    in_kr = lane < MLA_ROPE
    tk = build(jnp.where(in_kr, lane, 0), MLA_ROPE // 2, in_kr)
    return t64, tq, tk


def _nbr_bias_tables(na_bias, rows):
    heads = na_bias.shape[0]
    npair = rows // 2
    kr = min(NA_ROWS, rows)
    qcol = jnp.arange(GRID_W)[:, None]
    kcol = jnp.arange(GRID_W)[None, :]
    c0 = jnp.clip(qcol - NA_COLS // 2, 0, GRID_W - NA_COLS)
    col_ok = (kcol >= c0) & (kcol < c0 + NA_COLS)
    col_off = kcol - qcol + NA_COLS - 1
    onehot = (col_off[None] == jnp.arange(2 * NA_COLS - 1)[:, None, None]).astype(F32)
    colmat = jnp.einsum("hdc,cqk->hdqk", na_bias, onehot, precision=lax.Precision.HIGHEST) * LOG2E
    colmat = jnp.where(col_ok[None, None], colmat, NEG)
    outside = jnp.full((heads, GRID_W, GRID_W), NEG, F32)
    tabs = []
    for p in (0, 1, 2, npair - 2, npair - 1):
        ks = min(max(2 * p - NA_ROWS // 2, 0), rows - NBR_SPAN_ROWS)
        per_qrow = []
        for qr in range(2):
            r = 2 * p + qr
            r0 = min(max(r - kr // 2, 0), rows - kr)
            blocks = []
            for kri in range(NBR_SPAN_ROWS):
                krow = ks + kri
                blocks.append(colmat[:, krow - r + NA_ROWS - 1] if r0 <= krow < r0 + kr else outside)
            per_qrow.append(jnp.concatenate(blocks, axis=-1))
        tab = jnp.concatenate(per_qrow, axis=1)
        tabs.append(tab.reshape(heads // 2, 4 * GRID_W, NBR_SPAN_ROWS * GRID_W))
    return jnp.stack(tabs)


def _layer_weights(l, w_in, gqa_q_gain, gqa_k_gain, mla_q_gain, mla_w_qb, mla_kv_gain, mla_w_kvb,
                   w_out, ln1_g, ln1_b, router_group_w, router_group_b, router_expert_w, router_expert_b):
    d = w_in.shape[1]
    d_proj = w_in.shape[2]
    lw = {}
    lw["w_in"] = jnp.pad(w_in[l], ((0, 0), (0, 2304 - d_proj))).astype(BF16)
    lw["gq"] = jnp.tile(gqa_q_gain[l], 4)[None, :]
    lw["gk"] = jnp.tile(gqa_k_gain[l], 2)[None, :]
    lw["gmq"] = mla_q_gain[l][None, :]
    lw["gmkv"] = mla_kv_gain[l][None, :]
    wqb = mla_w_qb[l].reshape(-1, 4, MLA_QK)
    lw["w_qb"] = jnp.pad(wqb, ((0, 0), (0, 0), (0, LANES - MLA_QK))).reshape(-1, 4 * LANES).astype(BF16)
    wkvb = mla_w_kvb[l].reshape(-1, 4, 2 * HEAD_DIM)
    k_part = jnp.pad(wkvb[:, :, :MLA_NOPE], ((0, 0), (0, 0), (0, LANES - MLA_NOPE))).reshape(-1, 4 * LANES)
    v_part = jnp.pad(wkvb[:, :, MLA_NOPE:], ((0, 0), (0, 0), (0, LANES - HEAD_DIM))).reshape(-1, 4 * LANES)
    lw["w_kvb"] = jnp.concatenate([k_part, v_part], axis=1).astype(BF16)
    idx = jnp.arange(256) // HEAD_DIM
    lw["bd"] = (idx[:, None] == idx[None, :]).astype(BF16)
    lw["w_out"] = w_out[l].astype(BF16)
    lw["ln1_g"] = ln1_g[l][None, :]
    lw["ln1_b"] = ln1_b[l][None, :]
    wr = jnp.concatenate([router_group_w[l], router_expert_w[l]], axis=1)
    wr = jnp.pad(wr, ((0, 0), (0, LANES - wr.shape[1])))
    lw["wr_hi"] = wr.astype(BF16)
    lw["wr_lo"] = (wr - lw["wr_hi"].astype(F32)).astype(BF16)
    br = jnp.concatenate([router_group_b[l], router_expert_b[l]])
    lw["br"] = jnp.pad(br, (0, LANES - br.shape[0]))[None, :]
    return lw


def kernel(x, c, ctx, c_ctx, w_ada, b_ada, w_in, na_bias, gqa_q_gain, gqa_k_gain, win_sink, mla_q_gain, mla_w_qb, mla_kv_gain, mla_w_kvb, w_out, ln1_g, ln1_b, router_group_w, router_group_b, router_expert_w, router_expert_b, moe_w_gate, moe_w_up, moe_w_down, ln2_g, ln2_b):
    bsz, t, d = x.shape
    tc = ctx.shape[1]
    depth = w_ada.shape[0]
    rows = t // GRID_W
    assert t % (2 * GRID_W) == 0 and rows >= NBR_SPAN_ROWS and rows // 2 >= 5
    assert t >= BLOCK + 2 * WINDOW and t % ROW_TILE == 0 and tc % (8 * ROW_SPLIT) == 0 and t % tc == 0
    assert tc <= ROW_TILE and t % OUT_TILE == 0
    assert t % (LOCAL_NSUB * BLOCK) == 0 and t % GLOBAL_BQ == 0
    s_tot = t + tc
    alpha = (2.0 * depth) ** 0.25

    ada_rows = -(-(bsz + 1) // 8) * 8
    cs = jnp.zeros((ada_rows, d), F32).at[:bsz].set(c).at[bsz].set(c_ctx)
    mod = _ada(cs, w_ada, b_ada)
    tabs = _rope_tables(t)

    xl = x.reshape(bsz * t, d)
    xc = ctx.reshape(bsz * tc, d)
    fin_l = fin_c = None

    for l in range(depth):
        with_ctx = l < depth - 1
        lw = _layer_weights(l, w_in, gqa_q_gain, gqa_k_gain, mla_q_gain, mla_w_qb, mla_kv_gain,
                            mla_w_kvb, w_out, ln1_g, ln1_b, router_group_w, router_group_b,
                            router_expert_w, router_expert_b)
        ln2g, ln2b = ln2_g[l][None, :], ln2_b[l][None, :]

        def mods(lo, hi):
            parts = jnp.split(mod[l, lo:hi], 6, axis=-1)
            return [jnp.broadcast_to(p[:, None, :], (bsz, 1, d)) for p in parts]
        sh1, sc1, g1, sh2, sc2, g2 = mods(0, bsz)
        sh1c, sc1c, g1c, sh2c, sc2c, g2c = mods(bsz, bsz + 1)

        xl, pl_lat = _project(xl, fin_l, sc1, sh1, lw, tabs, t, s_tot, 0)
        xc, pl_ctx = _project(xc, fin_c, sc1c, sh1c, lw, None, tc, s_tot, t,
                              kv_prev=[pl_lat[i] for i in _KV_OUTS])
        qa, qb, qc, qd = [pl_lat[i].reshape(bsz, t, -1) for i in _Q_OUTS]
        qa_c, qb_c, qc_c, qd_c = [pl_ctx[i].reshape(bsz, tc, -1) for i in _Q_OUTS]
        ka, va, kb, vb, kc, vc, kd, vd = [pl_ctx[i] for i in _KV_OUTS]

        bias_tab = _nbr_bias_tables(na_bias[l], rows)
        everything = dict(parts=_key_ranges(s_tot), bq=GLOBAL_BQ, single_shot=False)
        out_a = _attention("mha", qa, ka, va, [("nbr", 0, t), ("full", t, tc)], bias=bias_tab, nsub=LOCAL_NSUB)
        out_b = _attention("gqa", qb, kb, vb, **everything)
        out_c = _attention("gqa", qc, kc, vc, [("win", 0, t), ("full", t, tc)], sink=win_sink[l],
                           nsub=LOCAL_NSUB)
        out_d = _attention("mla", qd, kd, vd, **everything)
        outs = [o.reshape(bsz * t, 256) for o in (out_a, out_b, out_c, out_d)]

        n_lat, n_ctx = bsz * t, bsz * tc
        n_moe = n_lat + n_ctx if with_ctx else n_lat
        x1, h2x, ids = _out_proj(alpha, outs, xl, g1, sc2, sh2, lw, t, n_moe, 0)
        if with_ctx:
            ctx_only = dict(parts=[("full", 0, tc)], kv_block=(tc, t // tc))
            outs_c = _attention_run([
                _attention_plan("mha", qa_c, ka, va, **ctx_only),
                _attention_plan("gqa", qb_c, kb, vb, **ctx_only),
                _attention_plan("gqa", qc_c, kc, vc, sink=win_sink[l], **ctx_only),
                _attention_plan("mla", qd_c, kd, vd, **ctx_only),
            ], bsz, tc)
            outs_c = [o.reshape(n_ctx, 256) for o in outs_c]
            x1c, h2x, ids = _out_proj(alpha, outs_c, xc, g1c, sc2c, sh2c, lw, tc, n_moe, n_lat, prev=(h2x, ids))
        y = _moe(l, h2x, ids, moe_w_gate, moe_w_up, moe_w_down, MOE_CHUNKS,
                 (n_lat, n_ctx) if with_ctx else (n_lat,))
        fin_l = (alpha, x1, y, g2, ln2g, ln2b, 0)
        if with_ctx:
            fin_c = (alpha, x1c, y, g2c, ln2g, ln2b, n_lat)

    return _finish(*fin_l[:6], t).reshape(bsz, t, d)
```

```python
import functools

import jax
import jax.numpy as jnp
from jax import lax
from jax.experimental import pallas as pl
from jax.experimental.pallas import tpu as pltpu

F32 = jnp.float32
BF16 = jnp.bfloat16

GRID_W = 64
HEAD_DIM = 64
BLOCK = 128
WINDOW = 128
ROPE_THETA = 10000.0
NEG = -1e30
EPS = 1e-6
LOG2E = 1.4426950408889634

NA_ROWS = 8
NA_COLS = 16
NBR_SPAN_ROWS = 10
MLA_NOPE = 64
MLA_ROPE = 32
MLA_QK = MLA_NOPE + MLA_ROPE
N_GROUPS = 4
EXPERTS_PER_GROUP = 4
N_EXPERTS = 16
N_PAIRS = 6
N_BUCKETS = N_GROUPS * N_PAIRS
ROUTE_LANES = 8

LANES = 128
VMEM_LIMIT = 48 * 1024 * 1024

ROW_TILE = 512
ROW_SPLIT = 2
OUT_TILE = 1024
MOE_TILE = 256
MOE_CHUNKS = 4
KV_CHUNK = 2560
GLOBAL_BQ = 512
LOCAL_NSUB = 8
H2_EXT = 128


def _cparams(*sem):
    return pltpu.CompilerParams(dimension_semantics=sem, vmem_limit_bytes=VMEM_LIMIT)


def _dot(a, b):
    return jnp.dot(a, b, preferred_element_type=F32)


def _dot_nt(a, b):
    return lax.dot_general(a, b, (((1,), (1,)), ((), ())), preferred_element_type=F32)


def _split(a):
    hi = a.astype(BF16)
    lo = (a - hi.astype(F32)).astype(BF16)
    return hi, lo


def _dot3(a, b):
    ah, al = _split(a)
    bh, bl = _split(b)
    return _dot(al, bh) + _dot(ah, bl) + _dot(ah, bh)


def _ln(x):
    mu = jnp.mean(x, axis=-1, keepdims=True)
    xc = x - mu
    var = jnp.mean(xc * xc, axis=-1, keepdims=True)
    return xc * lax.rsqrt(var + EPS)


def _silu(g):
    return g / (1.0 + jnp.exp(-g))


def _ada_kernel(c_ref, w_ref, b_ref, o_ref):
    o_ref[0] = _dot3(_silu(c_ref[...]), w_ref[0]) + b_ref[0]


def _ada(cs, w_ada, b_ada):
    depth, d, n = w_ada.shape
    rows = cs.shape[0]
    tn = 1536
    return pl.pallas_call(
        _ada_kernel,
        grid=(depth, n // tn),
        in_specs=[pl.BlockSpec((rows, d), lambda l, j: (0, 0)),
                  pl.BlockSpec((1, d, tn), lambda l, j: (l, 0, j)),
                  pl.BlockSpec((1, 1, tn), lambda l, j: (l, 0, j))],
        out_specs=pl.BlockSpec((1, rows, tn), lambda l, j: (l, 0, j)),
        out_shape=jax.ShapeDtypeStruct((depth, rows, n), F32),
        compiler_params=_cparams("parallel", "parallel"),
    )(cs, w_ada, b_ada.reshape(depth, 1, n))


def _head_rms(x, bd, gain):
    hi, lo = _split(x * x)
    ss = _dot(lo, bd) + _dot(hi, bd)
    return x * lax.rsqrt(ss * (1.0 / HEAD_DIM) + EPS) * gain


def _rms(x, gain):
    return x * lax.rsqrt(jnp.mean(x * x, axis=-1, keepdims=True) + EPS) * gain


def _rope(x, tab_ref, shift):
    outs = []
    for j in range(x.shape[1] // LANES):
        xb = x[:, j * LANES:(j + 1) * LANES]
        outs.append(xb * tab_ref[0]
                    + pltpu.roll(xb, shift, 1) * tab_ref[1]
                    + pltpu.roll(xb, LANES - shift, 1) * tab_ref[2])
    return outs[0] if len(outs) == 1 else jnp.concatenate(outs, axis=1)


def _value_tiles(v):
    lane = lax.broadcasted_iota(jnp.int32, (v.shape[0], LANES), 1)
    fill = jnp.where(lane == HEAD_DIM, 1.0, 0.0)
    outs = []
    for j in range(v.shape[1] // LANES):
        blk = v[:, j * LANES:(j + 1) * LANES]
        outs.append(jnp.where(lane < HEAD_DIM, blk, fill))
        outs.append(jnp.where(lane < HEAD_DIM, pltpu.roll(blk, HEAD_DIM, 1), fill))
    return jnp.concatenate(outs, axis=1).astype(BF16)


def _proj_kernel(rope, fin_alpha, n_aliased, *refs):
    refs = list(refs)
    if fin_alpha is None:
        x_ref = refs.pop(0)
    else:
        x1_ref, y_ref, g2_ref, lng_ref, lnb_ref = refs[:5]
        refs = refs[5:]
    (sc_ref, sh_ref, w_ref, gq_ref, gk_ref, gmq_ref, gmkv_ref,
     wqb_ref, wkvb_ref, bd_ref) = refs[:10]
    refs = refs[10:]
    tabs = None
    if rope:
        tabs = refs[:3]
        refs = refs[3:]
    refs = refs[n_aliased:]
    if fin_alpha is not None:
        xo_ref = refs.pop(0)
    refs = [r.at[0] if len(r.shape) == 3 else r for r in refs]
    group = refs[0].shape[0] // ROW_SPLIT
    for u in range(ROW_SPLIT):
        rows = slice(u * group, (u + 1) * group)
        if fin_alpha is None:
            x = x_ref[rows, :]
        else:
            x = (_ln(fin_alpha * x1_ref[rows, :] + g2_ref[0] * y_ref[rows, :]) * lng_ref[...]
                 + lnb_ref[...])
            xo_ref[rows, :] = x
        _proj_rows(rope, x, rows, sc_ref, sh_ref, w_ref, gq_ref, gk_ref, gmq_ref, gmkv_ref, wqb_ref,
                   wkvb_ref, bd_ref, tabs, refs)


def _proj_rows(rope, x, rows, sc_ref, sh_ref, w_ref, gq_ref, gk_ref, gmq_ref, gmkv_ref, wqb_ref, wkvb_ref,
               bd_ref, tabs, outs):
    qa, ka, va, qb, kb, vb, qc, kc, vc, qd, kd, vd = outs
    if rope:
        t64_ref, tq_ref, tk_ref = [t.at[:, rows, :] for t in tabs]
    h = _ln(x) * (1.0 + sc_ref[0]) + sh_ref[0]
    hb = h.astype(BF16)
    sq = HEAD_DIM ** -0.5 * LOG2E

    def proj(a, b):
        return _dot(hb, w_ref[:, a:b])

    pa = proj(0, 768)
    qa[rows, :] = (pa[:, 0:256] * sq).astype(BF16)
    ka[rows, :] = pa[:, 256:512].astype(BF16)
    va[rows, :] = _value_tiles(pa[:, 512:768])

    pb = proj(768, 1280)
    q = _head_rms(pb[:, 0:256], bd_ref[...], gq_ref[...])
    k = _head_rms(pb[:, 256:384], bd_ref[0:128, 0:128], gk_ref[...])
    if rope:
        q = _rope(q, t64_ref, 16)
        k = _rope(k, t64_ref, 16)
    qb[rows, :] = (q * sq).astype(BF16)
    kb[rows, :] = k.astype(BF16)
    vb[rows, :] = _value_tiles(pb[:, 384:512])

    pc = proj(1280, 1792)
    q = pc[:, 0:256]
    k = pc[:, 256:384]
    if rope:
        q = _rope(q, t64_ref, 16)
        k = _rope(k, t64_ref, 16)
    qc[rows, :] = (q * sq).astype(BF16)
    kc[rows, :] = k.astype(BF16)
    vc[rows, :] = _value_tiles(pc[:, 384:512])

    pd = proj(1792, 2304)
    cq = _rms(pd[:, 0:256], gmq_ref[...])
    q = _dot(cq.astype(BF16), wqb_ref[...])
    if rope:
        q = _rope(q, tq_ref, 8)
    qd[rows, :] = (q * (MLA_QK ** -0.5 * LOG2E)).astype(BF16)
    ckv = _rms(pd[:, 256:384], gmkv_ref[...])
    kvu = _dot(ckv.astype(BF16), wkvb_ref[...])
    kr = pd[:, 384:512]
    if rope:
        kr = _rope(kr, tk_ref, 8)
    kr = pltpu.roll(kr, MLA_NOPE, 1)
    kd[rows, :] = jnp.concatenate(
        [kvu[:, j * LANES:(j + 1) * LANES] + kr for j in range(4)], axis=1).astype(BF16)
    lane = lax.broadcasted_iota(jnp.int32, (kr.shape[0], LANES), 1)
    fill = jnp.where(lane == HEAD_DIM, 1.0, 0.0)
    vd[rows, :] = jnp.concatenate(
        [kvu[:, (4 + j) * LANES:(5 + j) * LANES] + fill for j in range(4)], axis=1).astype(BF16)


_PROJ_WIDTHS = (256, 256, 512, 256, 128, 256, 256, 128, 256, 512, 512, 512)


_Q_OUTS = (0, 3, 6, 9)
_KV_OUTS = (1, 2, 4, 5, 7, 8, 10, 11)


def _project(x2d, fin, sc, sh, lw, tabs, tokens_per_batch, kv_rows, kv_row0, kv_prev=None):
    n, d = (x2d if fin is None else fin[1]).shape
    tm = min(ROW_TILE, tokens_per_batch)
    per_b = tokens_per_batch // tm
    bsz = n // tokens_per_batch
    rope = tabs is not None
    row = lambda i: (i, 0)
    full = lambda i: (0, 0)
    perb = lambda i: (i // per_b, 0, 0)
    kvrow = lambda i: (i // per_b, kv_row0 // tm + i % per_b, 0)
    if fin is None:
        in_specs = [pl.BlockSpec((tm, d), row)]
        args = [x2d]
    else:
        in_specs = [pl.BlockSpec((tm, d), row), pl.BlockSpec((tm, d), lambda i: (i + fin[6] // tm, 0)),
                    pl.BlockSpec((1, 1, d), perb), pl.BlockSpec((1, d), full), pl.BlockSpec((1, d), full)]
        args = list(fin[1:6])
    in_specs += [pl.BlockSpec((1, 1, d), perb), pl.BlockSpec((1, 1, d), perb),
                 pl.BlockSpec(lw["w_in"].shape, full),
                 pl.BlockSpec((1, 256), full), pl.BlockSpec((1, 128), full),
                 pl.BlockSpec((1, 256), full), pl.BlockSpec((1, 128), full),
                 pl.BlockSpec(lw["w_qb"].shape, full), pl.BlockSpec(lw["w_kvb"].shape, full),
                 pl.BlockSpec((256, 256), full)]
    args += [sc, sh, lw["w_in"], lw["gq"], lw["gk"], lw["gmq"], lw["gmkv"],
             lw["w_qb"], lw["w_kvb"], lw["bd"]]
    if rope:
        tab = lambda i: (0, i % per_b, 0)
        in_specs += [pl.BlockSpec((3, tm, LANES), tab)] * 3
        args += list(tabs)
    out_specs = [pl.BlockSpec((1, tm, w), kvrow) if i in _KV_OUTS else pl.BlockSpec((tm, w), row)
                 for i, w in enumerate(_PROJ_WIDTHS)]
    out_shape = [jax.ShapeDtypeStruct((bsz, kv_rows, w) if i in _KV_OUTS else (n, w), BF16)
                 for i, w in enumerate(_PROJ_WIDTHS)]
    lead = 0 if fin is None else 1
    if fin is not None:
        out_specs.insert(0, pl.BlockSpec((tm, d), row))
        out_shape.insert(0, jax.ShapeDtypeStruct((n, d), F32))
    aliases = {}
    if kv_prev is not None:
        for a, i in zip(kv_prev, _KV_OUTS):
            aliases[len(args)] = lead + i
            in_specs.append(pl.BlockSpec(memory_space=pl.ANY))
            args.append(a)
    outs = pl.pallas_call(
        functools.partial(_proj_kernel, rope, None if fin is None else fin[0], len(aliases)),
        grid=(n // tm,),
        in_specs=in_specs,
        out_specs=out_specs,
        out_shape=out_shape,
        input_output_aliases=aliases,
        compiler_params=_cparams("parallel"),
    )(*args)
    return (x2d, list(outs)) if fin is None else (outs[0], list(outs[1:]))


def _softmax_step(s, vts, bq, m, acc):
    m_new = jnp.maximum(m, jnp.max(s, axis=1, keepdims=True))
    alpha = jnp.exp2(m - m_new)
    p = jnp.exp2(s - m_new).astype(BF16)
    if len(vts) == 1:
        pv = _dot(p, vts[0])
    else:
        pv = jnp.concatenate([_dot(p[:bq], vts[0]), _dot(p[bq:], vts[1])], axis=0)
    return [m_new, alpha * acc + pv]


def _attn_kernel(mode, bq, nsub, srcs, single_shot, has_bias, has_sink, *refs):
    refs = list(refs)
    q_ref = refs.pop(0)
    k_ref, v_ref = refs.pop(0), refs.pop(0)
    bias_refs = [refs.pop(0) for _ in range(nsub)] if has_bias else None
    sink_ref = refs.pop(0) if has_sink else None
    o_ref = refs.pop(0)

    for u in range(nsub):
        qi = pl.program_id(1) * nsub + u
        rows = slice(u * bq, (u + 1) * bq)
        rows2 = 2 * bq
        lo = lax.broadcasted_iota(jnp.int32, (bq, LANES), 1) < HEAD_DIM

        qss, kcs, vcs = [], [], []
        for j in range(2):
            if mode == "mla":
                qblk = q_ref[0, rows, 2 * LANES * j:2 * LANES * (j + 1)].astype(F32)
                first = lax.broadcasted_iota(jnp.int32, (bq, 2 * LANES), 1) < LANES
                qs = jnp.concatenate([jnp.where(first, qblk, 0.0), jnp.where(first, 0.0, qblk)], axis=0)
                kcs.append((2 * LANES * j, 2 * LANES * (j + 1)))
                vcs.append((2 * LANES * j, 2 * LANES * j + LANES))
            else:
                qblk = q_ref[0, rows, LANES * j:LANES * (j + 1)].astype(F32)
                if mode == "mha":
                    qe, qo = jnp.where(lo, qblk, 0.0), jnp.where(lo, 0.0, qblk)
                    kcs.append((LANES * j, LANES * (j + 1)))
                    vcs.append((2 * LANES * j, 2 * LANES * j + LANES))
                else:
                    rolled = pltpu.roll(qblk, HEAD_DIM, 1)
                    if j == 0:
                        qe, qo = jnp.where(lo, qblk, 0.0), jnp.where(lo, rolled, 0.0)
                    else:
                        qe, qo = jnp.where(lo, 0.0, rolled), jnp.where(lo, 0.0, qblk)
                    kcs.append((0, LANES))
                    vcs.append((LANES * j,))
                qs = jnp.concatenate([qe, qo], axis=0)
            qss.append(qs.astype(BF16))

        def pv(p, vts):
            if len(vts) == 1:
                return _dot(p, vts[0])
            return jnp.concatenate([_dot(p[:bq], vts[0]), _dot(p[bq:], vts[1])], axis=0)

        def tiles(j, off, n):
            kt = k_ref[0, pl.ds(off, n), kcs[j][0]:kcs[j][1]]
            return kt, [v_ref[0, pl.ds(off, n), v0:v0 + LANES] for v0 in vcs[j]]

        def block_scores(j, kind, row0, nrows):
            if kind == "full":
                kt, vts = tiles(j, row0, nrows)
                return _dot_nt(qss[j], kt), vts
            if kind == "win":
                span = bq + 2 * WINDOW
                start = pl.multiple_of(jnp.clip((qi - 1) * bq, 0, nrows - span), bq)
                r = lax.broadcasted_iota(jnp.int32, (rows2, span), 0)
                qpos = qi * bq + jnp.where(r >= bq, r - bq, r)
                kpos = start + lax.broadcasted_iota(jnp.int32, (rows2, span), 1)
                kt, vts = tiles(j, row0 + start, span)
                return jnp.where(jnp.abs(qpos - kpos) <= WINDOW, _dot_nt(qss[j], kt), NEG), vts
            span = NBR_SPAN_ROWS * GRID_W
            ks = jnp.clip(2 * qi - NA_ROWS // 2, 0, nrows // GRID_W - NBR_SPAN_ROWS)
            kt, vts = tiles(j, row0 + pl.multiple_of(ks * GRID_W, GRID_W), span)
            return _dot_nt(qss[j], kt) + bias_refs[u][0, j], vts

        if single_shot:
            state = []
            for j in range(2):
                parts = [block_scores(j, kind, row0, nrows) for kind, row0, nrows in srcs]
                m = functools.reduce(jnp.maximum, [jnp.max(sc, axis=1, keepdims=True) for sc, _ in parts])
                acc = functools.reduce(lambda a, b: a + b,
                                       [pv(jnp.exp2(sc - m).astype(BF16), vts) for sc, vts in parts])
                state += [m, acc]
        else:
            state = [jnp.full((rows2, 1), NEG, F32), jnp.zeros((rows2, LANES), F32)] * 2
            for kind, row0, nrows in srcs:
                for j in range(2):
                    sc, vts = block_scores(j, kind, row0, nrows)
                    state[2 * j:2 * j + 2] = _softmax_step(sc, vts, bq, state[2 * j], state[2 * j + 1])

        for j in range(2):
            m, acc = state[2 * j], state[2 * j + 1]
            l = acc[:, HEAD_DIM:HEAD_DIM + 1]
            if has_sink:
                sk = jnp.concatenate([jnp.full((bq, 1), sink_ref[2 * j] * LOG2E, F32),
                                      jnp.full((bq, 1), sink_ref[2 * j + 1] * LOG2E, F32)], axis=0)
                mf = jnp.maximum(m, sk)
                a = jnp.exp2(m - mf)
                l = l * a + jnp.exp2(sk - mf)
                acc = acc * a
            o = acc / l
            o_ref[0, rows, LANES * j:LANES * (j + 1)] = jnp.where(
                lo, o[:bq], pltpu.roll(o[bq:], HEAD_DIM, 1)).astype(BF16)


def _key_ranges(nrows):
    n = -(-nrows // KV_CHUNK)
    size = max(nrows // n // 256, 1) * 256
    starts = [i * size for i in range(n)]
    return [("full", r0, (nrows if i == n - 1 else r0 + size) - r0) for i, r0 in enumerate(starts)]


def _attention_plan(mode, q, k, v, parts, kv_block=None, bias=None, sink=None, bq=BLOCK, nsub=1,
                    single_shot=True):
    b, tq, wq = q.shape
    srcs = tuple(parts)
    rows, blk = (k.shape[1], 0) if kv_block is None else kv_block
    in_specs = [pl.BlockSpec((1, nsub * bq, wq), lambda bi, qi: (bi, qi, 0)),
                pl.BlockSpec((1, rows, k.shape[2]), lambda bi, qi: (bi, blk, 0)),
                pl.BlockSpec((1, rows, v.shape[2]), lambda bi, qi: (bi, blk, 0))]
    args = [q, k, v]
    if bias is not None:
        npair = tq // bq

        for u in range(nsub):
            def bias_idx(bi, qi, u=u):
                p = qi * nsub + u
                return (jnp.where(p < 2, p, jnp.where(p >= npair - 2, p - (npair - 2) + 3, 2)), 0, 0, 0)
            in_specs.append(pl.BlockSpec((1,) + bias.shape[1:], bias_idx))
            args.append(bias)
    if sink is not None:
        in_specs.append(pl.BlockSpec(memory_space=pltpu.SMEM))
        args.append(sink)
    body = functools.partial(_attn_kernel, mode, bq, nsub, srcs, single_shot, bias is not None, sink is not None)
    return (body, nsub * bq, in_specs, args, pl.BlockSpec((1, nsub * bq, 256), lambda bi, qi: (bi, qi, 0)),
            jax.ShapeDtypeStruct((b, tq, 256), BF16))


def _attention_run(plans, batch, tq):
    step = plans[0][1]
    assert all(p[1] == step for p in plans)
    counts = [len(p[2]) for p in plans]

    def body(*refs):
        ins, outs = refs[:sum(counts)], refs[sum(counts):]
        at = 0
        for (fn, _, _, _, _, _), n_in, o_ref in zip(plans, counts, outs):
            fn(*ins[at:at + n_in], o_ref)
            at += n_in

    return list(pl.pallas_call(
        body,
        grid=(batch, tq // step),
        in_specs=[sp for p in plans for sp in p[2]],
        out_specs=[p[4] for p in plans],
        out_shape=[p[5] for p in plans],
        compiler_params=_cparams("parallel", "arbitrary"),
    )(*[a for p in plans for a in p[3]]))


def _attention(mode, q, k, v, parts, **kw):
    return _attention_run([_attention_plan(mode, q, k, v, parts, **kw)], q.shape[0], q.shape[1])[0]


def _route(logits):
    lane = lax.broadcasted_iota(jnp.int32, logits.shape, 1).astype(F32)
    big = float(LANES)
    is_g = lane < N_GROUPS
    lg = jnp.where(is_g, logits, NEG)
    gmax = jnp.max(lg, axis=1, keepdims=True)
    grp = jnp.min(jnp.where(lg == gmax, lane, big), axis=1, keepdims=True)
    den = jnp.sum(jnp.where(is_g, jnp.exp(lg - gmax), 0.0), axis=1, keepdims=True)
    p_grp = 1.0 / den
    e0 = N_GROUPS + EXPERTS_PER_GROUP * grp
    in_grp = (lane >= e0) & (lane < e0 + EXPERTS_PER_GROUP)
    le = jnp.where(in_grp, logits, NEG)
    v1 = jnp.max(le, axis=1, keepdims=True)
    i1 = jnp.min(jnp.where(in_grp & (le == v1), lane, big), axis=1, keepdims=True)
    rest = in_grp & (lane != i1)
    le2 = jnp.where(rest, logits, NEG)
    v2 = jnp.max(le2, axis=1, keepdims=True)
    i2 = jnp.min(jnp.where(rest & (le2 == v2), lane, big), axis=1, keepdims=True)
    t = jnp.exp(v2 - v1)
    w1 = p_grp / (1.0 + t)
    w2 = p_grp * t / (1.0 + t)
    first_lower = i1 < i2
    e_lo = jnp.where(first_lower, i1, i2) - N_GROUPS
    e_hi = jnp.where(first_lower, i2, i1) - N_GROUPS
    w_lo = jnp.where(first_lower, w1, w2)
    w_hi = jnp.where(first_lower, w2, w1)
    a = e_lo - EXPERTS_PER_GROUP * grp
    b = e_hi - EXPERTS_PER_GROUP * grp
    pair = a * (2 * EXPERTS_PER_GROUP - 1 - a) * 0.5 + (b - a - 1.0)
    pair = jnp.where(pair == 3.0, 4.0, jnp.where(pair == 4.0, 3.0, pair))
    bucket = grp * N_PAIRS + pair
    return jnp.where(lane == 0, e_lo, jnp.where(lane == 1, e_hi, jnp.where(lane == 2, w_lo,
                     jnp.where(lane == 3, w_hi, jnp.where(lane == 4, bucket, 0.0)))))


def _out_kernel(alpha, oa, ob, oc, od, x_ref, g1_ref, sc2_ref, sh2_ref, w_ref, lng_ref, lnb_ref,
                wrh_ref, wrl_ref, br_ref, *rest):
    x1_ref, h2_ref, ids_ref = rest[-3:]
    mix = (_dot(oa[...], w_ref[0:256, :]) + _dot(ob[...], w_ref[256:512, :])
           + _dot(oc[...], w_ref[512:768, :]) + _dot(od[...], w_ref[768:1024, :]))
    x1 = _ln(alpha * x_ref[...] + g1_ref[0] * mix) * lng_ref[...] + lnb_ref[...]
    x1_ref[...] = x1
    h2 = _ln(x1) * (1.0 + sc2_ref[0]) + sh2_ref[0]
    hh, hl = _split(h2)
    logits = _dot(hl, wrh_ref[...]) + _dot(hh, wrl_ref[...]) + _dot(hh, wrh_ref[...]) + br_ref[...]
    rh, rl = _split(_route(logits))
    d = h2.shape[1]
    h2_ref[:, 0:d] = hh
    lane = lax.broadcasted_iota(jnp.int32, rh.shape, 1)
    h2_ref[:, d:d + LANES] = jnp.where(lane < ROUTE_LANES, rh.astype(F32),
                                       pltpu.roll(rl.astype(F32), ROUTE_LANES, 1)).astype(BF16)
    pick = ((lax.broadcasted_iota(jnp.int32, (8, LANES), 0) == 0)
            & (lax.broadcasted_iota(jnp.int32, (8, LANES), 1) == 4)).astype(BF16)
    ids_ref[...] = _dot_nt(pick, rh)


def _out_proj(alpha, outs, x2d, g1, sc2, sh2, lw, tokens_per_batch, n_rows, row0, prev=None):
    n, d = x2d.shape
    tm = min(OUT_TILE, tokens_per_batch)
    per_b = tokens_per_batch // tm
    row = lambda i: (i, 0)
    full = lambda i: (0, 0)
    perb = lambda i: (i // per_b, 0, 0)
    in_specs = [pl.BlockSpec((tm, 256), row)] * 4 + [
        pl.BlockSpec((tm, d), row),
        pl.BlockSpec((1, 1, d), perb), pl.BlockSpec((1, 1, d), perb), pl.BlockSpec((1, 1, d), perb),
        pl.BlockSpec((d, d), full), pl.BlockSpec((1, d), full), pl.BlockSpec((1, d), full),
        pl.BlockSpec((d, LANES), full), pl.BlockSpec((d, LANES), full), pl.BlockSpec((1, LANES), full)]
    args = [*outs, x2d, g1, sc2, sh2, lw["w_out"], lw["ln1_g"], lw["ln1_b"], lw["wr_hi"], lw["wr_lo"], lw["br"]]
    aliases = {}
    if prev is not None:
        for k, a in enumerate(prev):
            aliases[len(args)] = 1 + k
            in_specs.append(pl.BlockSpec(memory_space=pl.ANY))
            args.append(a)
    return pl.pallas_call(
        functools.partial(_out_kernel, alpha),
        grid=(n // tm,),
        in_specs=in_specs,
        out_specs=[pl.BlockSpec((tm, d), row), pl.BlockSpec((tm, d + H2_EXT), lambda i: (i + row0 // tm, 0)),
                   pl.BlockSpec((8, tm), lambda i: (0, i + row0 // tm))],
        out_shape=[jax.ShapeDtypeStruct((n, d), F32), jax.ShapeDtypeStruct((n_rows, d + H2_EXT), BF16),
                   jax.ShapeDtypeStruct((8, n_rows), F32)],
        input_output_aliases=aliases,
        compiler_params=_cparams("parallel"),
    )(*args)


def _moe_kernel(tile0, e0_ref, e1_ref, valid_ref, xs_ref, wg0, wg1, wu0, wu1, wd0, wd1, *rest):
    o_ref, wb = rest[-7], rest[-6:]
    t = pl.program_id(0) + tile0
    d = o_ref.shape[1]
    prev = jnp.maximum(t - 1, 0)
    first = pl.program_id(0) == 0

    @pl.when(first | (e0_ref[t] != e0_ref[prev]))
    def _():
        for src, dst in zip((wg0, wu0, wd0), wb[0::2]):
            dst[...] = src[0, 0].astype(BF16)

    @pl.when(first | (e1_ref[t] != e1_ref[prev]))
    def _():
        for src, dst in zip((wg1, wu1, wd1), wb[1::2]):
            dst[...] = src[0, 0].astype(BF16)

    @pl.when(valid_ref[t] == 1)
    def _():
        x = xs_ref[:, 0:d]
        ext = xs_ref[:, d:d + LANES].astype(F32)
        rt = ext + pltpu.roll(ext, LANES - ROUTE_LANES, 1)
        lower_first = e0_ref[t] < e1_ref[t]
        w0 = jnp.where(lower_first, rt[:, 2:3], rt[:, 3:4])
        w1 = jnp.where(lower_first, rt[:, 3:4], rt[:, 2:3])

        def expert(wg, wu, wd):
            a = _silu(_dot(x, wg[...])) * _dot(x, wu[...])
            return _dot(a.astype(BF16), wd[...])

        y = w0 * expert(wb[0], wb[2], wb[4]) + w1 * expert(wb[1], wb[3], wb[5])
        o_ref[...] = y.astype(o_ref.dtype)

    @pl.when(valid_ref[t] == 0)
    def _():
        o_ref[...] = jnp.zeros_like(o_ref)


def _moe_experts(layer, tile0, e0, e1, valid, xs, wg, wu, wd, ys_prev):
    p, dx = xs.shape
    d = dx - H2_EXT
    tm = MOE_TILE
    de = wg.shape[3]
    w0 = lambda t, e0, e1, v: (layer, e0[t + tile0], 0, 0)
    w1 = lambda t, e0, e1, v: (layer, e1[t + tile0], 0, 0)
    in_specs = [pl.BlockSpec((tm, dx), lambda t, e0, e1, v: (t, 0)),
                pl.BlockSpec((1, 1, d, de), w0), pl.BlockSpec((1, 1, d, de), w1),
                pl.BlockSpec((1, 1, d, de), w0), pl.BlockSpec((1, 1, d, de), w1),
                pl.BlockSpec((1, 1, de, d), w0), pl.BlockSpec((1, 1, de, d), w1)]
    args = [e0, e1, valid, xs, wg, wg, wu, wu, wd, wd]
    aliases = {}
    if ys_prev is not None:
        aliases[len(args)] = 0
        in_specs.append(pl.BlockSpec(memory_space=pl.ANY))
        args.append(ys_prev)
    return pl.pallas_call(
        functools.partial(_moe_kernel, tile0),
        grid_spec=pltpu.PrefetchScalarGridSpec(
            num_scalar_prefetch=3,
            grid=(p // tm,),
            in_specs=in_specs,
            out_specs=pl.BlockSpec((tm, d), lambda t, e0, e1, v: (t + tile0, 0)),
            scratch_shapes=[pltpu.VMEM((d, de), BF16)] * 4 + [pltpu.VMEM((de, d), BF16)] * 2),
        out_shape=jax.ShapeDtypeStruct((e0.shape[0] * tm, d), BF16),
        input_output_aliases=aliases,
        compiler_params=_cparams("arbitrary"),
    )(*args)


_PAIR_TABLE = ((0, 1), (0, 2), (0, 3), (1, 3), (1, 2), (3, 2))


def _moe(layer, h2x, ids, wg, wu, wd, chunks, segments):
    n = h2x.shape[0]
    tm = MOE_TILE
    n_tiles = -(-(n // tm + N_BUCKETS) // chunks) * chunks
    bucket = ids[0].astype(jnp.int32)
    bids = jnp.arange(N_BUCKETS + 1, dtype=jnp.int32)
    onehot_b = bucket[None, :] == bids[:N_BUCKETS, None]
    per_token = lambda table: jnp.sum(jnp.where(onehot_b, table[:, None], 0), axis=0)

    seg = []
    within_tok = []
    start = 0
    for size in segments:
        tok = jnp.arange(size, dtype=jnp.int32)
        sorted_bucket, order = lax.sort((bucket[start:start + size], tok), num_keys=1)
        _, rank = lax.sort((order, tok), num_keys=1)
        edges = jnp.sum(sorted_bucket[None, :] < bids[:, None], axis=1, dtype=jnp.int32)
        seg.append((start, order, edges[:-1], edges[1:] - edges[:-1]))
        within_tok.append(rank)
        start += size
    counts = sum(c for _, _, _, c in seg)
    tiles_per = (counts + tm - 1) // tm
    tile_end = jnp.cumsum(tiles_per)
    tile_start = tile_end - tiles_per

    tile_ids = jnp.arange(n_tiles, dtype=jnp.int32)
    valid = (tile_ids < tile_end[-1]).astype(jnp.int32)
    tile_bucket = jnp.sum(tile_ids[:, None] >= tile_end[None, :], axis=1, dtype=jnp.int32)
    last_bucket = jnp.sum(tile_end[-1] - 1 >= tile_end, dtype=jnp.int32)
    tile_bucket = jnp.minimum(jnp.where(valid == 1, tile_bucket, last_bucket), N_BUCKETS - 1)
    onehot_tb = tile_bucket[:, None] == bids[None, :N_BUCKETS]
    pick = lambda table: jnp.sum(jnp.where(onehot_tb, table[None, :], 0), axis=1)
    pairs = jnp.asarray(_PAIR_TABLE, dtype=jnp.int32)
    tg = tile_bucket // N_PAIRS
    e0 = tg * EXPERTS_PER_GROUP + pairs[tile_bucket % N_PAIRS, 0]
    e1 = tg * EXPERTS_PER_GROUP + pairs[tile_bucket % N_PAIRS, 1]

    first_row = (tile_ids - pick(tile_start)) * tm
    within = first_row[:, None] + jnp.arange(tm, dtype=jnp.int32)[None, :]
    row_ok = (within < pick(counts)[:, None]) & (valid[:, None] == 1)
    order_all = jnp.concatenate([start + order for start, order, _, _ in seg])
    idx = jnp.zeros(within.shape, jnp.int32)
    before = jnp.zeros((N_BUCKETS,), jnp.int32)
    base = []
    for (start, order, off, cnt), rank in zip(seg, within_tok):
        j = within - pick(before)[:, None]
        mine = (j >= 0) & (j < pick(cnt)[:, None])
        idx = jnp.where(mine, start + pick(off)[:, None] + j, idx)
        base.append((tile_start * tm + before - off, start, rank))
        before = before + cnt
    idx = jnp.clip(idx, 0, n - 1).reshape(-1)
    src = jnp.where(row_ok.reshape(-1), order_all.at[idx].get(mode="promise_in_bounds"), 0)

    per_chunk = n_tiles // chunks
    ys = None
    for c in range(chunks):
        xs = h2x.at[src[c * per_chunk * tm:(c + 1) * per_chunk * tm]].get(mode="promise_in_bounds")
        ys = _moe_experts(layer, c * per_chunk, e0, e1, valid, xs, wg, wu, wd, ys)

    pos = jnp.concatenate([per_token(table)[start:start + rank.shape[0]] + rank for table, start, rank in base])
    return ys.at[pos].get(mode="promise_in_bounds")


def _fin_kernel(alpha, x_ref, y_ref, g2_ref, lng_ref, lnb_ref, o_ref):
    o_ref[...] = _ln(alpha * x_ref[...] + g2_ref[0] * y_ref[...]) * lng_ref[...] + lnb_ref[...]


def _finish(alpha, x1, y, g2, ln_g, ln_b, tokens_per_batch):
    n, d = x1.shape
    tm = min(ROW_TILE, tokens_per_batch)
    per_b = tokens_per_batch // tm
    row = lambda i: (i, 0)
    full = lambda i: (0, 0)
    return pl.pallas_call(
        functools.partial(_fin_kernel, alpha),
        grid=(n // tm,),
        in_specs=[pl.BlockSpec((tm, d), row), pl.BlockSpec((tm, d), row),
                  pl.BlockSpec((1, 1, d), lambda i: (i // per_b, 0, 0)),
                  pl.BlockSpec((1, d), full), pl.BlockSpec((1, d), full)],
        out_specs=pl.BlockSpec((tm, d), row),
        out_shape=jax.ShapeDtypeStruct((n, d), F32),
        compiler_params=_cparams("parallel"),
    )(x1, y, g2, ln_g, ln_b)


def _rope_tables(t):
    pos = jnp.arange(t)
    row = (pos // GRID_W).astype(F32)[:, None]
    col = (pos % GRID_W).astype(F32)[:, None]
    lane = jnp.arange(LANES)

    def build(d, half, active):
        axis_col = (d // half) % 2 == 1
        w = d % half
        first = w < half // 2
        f = (w % (half // 2)).astype(F32)
        freq = ROPE_THETA ** (-(2.0 * f) / half)
        ang = jnp.where(axis_col[None, :], col, row) * freq[None, :]
        cos = jnp.where(active[None, :], jnp.cos(ang), 1.0)
        sin = jnp.where(active[None, :], jnp.sin(ang), 0.0)
        return jnp.stack([cos, jnp.where(first[None, :], 0.0, sin), jnp.where(first[None, :], -sin, 0.0)])

    t64 = build(lane % HEAD_DIM, HEAD_DIM // 2, lane >= 0)
    in_rope = (lane >= MLA_NOPE) & (lane < MLA_QK)
    tq = build(jnp.where(in_rope, lane - MLA_NOPE, 0), MLA_ROPE // 2, in_rope)
    in_kr = lane < MLA_ROPE
    tk = build(jnp.where(in_kr, lane, 0), MLA_ROPE // 2, in_kr)
    return t64, tq, tk


def _nbr_bias_tables(na_bias, rows):
    heads = na_bias.shape[0]
    npair = rows // 2
    kr = min(NA_ROWS, rows)
    qcol = jnp.arange(GRID_W)[:, None]
    kcol = jnp.arange(GRID_W)[None, :]
    c0 = jnp.clip(qcol - NA_COLS // 2, 0, GRID_W - NA_COLS)
    col_ok = (kcol >= c0) & (kcol < c0 + NA_COLS)
    col_off = kcol - qcol + NA_COLS - 1
    onehot = (col_off[None] == jnp.arange(2 * NA_COLS - 1)[:, None, None]).astype(F32)
    colmat = jnp.einsum("hdc,cqk->hdqk", na_bias, onehot, precision=lax.Precision.HIGHEST) * LOG2E
    colmat = jnp.where(col_ok[None, None], colmat, NEG)
    outside = jnp.full((heads, GRID_W, GRID_W), NEG, F32)
    tabs = []
    for p in (0, 1, 2, npair - 2, npair - 1):
        ks = min(max(2 * p - NA_ROWS // 2, 0), rows - NBR_SPAN_ROWS)
        per_qrow = []
        for qr in range(2):
            r = 2 * p + qr
            r0 = min(max(r - kr // 2, 0), rows - kr)
            blocks = []
            for kri in range(NBR_SPAN_ROWS):
                krow = ks + kri
                blocks.append(colmat[:, krow - r + NA_ROWS - 1] if r0 <= krow < r0 + kr else outside)
            per_qrow.append(jnp.concatenate(blocks, axis=-1))
        tab = jnp.concatenate(per_qrow, axis=1)
        tabs.append(tab.reshape(heads // 2, 4 * GRID_W, NBR_SPAN_ROWS * GRID_W))
    return jnp.stack(tabs)


def _layer_weights(l, w_in, gqa_q_gain, gqa_k_gain, mla_q_gain, mla_w_qb, mla_kv_gain, mla_w_kvb,
                   w_out, ln1_g, ln1_b, router_group_w, router_group_b, router_expert_w, router_expert_b):
    d = w_in.shape[1]
    d_proj = w_in.shape[2]
    lw = {}
    lw["w_in"] = jnp.pad(w_in[l], ((0, 0), (0, 2304 - d_proj))).astype(BF16)
    lw["gq"] = jnp.tile(gqa_q_gain[l], 4)[None, :]
    lw["gk"] = jnp.tile(gqa_k_gain[l], 2)[None, :]
    lw["gmq"] = mla_q_gain[l][None, :]
    lw["gmkv"] = mla_kv_gain[l][None, :]
    wqb = mla_w_qb[l].reshape(-1, 4, MLA_QK)
    lw["w_qb"] = jnp.pad(wqb, ((0, 0), (0, 0), (0, LANES - MLA_QK))).reshape(-1, 4 * LANES).astype(BF16)
    wkvb = mla_w_kvb[l].reshape(-1, 4, 2 * HEAD_DIM)
    k_part = jnp.pad(wkvb[:, :, :MLA_NOPE], ((0, 0), (0, 0), (0, LANES - MLA_NOPE))).reshape(-1, 4 * LANES)
    v_part = jnp.pad(wkvb[:, :, MLA_NOPE:], ((0, 0), (0, 0), (0, LANES - HEAD_DIM))).reshape(-1, 4 * LANES)
    lw["w_kvb"] = jnp.concatenate([k_part, v_part], axis=1).astype(BF16)
    idx = jnp.arange(256) // HEAD_DIM
    lw["bd"] = (idx[:, None] == idx[None, :]).astype(BF16)
    lw["w_out"] = w_out[l].astype(BF16)
    lw["ln1_g"] = ln1_g[l][None, :]
    lw["ln1_b"] = ln1_b[l][None, :]
    wr = jnp.concatenate([router_group_w[l], router_expert_w[l]], axis=1)
    wr = jnp.pad(wr, ((0, 0), (0, LANES - wr.shape[1])))
    lw["wr_hi"] = wr.astype(BF16)
    lw["wr_lo"] = (wr - lw["wr_hi"].astype(F32)).astype(BF16)
    br = jnp.concatenate([router_group_b[l], router_expert_b[l]])
    lw["br"] = jnp.pad(br, (0, LANES - br.shape[0]))[None, :]
    return lw


def kernel(x, c, ctx, c_ctx, w_ada, b_ada, w_in, na_bias, gqa_q_gain, gqa_k_gain, win_sink, mla_q_gain, mla_w_qb, mla_kv_gain, mla_w_kvb, w_out, ln1_g, ln1_b, router_group_w, router_group_b, router_expert_w, router_expert_b, moe_w_gate, moe_w_up, moe_w_down, ln2_g, ln2_b):
    bsz, t, d = x.shape
    tc = ctx.shape[1]
    depth = w_ada.shape[0]
    rows = t // GRID_W
    assert t % (2 * GRID_W) == 0 and rows >= NBR_SPAN_ROWS and rows // 2 >= 5
    assert t >= BLOCK + 2 * WINDOW and t % ROW_TILE == 0 and tc % (8 * ROW_SPLIT) == 0 and t % tc == 0
    assert tc <= ROW_TILE and t % OUT_TILE == 0
    assert t % (LOCAL_NSUB * BLOCK) == 0 and t % GLOBAL_BQ == 0
    s_tot = t + tc
    alpha = (2.0 * depth) ** 0.25

    ada_rows = -(-(bsz + 1) // 8) * 8
    cs = jnp.zeros((ada_rows, d), F32).at[:bsz].set(c).at[bsz].set(c_ctx)
    mod = _ada(cs, w_ada, b_ada)
    tabs = _rope_tables(t)

    xl = x.reshape(bsz * t, d)
    xc = ctx.reshape(bsz * tc, d)
    fin_l = fin_c = None

    for l in range(depth):
        with_ctx = l < depth - 1
        lw = _layer_weights(l, w_in, gqa_q_gain, gqa_k_gain, mla_q_gain, mla_w_qb, mla_kv_gain,
                            mla_w_kvb, w_out, ln1_g, ln1_b, router_group_w, router_group_b,
                            router_expert_w, router_expert_b)
        ln2g, ln2b = ln2_g[l][None, :], ln2_b[l][None, :]

        def mods(lo, hi):
            parts = jnp.split(mod[l, lo:hi], 6, axis=-1)
            return [jnp.broadcast_to(p[:, None, :], (bsz, 1, d)) for p in parts]
        sh1, sc1, g1, sh2, sc2, g2 = mods(0, bsz)
        sh1c, sc1c, g1c, sh2c, sc2c, g2c = mods(bsz, bsz + 1)

        xl, pl_lat = _project(xl, fin_l, sc1, sh1, lw, tabs, t, s_tot, 0)
        xc, pl_ctx = _project(xc, fin_c, sc1c, sh1c, lw, None, tc, s_tot, t,
                              kv_prev=[pl_lat[i] for i in _KV_OUTS])
        qa, qb, qc, qd = [pl_lat[i].reshape(bsz, t, -1) for i in _Q_OUTS]
        qa_c, qb_c, qc_c, qd_c = [pl_ctx[i].reshape(bsz, tc, -1) for i in _Q_OUTS]
        ka, va, kb, vb, kc, vc, kd, vd = [pl_ctx[i] for i in _KV_OUTS]

        bias_tab = _nbr_bias_tables(na_bias[l], rows)
        everything = dict(parts=_key_ranges(s_tot), bq=GLOBAL_BQ, single_shot=False)
        out_a = _attention("mha", qa, ka, va, [("nbr", 0, t), ("full", t, tc)], bias=bias_tab, nsub=LOCAL_NSUB)
        out_b = _attention("gqa", qb, kb, vb, **everything)
        out_c = _attention("gqa", qc, kc, vc, [("win", 0, t), ("full", t, tc)], sink=win_sink[l],
                           nsub=LOCAL_NSUB)
        out_d = _attention("mla", qd, kd, vd, **everything)
        outs = [o.reshape(bsz * t, 256) for o in (out_a, out_b, out_c, out_d)]

        n_lat, n_ctx = bsz * t, bsz * tc
        n_moe = n_lat + n_ctx if with_ctx else n_lat
        x1, h2x, ids = _out_proj(alpha, outs, xl, g1, sc2, sh2, lw, t, n_moe, 0)
        if with_ctx:
            ctx_only = dict(parts=[("full", 0, tc)], kv_block=(tc, t // tc))
            outs_c = _attention_run([
                _attention_plan("mha", qa_c, ka, va, **ctx_only),
                _attention_plan("gqa", qb_c, kb, vb, **ctx_only),
                _attention_plan("gqa", qc_c, kc, vc, sink=win_sink[l], **ctx_only),
                _attention_plan("mla", qd_c, kd, vd, **ctx_only),
            ], bsz, tc)
            outs_c = [o.reshape(n_ctx, 256) for o in outs_c]
            x1c, h2x, ids = _out_proj(alpha, outs_c, xc, g1c, sc2c, sh2c, lw, tc, n_moe, n_lat, prev=(h2x, ids))
        y = _moe(l, h2x, ids, moe_w_gate, moe_w_up, moe_w_down, MOE_CHUNKS,
                 (n_lat, n_ctx) if with_ctx else (n_lat,))
        fin_l = (alpha, x1, y, g2, ln2g, ln2b, 0)
        if with_ctx:
            fin_c = (alpha, x1c, y, g2c, ln2g, ln2b, n_lat)

    return _finish(*fin_l[:6], t).reshape(bsz, t, d)
```

```python
import functools

import jax
import jax.numpy as jnp
from jax import lax
from jax.experimental import pallas as pl
from jax.experimental.pallas import tpu as pltpu

F32 = jnp.float32
BF16 = jnp.bfloat16

GRID_W = 64
HEAD_DIM = 64
BLOCK = 128
WINDOW = 128
ROPE_THETA = 10000.0
NEG = -1e30
EPS = 1e-6
LOG2E = 1.4426950408889634

NA_ROWS = 8
NA_COLS = 16
NBR_SPAN_ROWS = 10
MLA_NOPE = 64
MLA_ROPE = 32
MLA_QK = MLA_NOPE + MLA_ROPE
N_GROUPS = 4
EXPERTS_PER_GROUP = 4
N_EXPERTS = 16
N_PAIRS = 6
N_BUCKETS = N_GROUPS * N_PAIRS
ROUTE_LANES = 8

LANES = 128
VMEM_LIMIT = 48 * 1024 * 1024

ROW_TILE = 512
ROW_SPLIT = 2
OUT_TILE = 1024
MOE_TILE = 512
MOE_CHUNKS = 4
KV_CHUNK = 2560
GLOBAL_BQ = 512
LOCAL_NSUB = 8
H2_EXT = 128


def _cparams(*sem):
    return pltpu.CompilerParams(dimension_semantics=sem, vmem_limit_bytes=VMEM_LIMIT)


def _dot(a, b):
    return jnp.dot(a, b, preferred_element_type=F32)


def _dot_nt(a, b):
    return lax.dot_general(a, b, (((1,), (1,)), ((), ())), preferred_element_type=F32)


def _split(a):
    hi = a.astype(BF16)
    lo = (a - hi.astype(F32)).astype(BF16)
    return hi, lo


def _dot3(a, b):
    ah, al = _split(a)
    bh, bl = _split(b)
    return _dot(al, bh) + _dot(ah, bl) + _dot(ah, bh)


def _ln(x):
    mu = jnp.mean(x, axis=-1, keepdims=True)
    xc = x - mu
    var = jnp.mean(xc * xc, axis=-1, keepdims=True)
    return xc * lax.rsqrt(var + EPS)


def _silu(g):
    return g / (1.0 + jnp.exp(-g))


def _ada_kernel(c_ref, w_ref, b_ref, o_ref):
    o_ref[0] = _dot3(_silu(c_ref[...]), w_ref[0]) + b_ref[0]


def _ada(cs, w_ada, b_ada):
    depth, d, n = w_ada.shape
    rows = cs.shape[0]
    tn = 1536
    return pl.pallas_call(
        _ada_kernel,
        grid=(depth, n // tn),
        in_specs=[pl.BlockSpec((rows, d), lambda l, j: (0, 0)),
                  pl.BlockSpec((1, d, tn), lambda l, j: (l, 0, j)),
                  pl.BlockSpec((1, 1, tn), lambda l, j: (l, 0, j))],
        out_specs=pl.BlockSpec((1, rows, tn), lambda l, j: (l, 0, j)),
        out_shape=jax.ShapeDtypeStruct((depth, rows, n), F32),
        compiler_params=_cparams("parallel", "parallel"),
    )(cs, w_ada, b_ada.reshape(depth, 1, n))


def _head_rms(x, bd, gain):
    hi, lo = _split(x * x)
    ss = _dot(lo, bd) + _dot(hi, bd)
    return x * lax.rsqrt(ss * (1.0 / HEAD_DIM) + EPS) * gain


def _rms(x, gain):
    return x * lax.rsqrt(jnp.mean(x * x, axis=-1, keepdims=True) + EPS) * gain


def _rope(x, tab_ref, shift):
    outs = []
    for j in range(x.shape[1] // LANES):
        xb = x[:, j * LANES:(j + 1) * LANES]
        outs.append(xb * tab_ref[0]
                    + pltpu.roll(xb, shift, 1) * tab_ref[1]
                    + pltpu.roll(xb, LANES - shift, 1) * tab_ref[2])
    return outs[0] if len(outs) == 1 else jnp.concatenate(outs, axis=1)


def _value_tiles(v):
    lane = lax.broadcasted_iota(jnp.int32, (v.shape[0], LANES), 1)
    fill = jnp.where(lane == HEAD_DIM, 1.0, 0.0)
    outs = []
    for j in range(v.shape[1] // LANES):
        blk = v[:, j * LANES:(j + 1) * LANES]
        outs.append(jnp.where(lane < HEAD_DIM, blk, fill))
        outs.append(jnp.where(lane < HEAD_DIM, pltpu.roll(blk, HEAD_DIM, 1), fill))
    return jnp.concatenate(outs, axis=1).astype(BF16)


def _proj_kernel(rope, fin_alpha, n_aliased, *refs):
    refs = list(refs)
    if fin_alpha is None:
        x_ref = refs.pop(0)
    else:
        x1_ref, y_ref, g2_ref, lng_ref, lnb_ref = refs[:5]
        refs = refs[5:]
    (sc_ref, sh_ref, w_ref, gq_ref, gk_ref, gmq_ref, gmkv_ref,
     wqb_ref, wkvb_ref, bd_ref) = refs[:10]
    refs = refs[10:]
    tabs = None
    if rope:
        tabs = refs[:3]
        refs = refs[3:]
    refs = refs[n_aliased:]
    if fin_alpha is not None:
        xo_ref = refs.pop(0)
    refs = [r.at[0] if len(r.shape) == 3 else r for r in refs]
    group = refs[0].shape[0] // ROW_SPLIT
    for u in range(ROW_SPLIT):
        rows = slice(u * group, (u + 1) * group)
        if fin_alpha is None:
            x = x_ref[rows, :]
        else:
            x = (_ln(fin_alpha * x1_ref[rows, :] + g2_ref[0] * y_ref[rows, :]) * lng_ref[...]
                 + lnb_ref[...])
            xo_ref[rows, :] = x
        _proj_rows(rope, x, rows, sc_ref, sh_ref, w_ref, gq_ref, gk_ref, gmq_ref, gmkv_ref, wqb_ref,
                   wkvb_ref, bd_ref, tabs, refs)


def _proj_rows(rope, x, rows, sc_ref, sh_ref, w_ref, gq_ref, gk_ref, gmq_ref, gmkv_ref, wqb_ref, wkvb_ref,
               bd_ref, tabs, outs):
    qa, ka, va, qb, kb, vb, qc, kc, vc, qd, kd, vd = outs
    if rope:
        t64_ref, tq_ref, tk_ref = [t.at[:, rows, :] for t in tabs]
    h = _ln(x) * (1.0 + sc_ref[0]) + sh_ref[0]
    hb = h.astype(BF16)
    sq = HEAD_DIM ** -0.5 * LOG2E

    def proj(a, b):
        return _dot(hb, w_ref[:, a:b])

    pa = proj(0, 768)
    qa[rows, :] = (pa[:, 0:256] * sq).astype(BF16)
    ka[rows, :] = pa[:, 256:512].astype(BF16)
    va[rows, :] = _value_tiles(pa[:, 512:768])

    pb = proj(768, 1280)
    q = _head_rms(pb[:, 0:256], bd_ref[...], gq_ref[...])
    k = _head_rms(pb[:, 256:384], bd_ref[0:128, 0:128], gk_ref[...])
    if rope:
        q = _rope(q, t64_ref, 16)
        k = _rope(k, t64_ref, 16)
    qb[rows, :] = (q * sq).astype(BF16)
    kb[rows, :] = k.astype(BF16)
    vb[rows, :] = _value_tiles(pb[:, 384:512])

    pc = proj(1280, 1792)
    q = pc[:, 0:256]
    k = pc[:, 256:384]
    if rope:
        q = _rope(q, t64_ref, 16)
        k = _rope(k, t64_ref, 16)
    qc[rows, :] = (q * sq).astype(BF16)
    kc[rows, :] = k.astype(BF16)
    vc[rows, :] = _value_tiles(pc[:, 384:512])

    pd = proj(1792, 2304)
    cq = _rms(pd[:, 0:256], gmq_ref[...])
    q = _dot(cq.astype(BF16), wqb_ref[...])
    if rope:
        q = _rope(q, tq_ref, 8)
    qd[rows, :] = (q * (MLA_QK ** -0.5 * LOG2E)).astype(BF16)
    ckv = _rms(pd[:, 256:384], gmkv_ref[...])
    kvu = _dot(ckv.astype(BF16), wkvb_ref[...])
    kr = pd[:, 384:512]
    if rope:
        kr = _rope(kr, tk_ref, 8)
    kr = pltpu.roll(kr, MLA_NOPE, 1)
    kd[rows, :] = jnp.concatenate(
        [kvu[:, j * LANES:(j + 1) * LANES] + kr for j in range(4)], axis=1).astype(BF16)
    lane = lax.broadcasted_iota(jnp.int32, (kr.shape[0], LANES), 1)
    fill = jnp.where(lane == HEAD_DIM, 1.0, 0.0)
    vd[rows, :] = jnp.concatenate(
        [kvu[:, (4 + j) * LANES:(5 + j) * LANES] + fill for j in range(4)], axis=1).astype(BF16)


_PROJ_WIDTHS = (256, 256, 512, 256, 128, 256, 256, 128, 256, 512, 512, 512)


_Q_OUTS = (0, 3, 6, 9)
_KV_OUTS = (1, 2, 4, 5, 7, 8, 10, 11)


def _project(x2d, fin, sc, sh, lw, tabs, tokens_per_batch, kv_rows, kv_row0, kv_prev=None):
    n, d = (x2d if fin is None else fin[1]).shape
    tm = min(ROW_TILE, tokens_per_batch)
    per_b = tokens_per_batch // tm
    bsz = n // tokens_per_batch
    rope = tabs is not None
    row = lambda i: (i, 0)
    full = lambda i: (0, 0)
    perb = lambda i: (i // per_b, 0, 0)
    kvrow = lambda i: (i // per_b, kv_row0 // tm + i % per_b, 0)
    if fin is None:
        in_specs = [pl.BlockSpec((tm, d), row)]
        args = [x2d]
    else:
        in_specs = [pl.BlockSpec((tm, d), row), pl.BlockSpec((tm, d), lambda i: (i + fin[6] // tm, 0)),
                    pl.BlockSpec((1, 1, d), perb), pl.BlockSpec((1, d), full), pl.BlockSpec((1, d), full)]
        args = list(fin[1:6])
    in_specs += [pl.BlockSpec((1, 1, d), perb), pl.BlockSpec((1, 1, d), perb),
                 pl.BlockSpec(lw["w_in"].shape, full),
                 pl.BlockSpec((1, 256), full), pl.BlockSpec((1, 128), full),
                 pl.BlockSpec((1, 256), full), pl.BlockSpec((1, 128), full),
                 pl.BlockSpec(lw["w_qb"].shape, full), pl.BlockSpec(lw["w_kvb"].shape, full),
                 pl.BlockSpec((256, 256), full)]
    args += [sc, sh, lw["w_in"], lw["gq"], lw["gk"], lw["gmq"], lw["gmkv"],
             lw["w_qb"], lw["w_kvb"], lw["bd"]]
    if rope:
        tab = lambda i: (0, i % per_b, 0)
        in_specs += [pl.BlockSpec((3, tm, LANES), tab)] * 3
        args += list(tabs)
    out_specs = [pl.BlockSpec((1, tm, w), kvrow) if i in _KV_OUTS else pl.BlockSpec((tm, w), row)
                 for i, w in enumerate(_PROJ_WIDTHS)]
    out_shape = [jax.ShapeDtypeStruct((bsz, kv_rows, w) if i in _KV_OUTS else (n, w), BF16)
                 for i, w in enumerate(_PROJ_WIDTHS)]
    lead = 0 if fin is None else 1
    if fin is not None:
        out_specs.insert(0, pl.BlockSpec((tm, d), row))
        out_shape.insert(0, jax.ShapeDtypeStruct((n, d), F32))
    aliases = {}
    if kv_prev is not None:
        for a, i in zip(kv_prev, _KV_OUTS):
            aliases[len(args)] = lead + i
            in_specs.append(pl.BlockSpec(memory_space=pl.ANY))
            args.append(a)
    outs = pl.pallas_call(
        functools.partial(_proj_kernel, rope, None if fin is None else fin[0], len(aliases)),
        grid=(n // tm,),
        in_specs=in_specs,
        out_specs=out_specs,
        out_shape=out_shape,
        input_output_aliases=aliases,
        compiler_params=_cparams("parallel"),
    )(*args)
    return (x2d, list(outs)) if fin is None else (outs[0], list(outs[1:]))


def _softmax_step(s, vts, bq, m, acc):
    m_new = jnp.maximum(m, jnp.max(s, axis=1, keepdims=True))
    alpha = jnp.exp2(m - m_new)
    p = jnp.exp2(s - m_new).astype(BF16)
    if len(vts) == 1:
        pv = _dot(p, vts[0])
    else:
        pv = jnp.concatenate([_dot(p[:bq], vts[0]), _dot(p[bq:], vts[1])], axis=0)
    return [m_new, alpha * acc + pv]


def _attn_kernel(mode, bq, nsub, srcs, single_shot, has_bias, has_sink, *refs):
    refs = list(refs)
    q_ref = refs.pop(0)
    k_ref, v_ref = refs.pop(0), refs.pop(0)
    bias_refs = [refs.pop(0) for _ in range(nsub)] if has_bias else None
    sink_ref = refs.pop(0) if has_sink else None
    o_ref = refs.pop(0)

    for u in range(nsub):
        qi = pl.program_id(1) * nsub + u
        rows = slice(u * bq, (u + 1) * bq)
        rows2 = 2 * bq
        lo = lax.broadcasted_iota(jnp.int32, (bq, LANES), 1) < HEAD_DIM

        qss, kcs, vcs = [], [], []
        for j in range(2):
            if mode == "mla":
                qblk = q_ref[0, rows, 2 * LANES * j:2 * LANES * (j + 1)].astype(F32)
                first = lax.broadcasted_iota(jnp.int32, (bq, 2 * LANES), 1) < LANES
                qs = jnp.concatenate([jnp.where(first, qblk, 0.0), jnp.where(first, 0.0, qblk)], axis=0)
                kcs.append((2 * LANES * j, 2 * LANES * (j + 1)))
                vcs.append((2 * LANES * j, 2 * LANES * j + LANES))
            else:
                qblk = q_ref[0, rows, LANES * j:LANES * (j + 1)].astype(F32)
                if mode == "mha":
                    qe, qo = jnp.where(lo, qblk, 0.0), jnp.where(lo, 0.0, qblk)
                    kcs.append((LANES * j, LANES * (j + 1)))
                    vcs.append((2 * LANES * j, 2 * LANES * j + LANES))
                else:
                    rolled = pltpu.roll(qblk, HEAD_DIM, 1)
                    if j == 0:
                        qe, qo = jnp.where(lo, qblk, 0.0), jnp.where(lo, rolled, 0.0)
                    else:
                        qe, qo = jnp.where(lo, 0.0, rolled), jnp.where(lo, 0.0, qblk)
                    kcs.append((0, LANES))
                    vcs.append((LANES * j,))
                qs = jnp.concatenate([qe, qo], axis=0)
            qss.append(qs.astype(BF16))

        def pv(p, vts):
            if len(vts) == 1:
                return _dot(p, vts[0])
            return jnp.concatenate([_dot(p[:bq], vts[0]), _dot(p[bq:], vts[1])], axis=0)

        def tiles(j, off, n):
            kt = k_ref[0, pl.ds(off, n), kcs[j][0]:kcs[j][1]]
            return kt, [v_ref[0, pl.ds(off, n), v0:v0 + LANES] for v0 in vcs[j]]

        def block_scores(j, kind, row0, nrows):
            if kind == "full":
                kt, vts = tiles(j, row0, nrows)
                return _dot_nt(qss[j], kt), vts
            if kind == "win":
                span = bq + 2 * WINDOW
                start = pl.multiple_of(jnp.clip((qi - 1) * bq, 0, nrows - span), bq)
                r = lax.broadcasted_iota(jnp.int32, (rows2, span), 0)
                qpos = qi * bq + jnp.where(r >= bq, r - bq, r)
                kpos = start + lax.broadcasted_iota(jnp.int32, (rows2, span), 1)
                kt, vts = tiles(j, row0 + start, span)
                return jnp.where(jnp.abs(qpos - kpos) <= WINDOW, _dot_nt(qss[j], kt), NEG), vts
            span = NBR_SPAN_ROWS * GRID_W
            ks = jnp.clip(2 * qi - NA_ROWS // 2, 0, nrows // GRID_W - NBR_SPAN_ROWS)
            kt, vts = tiles(j, row0 + pl.multiple_of(ks * GRID_W, GRID_W), span)
            return _dot_nt(qss[j], kt) + bias_refs[u][0, j], vts

        if single_shot:
            state = []
            for j in range(2):
                parts = [block_scores(j, kind, row0, nrows) for kind, row0, nrows in srcs]
                m = functools.reduce(jnp.maximum, [jnp.max(sc, axis=1, keepdims=True) for sc, _ in parts])
                acc = functools.reduce(lambda a, b: a + b,
                                       [pv(jnp.exp2(sc - m).astype(BF16), vts) for sc, vts in parts])
                state += [m, acc]
        else:
            state = [jnp.full((rows2, 1), NEG, F32), jnp.zeros((rows2, LANES), F32)] * 2
            for kind, row0, nrows in srcs:
                for j in range(2):
                    sc, vts = block_scores(j, kind, row0, nrows)
                    state[2 * j:2 * j + 2] = _softmax_step(sc, vts, bq, state[2 * j], state[2 * j + 1])

        for j in range(2):
            m, acc = state[2 * j], state[2 * j + 1]
            l = acc[:, HEAD_DIM:HEAD_DIM + 1]
            if has_sink:
                sk = jnp.concatenate([jnp.full((bq, 1), sink_ref[2 * j] * LOG2E, F32),
                                      jnp.full((bq, 1), sink_ref[2 * j + 1] * LOG2E, F32)], axis=0)
                mf = jnp.maximum(m, sk)
                a = jnp.exp2(m - mf)
                l = l * a + jnp.exp2(sk - mf)
                acc = acc * a
            o = acc / l
            o_ref[0, rows, LANES * j:LANES * (j + 1)] = jnp.where(
                lo, o[:bq], pltpu.roll(o[bq:], HEAD_DIM, 1)).astype(BF16)


def _key_ranges(nrows):
    n = -(-nrows // KV_CHUNK)
    size = max(nrows // n // 256, 1) * 256
    starts = [i * size for i in range(n)]
    return [("full", r0, (nrows if i == n - 1 else r0 + size) - r0) for i, r0 in enumerate(starts)]


def _attention_plan(mode, q, k, v, parts, kv_block=None, bias=None, sink=None, bq=BLOCK, nsub=1,
                    single_shot=True):
    b, tq, wq = q.shape
    srcs = tuple(parts)
    rows, blk = (k.shape[1], 0) if kv_block is None else kv_block
    in_specs = [pl.BlockSpec((1, nsub * bq, wq), lambda bi, qi: (bi, qi, 0)),
                pl.BlockSpec((1, rows, k.shape[2]), lambda bi, qi: (bi, blk, 0)),
                pl.BlockSpec((1, rows, v.shape[2]), lambda bi, qi: (bi, blk, 0))]
    args = [q, k, v]
    if bias is not None:
        npair = tq // bq

        for u in range(nsub):
            def bias_idx(bi, qi, u=u):
                p = qi * nsub + u
                return (jnp.where(p < 2, p, jnp.where(p >= npair - 2, p - (npair - 2) + 3, 2)), 0, 0, 0)
            in_specs.append(pl.BlockSpec((1,) + bias.shape[1:], bias_idx))
            args.append(bias)
    if sink is not None:
        in_specs.append(pl.BlockSpec(memory_space=pltpu.SMEM))
        args.append(sink)
    body = functools.partial(_attn_kernel, mode, bq, nsub, srcs, single_shot, bias is not None, sink is not None)
    return (body, nsub * bq, in_specs, args, pl.BlockSpec((1, nsub * bq, 256), lambda bi, qi: (bi, qi, 0)),
            jax.ShapeDtypeStruct((b, tq, 256), BF16))


def _attention_run(plans, batch, tq):
    step = plans[0][1]
    assert all(p[1] == step for p in plans)
    counts = [len(p[2]) for p in plans]

    def body(*refs):
        ins, outs = refs[:sum(counts)], refs[sum(counts):]
        at = 0
        for (fn, _, _, _, _, _), n_in, o_ref in zip(plans, counts, outs):
            fn(*ins[at:at + n_in], o_ref)
            at += n_in

    return list(pl.pallas_call(
        body,
        grid=(batch, tq // step),
        in_specs=[sp for p in plans for sp in p[2]],
        out_specs=[p[4] for p in plans],
        out_shape=[p[5] for p in plans],
        compiler_params=_cparams("parallel", "arbitrary"),
    )(*[a for p in plans for a in p[3]]))


def _attention(mode, q, k, v, parts, **kw):
    return _attention_run([_attention_plan(mode, q, k, v, parts, **kw)], q.shape[0], q.shape[1])[0]


def _route(logits):
    lane = lax.broadcasted_iota(jnp.int32, logits.shape, 1).astype(F32)
    big = float(LANES)
    is_g = lane < N_GROUPS
    lg = jnp.where(is_g, logits, NEG)
    gmax = jnp.max(lg, axis=1, keepdims=True)
    grp = jnp.min(jnp.where(lg == gmax, lane, big), axis=1, keepdims=True)
    den = jnp.sum(jnp.where(is_g, jnp.exp(lg - gmax), 0.0), axis=1, keepdims=True)
    p_grp = 1.0 / den
    e0 = N_GROUPS + EXPERTS_PER_GROUP * grp
    in_grp = (lane >= e0) & (lane < e0 + EXPERTS_PER_GROUP)
    le = jnp.where(in_grp, logits, NEG)
    v1 = jnp.max(le, axis=1, keepdims=True)
    i1 = jnp.min(jnp.where(in_grp & (le == v1), lane, big), axis=1, keepdims=True)
    rest = in_grp & (lane != i1)
    le2 = jnp.where(rest, logits, NEG)
    v2 = jnp.max(le2, axis=1, keepdims=True)
    i2 = jnp.min(jnp.where(rest & (le2 == v2), lane, big), axis=1, keepdims=True)
    t = jnp.exp(v2 - v1)
    w1 = p_grp / (1.0 + t)
    w2 = p_grp * t / (1.0 + t)
    first_lower = i1 < i2
    e_lo = jnp.where(first_lower, i1, i2) - N_GROUPS
    e_hi = jnp.where(first_lower, i2, i1) - N_GROUPS
    w_lo = jnp.where(first_lower, w1, w2)
    w_hi = jnp.where(first_lower, w2, w1)
    a = e_lo - EXPERTS_PER_GROUP * grp
    b = e_hi - EXPERTS_PER_GROUP * grp
    pair = a * (2 * EXPERTS_PER_GROUP - 1 - a) * 0.5 + (b - a - 1.0)
    pair = jnp.where(pair == 3.0, 4.0, jnp.where(pair == 4.0, 3.0, pair))
    bucket = grp * N_PAIRS + pair
    return jnp.where(lane == 0, e_lo, jnp.where(lane == 1, e_hi, jnp.where(lane == 2, w_lo,
                     jnp.where(lane == 3, w_hi, jnp.where(lane == 4, bucket, 0.0)))))


def _out_kernel(alpha, oa, ob, oc, od, x_ref, g1_ref, sc2_ref, sh2_ref, w_ref, lng_ref, lnb_ref,
                wrh_ref, wrl_ref, br_ref, *rest):
    x1_ref, h2_ref, ids_ref = rest[-3:]
    mix = (_dot(oa[...], w_ref[0:256, :]) + _dot(ob[...], w_ref[256:512, :])
           + _dot(oc[...], w_ref[512:768, :]) + _dot(od[...], w_ref[768:1024, :]))
    x1 = _ln(alpha * x_ref[...] + g1_ref[0] * mix) * lng_ref[...] + lnb_ref[...]
    x1_ref[...] = x1
    h2 = _ln(x1) * (1.0 + sc2_ref[0]) + sh2_ref[0]
    hh, hl = _split(h2)
    logits = _dot(hl, wrh_ref[...]) + _dot(hh, wrl_ref[...]) + _dot(hh, wrh_ref[...]) + br_ref[...]
    rh, rl = _split(_route(logits))
    d = h2.shape[1]
    h2_ref[:, 0:d] = hh
    lane = lax.broadcasted_iota(jnp.int32, rh.shape, 1)
    h2_ref[:, d:d + LANES] = jnp.where(lane < ROUTE_LANES, rh.astype(F32),
                                       pltpu.roll(rl.astype(F32), ROUTE_LANES, 1)).astype(BF16)
    pick = ((lax.broadcasted_iota(jnp.int32, (8, LANES), 0) == 0)
            & (lax.broadcasted_iota(jnp.int32, (8, LANES), 1) == 4)).astype(BF16)
    ids_ref[...] = _dot_nt(pick, rh)


def _out_proj(alpha, outs, x2d, g1, sc2, sh2, lw, tokens_per_batch, n_rows, row0, prev=None):
    n, d = x2d.shape
    tm = min(OUT_TILE, tokens_per_batch)
    per_b = tokens_per_batch // tm
    row = lambda i: (i, 0)
    full = lambda i: (0, 0)
    perb = lambda i: (i // per_b, 0, 0)
    in_specs = [pl.BlockSpec((tm, 256), row)] * 4 + [
        pl.BlockSpec((tm, d), row),
        pl.BlockSpec((1, 1, d), perb), pl.BlockSpec((1, 1, d), perb), pl.BlockSpec((1, 1, d), perb),
        pl.BlockSpec((d, d), full), pl.BlockSpec((1, d), full), pl.BlockSpec((1, d), full),
        pl.BlockSpec((d, LANES), full), pl.BlockSpec((d, LANES), full), pl.BlockSpec((1, LANES), full)]
    args = [*outs, x2d, g1, sc2, sh2, lw["w_out"], lw["ln1_g"], lw["ln1_b"], lw["wr_hi"], lw["wr_lo"], lw["br"]]
    aliases = {}
    if prev is not None:
        for k, a in enumerate(prev):
            aliases[len(args)] = 1 + k
            in_specs.append(pl.BlockSpec(memory_space=pl.ANY))
            args.append(a)
    return pl.pallas_call(
        functools.partial(_out_kernel, alpha),
        grid=(n // tm,),
        in_specs=in_specs,
        out_specs=[pl.BlockSpec((tm, d), row), pl.BlockSpec((tm, d + H2_EXT), lambda i: (i + row0 // tm, 0)),
                   pl.BlockSpec((8, tm), lambda i: (0, i + row0 // tm))],
        out_shape=[jax.ShapeDtypeStruct((n, d), F32), jax.ShapeDtypeStruct((n_rows, d + H2_EXT), BF16),
                   jax.ShapeDtypeStruct((8, n_rows), F32)],
        input_output_aliases=aliases,
        compiler_params=_cparams("parallel"),
    )(*args)


def _moe_kernel(tile0, e0_ref, e1_ref, valid_ref, xs_ref, wg0, wg1, wu0, wu1, wd0, wd1, *rest):
    o_ref, wb = rest[-7], rest[-6:]
    t = pl.program_id(0) + tile0
    d = o_ref.shape[1]
    prev = jnp.maximum(t - 1, 0)
    first = pl.program_id(0) == 0

    @pl.when(first | (e0_ref[t] != e0_ref[prev]))
    def _():
        for src, dst in zip((wg0, wu0, wd0), wb[0::2]):
            dst[...] = src[0, 0].astype(BF16)

    @pl.when(first | (e1_ref[t] != e1_ref[prev]))
    def _():
        for src, dst in zip((wg1, wu1, wd1), wb[1::2]):
            dst[...] = src[0, 0].astype(BF16)

    @pl.when(valid_ref[t] == 1)
    def _():
        x = xs_ref[:, 0:d]
        ext = xs_ref[:, d:d + LANES].astype(F32)
        rt = ext + pltpu.roll(ext, LANES - ROUTE_LANES, 1)
        lower_first = e0_ref[t] < e1_ref[t]
        w0 = jnp.where(lower_first, rt[:, 2:3], rt[:, 3:4])
        w1 = jnp.where(lower_first, rt[:, 3:4], rt[:, 2:3])

        def expert(wg, wu, wd):
            a = _silu(_dot(x, wg[...])) * _dot(x, wu[...])
            return _dot(a.astype(BF16), wd[...])

        y = w0 * expert(wb[0], wb[2], wb[4]) + w1 * expert(wb[1], wb[3], wb[5])
        o_ref[...] = y.astype(o_ref.dtype)

    @pl.when(valid_ref[t] == 0)
    def _():
        o_ref[...] = jnp.zeros_like(o_ref)


def _moe_experts(layer, tile0, e0, e1, valid, xs, wg, wu, wd, ys_prev):
    p, dx = xs.shape
    d = dx - H2_EXT
    tm = MOE_TILE
    de = wg.shape[3]
    w0 = lambda t, e0, e1, v: (layer, e0[t + tile0], 0, 0)
    w1 = lambda t, e0, e1, v: (layer, e1[t + tile0], 0, 0)
    in_specs = [pl.BlockSpec((tm, dx), lambda t, e0, e1, v: (t, 0)),
                pl.BlockSpec((1, 1, d, de), w0), pl.BlockSpec((1, 1, d, de), w1),
                pl.BlockSpec((1, 1, d, de), w0), pl.BlockSpec((1, 1, d, de), w1),
                pl.BlockSpec((1, 1, de, d), w0), pl.BlockSpec((1, 1, de, d), w1)]
    args = [e0, e1, valid, xs, wg, wg, wu, wu, wd, wd]
    aliases = {}
    if ys_prev is not None:
        aliases[len(args)] = 0
        in_specs.append(pl.BlockSpec(memory_space=pl.ANY))
        args.append(ys_prev)
    return pl.pallas_call(
        functools.partial(_moe_kernel, tile0),
        grid_spec=pltpu.PrefetchScalarGridSpec(
            num_scalar_prefetch=3,
            grid=(p // tm,),
            in_specs=in_specs,
            out_specs=pl.BlockSpec((tm, d), lambda t, e0, e1, v: (t + tile0, 0)),
            scratch_shapes=[pltpu.VMEM((d, de), BF16)] * 4 + [pltpu.VMEM((de, d), BF16)] * 2),
        out_shape=jax.ShapeDtypeStruct((e0.shape[0] * tm, d), BF16),
        input_output_aliases=aliases,
        compiler_params=_cparams("arbitrary"),
    )(*args)


_PAIR_TABLE = ((0, 1), (0, 2), (0, 3), (1, 3), (1, 2), (3, 2))


def _moe(layer, h2x, ids, wg, wu, wd, chunks, segments):
    n = h2x.shape[0]
    tm = MOE_TILE
    n_tiles = -(-(n // tm + N_BUCKETS) // chunks) * chunks
    bucket = ids[0].astype(jnp.int32)
    bids = jnp.arange(N_BUCKETS + 1, dtype=jnp.int32)
    onehot_b = bucket[None, :] == bids[:N_BUCKETS, None]
    per_token = lambda table: jnp.sum(jnp.where(onehot_b, table[:, None], 0), axis=0)

    seg = []
    within_tok = []
    start = 0
    for size in segments:
        tok = jnp.arange(size, dtype=jnp.int32)
        sorted_bucket, order = lax.sort((bucket[start:start + size], tok), num_keys=1)
        _, rank = lax.sort((order, tok), num_keys=1)
        edges = jnp.sum(sorted_bucket[None, :] < bids[:, None], axis=1, dtype=jnp.int32)
        seg.append((start, order, edges[:-1], edges[1:] - edges[:-1]))
        within_tok.append(rank)
        start += size
    counts = sum(c for _, _, _, c in seg)
    tiles_per = (counts + tm - 1) // tm
    tile_end = jnp.cumsum(tiles_per)
    tile_start = tile_end - tiles_per

    tile_ids = jnp.arange(n_tiles, dtype=jnp.int32)
    valid = (tile_ids < tile_end[-1]).astype(jnp.int32)
    tile_bucket = jnp.sum(tile_ids[:, None] >= tile_end[None, :], axis=1, dtype=jnp.int32)
    last_bucket = jnp.sum(tile_end[-1] - 1 >= tile_end, dtype=jnp.int32)
    tile_bucket = jnp.minimum(jnp.where(valid == 1, tile_bucket, last_bucket), N_BUCKETS - 1)
    onehot_tb = tile_bucket[:, None] == bids[None, :N_BUCKETS]
    pick = lambda table: jnp.sum(jnp.where(onehot_tb, table[None, :], 0), axis=1)
    pairs = jnp.asarray(_PAIR_TABLE, dtype=jnp.int32)
    tg = tile_bucket // N_PAIRS
    e0 = tg * EXPERTS_PER_GROUP + pairs[tile_bucket % N_PAIRS, 0]
    e1 = tg * EXPERTS_PER_GROUP + pairs[tile_bucket % N_PAIRS, 1]

    first_row = (tile_ids - pick(tile_start)) * tm
    within = first_row[:, None] + jnp.arange(tm, dtype=jnp.int32)[None, :]
    row_ok = (within < pick(counts)[:, None]) & (valid[:, None] == 1)
    order_all = jnp.concatenate([start + order for start, order, _, _ in seg])
    idx = jnp.zeros(within.shape, jnp.int32)
    before = jnp.zeros((N_BUCKETS,), jnp.int32)
    base = []
    for (start, order, off, cnt), rank in zip(seg, within_tok):
        j = within - pick(before)[:, None]
        mine = (j >= 0) & (j < pick(cnt)[:, None])
        idx = jnp.where(mine, start + pick(off)[:, None] + j, idx)
        base.append((tile_start * tm + before - off, start, rank))
        before = before + cnt
    idx = jnp.clip(idx, 0, n - 1).reshape(-1)
    src = jnp.where(row_ok.reshape(-1), order_all.at[idx].get(mode="promise_in_bounds"), 0)

    per_chunk = n_tiles // chunks
    ys = None
    for c in range(chunks):
        xs = h2x.at[src[c * per_chunk * tm:(c + 1) * per_chunk * tm]].get(mode="promise_in_bounds")
        ys = _moe_experts(layer, c * per_chunk, e0, e1, valid, xs, wg, wu, wd, ys)

    pos = jnp.concatenate([per_token(table)[start:start + rank.shape[0]] + rank for table, start, rank in base])
    return ys.at[pos].get(mode="promise_in_bounds")


def _fin_kernel(alpha, x_ref, y_ref, g2_ref, lng_ref, lnb_ref, o_ref):
    o_ref[...] = _ln(alpha * x_ref[...] + g2_ref[0] * y_ref[...]) * lng_ref[...] + lnb_ref[...]


def _finish(alpha, x1, y, g2, ln_g, ln_b, tokens_per_batch):
    n, d = x1.shape
    tm = min(ROW_TILE, tokens_per_batch)
    per_b = tokens_per_batch // tm
    row = lambda i: (i, 0)
    full = lambda i: (0, 0)
    return pl.pallas_call(
        functools.partial(_fin_kernel, alpha),
        grid=(n // tm,),
        in_specs=[pl.BlockSpec((tm, d), row), pl.BlockSpec((tm, d), row),
                  pl.BlockSpec((1, 1, d), lambda i: (i // per_b, 0, 0)),
                  pl.BlockSpec((1, d), full), pl.BlockSpec((1, d), full)],
        out_specs=pl.BlockSpec((tm, d), row),
        out_shape=jax.ShapeDtypeStruct((n, d), F32),
        compiler_params=_cparams("parallel"),
    )(x1, y, g2, ln_g, ln_b)


def _rope_tables(t):
    pos = jnp.arange(t)
    row = (pos // GRID_W).astype(F32)[:, None]
    col = (pos % GRID_W).astype(F32)[:, None]
    lane = jnp.arange(LANES)

    def build(d, half, active):
        axis_col = (d // half) % 2 == 1
        w = d % half
        first = w < half // 2
        f = (w % (half // 2)).astype(F32)
        freq = ROPE_THETA ** (-(2.0 * f) / half)
        ang = jnp.where(axis_col[None, :], col, row) * freq[None, :]
        cos = jnp.where(active[None, :], jnp.cos(ang), 1.0)
        sin = jnp.where(active[None, :], jnp.sin(ang), 0.0)
        return jnp.stack([cos, jnp.where(first[None, :], 0.0, sin), jnp.where(first[None, :], -sin, 0.0)])

    t64 = build(lane % HEAD_DIM, HEAD_DIM // 2, lane >= 0)
    in_rope = (lane >= MLA_NOPE) & (lane < MLA_QK)
    tq = build(jnp.where(in_rope, lane - MLA_NOPE, 0), MLA_ROPE // 2, in_rope)
    in_kr = lane < MLA_ROPE
    tk = build(jnp.where(in_kr, lane, 0), MLA_ROPE // 2, in_kr)
    return t64, tq, tk


def _nbr_bias_tables(na_bias, rows):
    heads = na_bias.shape[0]
    npair = rows // 2
    kr = min(NA_ROWS, rows)
    qcol = jnp.arange(GRID_W)[:, None]
    kcol = jnp.arange(GRID_W)[None, :]
    c0 = jnp.clip(qcol - NA_COLS // 2, 0, GRID_W - NA_COLS)
    col_ok = (kcol >= c0) & (kcol < c0 + NA_COLS)
    col_off = kcol - qcol + NA_COLS - 1
    onehot = (col_off[None] == jnp.arange(2 * NA_COLS - 1)[:, None, None]).astype(F32)
    colmat = jnp.einsum("hdc,cqk->hdqk", na_bias, onehot, precision=lax.Precision.HIGHEST) * LOG2E
    colmat = jnp.where(col_ok[None, None], colmat, NEG)
    outside = jnp.full((heads, GRID_W, GRID_W), NEG, F32)
    tabs = []
    for p in (0, 1, 2, npair - 2, npair - 1):
        ks = min(max(2 * p - NA_ROWS // 2, 0), rows - NBR_SPAN_ROWS)
        per_qrow = []
        for qr in range(2):
            r = 2 * p + qr
            r0 = min(max(r - kr // 2, 0), rows - kr)
            blocks = []
            for kri in range(NBR_SPAN_ROWS):
                krow = ks + kri
                blocks.append(colmat[:, krow - r + NA_ROWS - 1] if r0 <= krow < r0 + kr else outside)
            per_qrow.append(jnp.concatenate(blocks, axis=-1))
        tab = jnp.concatenate(per_qrow, axis=1)
        tabs.append(tab.reshape(heads // 2, 4 * GRID_W, NBR_SPAN_ROWS * GRID_W))
    return jnp.stack(tabs)


def _layer_weights(l, w_in, gqa_q_gain, gqa_k_gain, mla_q_gain, mla_w_qb, mla_kv_gain, mla_w_kvb,
                   w_out, ln1_g, ln1_b, router_group_w, router_group_b, router_expert_w, router_expert_b):
    d = w_in.shape[1]
    d_proj = w_in.shape[2]
    lw = {}
    lw["w_in"] = jnp.pad(w_in[l], ((0, 0), (0, 2304 - d_proj))).astype(BF16)
    lw["gq"] = jnp.tile(gqa_q_gain[l], 4)[None, :]
    lw["gk"] = jnp.tile(gqa_k_gain[l], 2)[None, :]
    lw["gmq"] = mla_q_gain[l][None, :]
    lw["gmkv"] = mla_kv_gain[l][None, :]
    wqb = mla_w_qb[l].reshape(-1, 4, MLA_QK)
    lw["w_qb"] = jnp.pad(wqb, ((0, 0), (0, 0), (0, LANES - MLA_QK))).reshape(-1, 4 * LANES).astype(BF16)
    wkvb = mla_w_kvb[l].reshape(-1, 4, 2 * HEAD_DIM)
    k_part = jnp.pad(wkvb[:, :, :MLA_NOPE], ((0, 0), (0, 0), (0, LANES - MLA_NOPE))).reshape(-1, 4 * LANES)
    v_part = jnp.pad(wkvb[:, :, MLA_NOPE:], ((0, 0), (0, 0), (0, LANES - HEAD_DIM))).reshape(-1, 4 * LANES)
    lw["w_kvb"] = jnp.concatenate([k_part, v_part], axis=1).astype(BF16)
    idx = jnp.arange(256) // HEAD_DIM
    lw["bd"] = (idx[:, None] == idx[None, :]).astype(BF16)
    lw["w_out"] = w_out[l].astype(BF16)
    lw["ln1_g"] = ln1_g[l][None, :]
    lw["ln1_b"] = ln1_b[l][None, :]
    wr = jnp.concatenate([router_group_w[l], router_expert_w[l]], axis=1)
    wr = jnp.pad(wr, ((0, 0), (0, LANES - wr.shape[1])))
    lw["wr_hi"] = wr.astype(BF16)
    lw["wr_lo"] = (wr - lw["wr_hi"].astype(F32)).astype(BF16)
    br = jnp.concatenate([router_group_b[l], router_expert_b[l]])
    lw["br"] = jnp.pad(br, (0, LANES - br.shape[0]))[None, :]
    return lw


def kernel(x, c, ctx, c_ctx, w_ada, b_ada, w_in, na_bias, gqa_q_gain, gqa_k_gain, win_sink, mla_q_gain, mla_w_qb, mla_kv_gain, mla_w_kvb, w_out, ln1_g, ln1_b, router_group_w, router_group_b, router_expert_w, router_expert_b, moe_w_gate, moe_w_up, moe_w_down, ln2_g, ln2_b):
    bsz, t, d = x.shape
    tc = ctx.shape[1]
    depth = w_ada.shape[0]
    rows = t // GRID_W
    assert t % (2 * GRID_W) == 0 and rows >= NBR_SPAN_ROWS and rows // 2 >= 5
    assert t >= BLOCK + 2 * WINDOW and t % ROW_TILE == 0 and tc % (8 * ROW_SPLIT) == 0 and t % tc == 0
    assert tc <= ROW_TILE and t % OUT_TILE == 0
    assert t % (LOCAL_NSUB * BLOCK) == 0 and t % GLOBAL_BQ == 0
    s_tot = t + tc
    alpha = (2.0 * depth) ** 0.25

    ada_rows = -(-(bsz + 1) // 8) * 8
    cs = jnp.zeros((ada_rows, d), F32).at[:bsz].set(c).at[bsz].set(c_ctx)
    mod = _ada(cs, w_ada, b_ada)
    tabs = _rope_tables(t)

    xl = x.reshape(bsz * t, d)
    xc = ctx.reshape(bsz * tc, d)
    fin_l = fin_c = None

    for l in range(depth):
        with_ctx = l < depth - 1
        lw = _layer_weights(l, w_in, gqa_q_gain, gqa_k_gain, mla_q_gain, mla_w_qb, mla_kv_gain,
                            mla_w_kvb, w_out, ln1_g, ln1_b, router_group_w, router_group_b,
                            router_expert_w, router_expert_b)
        ln2g, ln2b = ln2_g[l][None, :], ln2_b[l][None, :]

        def mods(lo, hi):
            parts = jnp.split(mod[l, lo:hi], 6, axis=-1)
            return [jnp.broadcast_to(p[:, None, :], (bsz, 1, d)) for p in parts]
        sh1, sc1, g1, sh2, sc2, g2 = mods(0, bsz)
        sh1c, sc1c, g1c, sh2c, sc2c, g2c = mods(bsz, bsz + 1)

        xl, pl_lat = _project(xl, fin_l, sc1, sh1, lw, tabs, t, s_tot, 0)
        xc, pl_ctx = _project(xc, fin_c, sc1c, sh1c, lw, None, tc, s_tot, t,
                              kv_prev=[pl_lat[i] for i in _KV_OUTS])
        qa, qb, qc, qd = [pl_lat[i].reshape(bsz, t, -1) for i in _Q_OUTS]
        qa_c, qb_c, qc_c, qd_c = [pl_ctx[i].reshape(bsz, tc, -1) for i in _Q_OUTS]
        ka, va, kb, vb, kc, vc, kd, vd = [pl_ctx[i] for i in _KV_OUTS]

        bias_tab = _nbr_bias_tables(na_bias[l], rows)
        everything = dict(parts=_key_ranges(s_tot), bq=GLOBAL_BQ, single_shot=False)
        out_a = _attention("mha", qa, ka, va, [("nbr", 0, t), ("full", t, tc)], bias=bias_tab, nsub=LOCAL_NSUB)
        out_b = _attention("gqa", qb, kb, vb, **everything)
        out_c = _attention("gqa", qc, kc, vc, [("win", 0, t), ("full", t, tc)], sink=win_sink[l],
                           nsub=LOCAL_NSUB)
        out_d = _attention("mla", qd, kd, vd, **everything)
        outs = [o.reshape(bsz * t, 256) for o in (out_a, out_b, out_c, out_d)]

        n_lat, n_ctx = bsz * t, bsz * tc
        n_moe = n_lat + n_ctx if with_ctx else n_lat
        x1, h2x, ids = _out_proj(alpha, outs, xl, g1, sc2, sh2, lw, t, n_moe, 0)
        if with_ctx:
            ctx_only = dict(parts=[("full", 0, tc)], kv_block=(tc, t // tc))
            outs_c = _attention_run([
                _attention_plan("mha", qa_c, ka, va, **ctx_only),
                _attention_plan("gqa", qb_c, kb, vb, **ctx_only),
                _attention_plan("gqa", qc_c, kc, vc, sink=win_sink[l], **ctx_only),
                _attention_plan("mla", qd_c, kd, vd, **ctx_only),
            ], bsz, tc)
            outs_c = [o.reshape(n_ctx, 256) for o in outs_c]
            x1c, h2x, ids = _out_proj(alpha, outs_c, xc, g1c, sc2c, sh2c, lw, tc, n_moe, n_lat, prev=(h2x, ids))
        y = _moe(l, h2x, ids, moe_w_gate, moe_w_up, moe_w_down, MOE_CHUNKS,
                 (n_lat, n_ctx) if with_ctx else (n_lat,))
        fin_l = (alpha, x1, y, g2, ln2g, ln2b, 0)
        if with_ctx:
            fin_c = (alpha, x1c, y, g2c, ln2g, ln2b, n_lat)

    return _finish(*fin_l[:6], t).reshape(bsz, t, d)
```
